```python
import math
import jax, jax.numpy as jnp
from jax import lax
import numpy as np

D_MODEL = 2048
BATCH = 2
SEQ = 8192
DEPTH = 1

HEAD_DIM = 64
SWA_Q_HEADS = 16
SWA_KV_HEADS = 4
SWA_WINDOW = 128
SWA_WIDTH = SWA_Q_HEADS * HEAD_DIM
SWA_KV_WIDTH = SWA_KV_HEADS * HEAD_DIM
DIFF_HEADS = 8
DIFF_WIDTH = DIFF_HEADS * 2 * HEAD_DIM
N_BRANCH = 2
BRANCH_WIDTH = 1024
IN_SPLIT_SIZES = [SWA_WIDTH, SWA_KV_WIDTH, SWA_KV_WIDTH, DIFF_WIDTH, DIFF_WIDTH, DIFF_WIDTH, N_BRANCH * D_MODEL]
IN_WIDTH = int(sum(IN_SPLIT_SIZES))
IN_SPLIT_IDX = [int(v) for v in np.cumsum(IN_SPLIT_SIZES)[:-1]]
MEM_LEN = 256
X_HEADS = 4
X_HEAD_DIM = 128
X_WIDTH = X_HEADS * X_HEAD_DIM
N_EXPERTS = 32
TOP_K = 4
D_FF = 2048
SWIGLU_LIMIT = 7.0
SWIGLU_ALPHA = 1.702
BLOCK = 128
RMS_EPS = 1e-5

kernel_name = "hybrid_swa_sink_diffattn_alibi_memxattn_moe"


def rms_norm(x, g):
    xf = x.astype(jnp.float32)
    y = xf * lax.rsqrt(jnp.mean(xf * xf, axis=-1, keepdims=True) + RMS_EPS)
    return (y * g.astype(jnp.float32)).astype(x.dtype)


def alibi_slopes(n_heads):
    return 2.0 ** (-8.0 * jnp.arange(1, n_heads + 1, dtype=jnp.float32) / n_heads)


def lambda_init_fn(layer_idx):
    return 0.8 - 0.6 * math.exp(-0.3 * layer_idx)


def sliding_window_attention(q, k, v, sinks):
    B, S, Hq, d = q.shape
    Hkv = k.shape[2]
    G = Hq // Hkv
    nb = S // BLOCK
    qb = q.reshape(B, nb, BLOCK, Hkv, G, d)

    def band(t):
        tb = t.reshape(B, nb, BLOCK, Hkv, d)
        prev = jnp.pad(tb, ((0, 0), (1, 0), (0, 0), (0, 0), (0, 0)))[:, :-1]
        return jnp.concatenate([prev, tb], axis=2)

    kb, vb = band(k), band(v)
    scores = jnp.einsum('bnqhgd,bnkhd->bnhgqk', qb, kb).astype(jnp.float32) * (d ** -0.5)
    q_pos = jnp.arange(BLOCK)[:, None] + BLOCK
    k_pos = jnp.arange(2 * BLOCK)[None, :]
    dist = q_pos - k_pos
    allowed = (dist >= 0) & (dist < SWA_WINDOW)
    blk = jnp.arange(nb)[:, None, None]
    valid = allowed[None] & ((k_pos[None] >= BLOCK) | (blk > 0))
    slopes = alibi_slopes(Hq).reshape(Hkv, G)
    bias = -slopes[:, :, None, None] * dist.astype(jnp.float32)[None, None]
    scores = jnp.where(valid[None, :, None, None], scores + bias[None, None], -jnp.inf)
    sink = sinks.astype(jnp.float32).reshape(1, 1, Hkv, G, 1, 1)
    m = jnp.maximum(jnp.max(scores, axis=-1, keepdims=True), sink)
    p = jnp.exp(scores - m)
    p = p / (jnp.sum(p, axis=-1, keepdims=True) + jnp.exp(sink - m))
    out = jnp.einsum('bnhgqk,bnkhd->bnqhgd', p.astype(v.dtype), vb)
    return out.reshape(B, S, Hq * d)


def differential_attention(q, k, v, lam, subln_g, lam_init):
    B, S, H, _, d = q.shape
    nb = S // BLOCK
    slopes = alibi_slopes(H)
    k_pos = jnp.arange(S)

    def query_block(i):
        q_blk = lax.dynamic_slice_in_dim(q, i * BLOCK, BLOCK, axis=1)
        s = jnp.einsum('bqhcd,bkhcd->bhcqk', q_blk, k).astype(jnp.float32) * (d ** -0.5)
        q_pos = i * BLOCK + jnp.arange(BLOCK)
        dist = q_pos[:, None] - k_pos[None, :]
        bias = -slopes[:, None, None, None] * dist.astype(jnp.float32)[None, None]
        s = jnp.where(dist >= 0, s + bias, -jnp.inf)
        a = jax.nn.softmax(s, axis=-1)
        a = a[:, :, 0] - lam * a[:, :, 1]
        return jnp.einsum('bhqk,bkhe->bqhe', a.astype(v.dtype), v)

    out = lax.map(query_block, jnp.arange(nb))
    out = jnp.moveaxis(out, 0, 1).reshape(B, S, H, 2 * d)
    out = rms_norm(out, subln_g) * (1.0 - lam_init)
    return out.reshape(B, S, H * 2 * d)


def memory_cross_attention(hn, mem_n, w_cq, w_ckv, w_co):
    B, S, _ = hn.shape
    M = mem_n.shape[1]
    q = (hn @ w_cq).reshape(B, S, X_HEADS, X_HEAD_DIM)
    k, v = jnp.split(mem_n @ w_ckv, 2, axis=-1)
    k = k.reshape(B, M, X_HEADS, X_HEAD_DIM)
    v = v.reshape(B, M, X_HEADS, X_HEAD_DIM)
    s = jnp.einsum('bqhd,bkhd->bhqk', q, k).astype(jnp.float32) * (X_HEAD_DIM ** -0.5)
    p = jax.nn.softmax(s, axis=-1)
    o = jnp.einsum('bhqk,bkhd->bqhd', p.astype(v.dtype), v).reshape(B, S, X_WIDTH)
    return o @ w_co


def moe_ffn(xn, w_router, b_router, w_gate_up, b_gate_up, w_down, b_down):
    B, S, D = xn.shape
    T = B * S
    xt = xn.reshape(T, D)
    logits = (xt @ w_router + b_router).astype(jnp.float32)
    top_val, top_idx = lax.top_k(logits, TOP_K)
    gates = jax.nn.softmax(top_val, axis=-1)
    flat_e = top_idx.reshape(-1)
    flat_tok = jnp.arange(T * TOP_K, dtype=jnp.int32) // TOP_K
    flat_w = gates.reshape(-1)
    order = jnp.argsort(flat_e)
    sorted_e = flat_e[order]
    counts = jnp.bincount(flat_e, length=N_EXPERTS)
    padded = (counts + BLOCK - 1) // BLOCK * BLOCK
    start = jnp.cumsum(counts) - counts
    pstart = jnp.cumsum(padded) - padded
    dest = pstart[sorted_e] + (jnp.arange(T * TOP_K) - start[sorted_e])
    n_rows = T * TOP_K + N_EXPERTS * BLOCK
    n_blocks = n_rows // BLOCK
    row_tok = jnp.zeros((n_rows,), jnp.int32).at[dest].set(flat_tok[order])
    row_w = jnp.zeros((n_rows,), jnp.float32).at[dest].set(flat_w[order])
    block_e = jnp.minimum(
        jnp.searchsorted(jnp.cumsum(padded), jnp.arange(n_blocks) * BLOCK, side='right'),
        N_EXPERTS - 1)

    def expert_block(args):
        e, toks = args
        xb = xt[toks]
        hgu = xb @ w_gate_up[e] + b_gate_up[e]
        glu, lin = jnp.split(hgu, 2, axis=-1)
        glu = jnp.minimum(glu, SWIGLU_LIMIT)
        lin = jnp.clip(lin, -SWIGLU_LIMIT, SWIGLU_LIMIT)
        act = glu * jax.nn.sigmoid(SWIGLU_ALPHA * glu) * (lin + 1.0)
        return act @ w_down[e] + b_down[e]

    out = lax.map(expert_block, (block_e, row_tok.reshape(n_blocks, BLOCK)))
    out = out.reshape(n_rows, D) * row_w[:, None].astype(out.dtype)
    y = jnp.zeros((T, D), out.dtype).at[row_tok].add(out)
    return y.reshape(B, S, D)


def setup_inputs(seed: int = 0) -> dict:
    key = jax.random.key(seed)
    ks = jax.random.split(key, 32)
    L, D = DEPTH, D_MODEL

    def nrm(k, shape, scale):
        return jax.random.normal(k, shape, jnp.float32) * scale

    def gain(k, shape):
        return 1.0 + nrm(k, shape, 0.02)

    return {
        "x": nrm(ks[0], (BATCH, SEQ, D), 1.0),
        "mem": nrm(ks[1], (BATCH, MEM_LEN, D), 1.0),
        "norm_mix_g": gain(ks[2], (L, D)),
        "w_in": nrm(ks[3], (L, D, IN_WIDTH), D ** -0.5),
        "sinks": nrm(ks[4], (L, SWA_Q_HEADS), 0.5),
        "lambda_q1": nrm(ks[5], (L, HEAD_DIM), 0.1),
        "lambda_k1": nrm(ks[6], (L, HEAD_DIM), 0.1),
        "lambda_q2": nrm(ks[7], (L, HEAD_DIM), 0.1),
        "lambda_k2": nrm(ks[8], (L, HEAD_DIM), 0.1),
        "diff_subln_g": gain(ks[9], (L, 2 * HEAD_DIM)),
        "w_branch": nrm(ks[10], (L, N_BRANCH, BRANCH_WIDTH, D), BRANCH_WIDTH ** -0.5),
        "w_o": nrm(ks[11], (L, D, D), D ** -0.5),
        "norm_cross_g": gain(ks[12], (L, D)),
        "norm_mem_g": gain(ks[13], (L, D)),
        "w_cq": nrm(ks[14], (L, D, X_WIDTH), D ** -0.5),
        "w_ckv": nrm(ks[15], (L, D, 2 * X_WIDTH), D ** -0.5),
        "w_co": nrm(ks[16], (L, X_WIDTH, D), X_WIDTH ** -0.5),
        "norm_ffn_g": gain(ks[17], (L, D)),
        "w_router": nrm(ks[18], (L, D, N_EXPERTS), D ** -0.5),
        "b_router": nrm(ks[19], (L, N_EXPERTS), 0.01),
        "w_gate_up": nrm(ks[20], (L, N_EXPERTS, D, 2 * D_FF), D ** -0.5),
        "b_gate_up": nrm(ks[21], (L, N_EXPERTS, 2 * D_FF), 0.01),
        "w_down": nrm(ks[22], (L, N_EXPERTS, D_FF, D), D_FF ** -0.5),
        "b_down": nrm(ks[23], (L, N_EXPERTS, D), 0.01),
        "norm_final_g": gain(ks[24], (D,)),
    }


def reference(x, mem, norm_mix_g, w_in, sinks, lambda_q1, lambda_k1, lambda_q2, lambda_k2,
              diff_subln_g, w_branch, w_o, norm_cross_g, norm_mem_g, w_cq, w_ckv, w_co,
              norm_ffn_g, w_router, b_router, w_gate_up, b_gate_up, w_down, b_down, norm_final_g):
    B, S, D = x.shape
    h = x
    for l in range(DEPTH):
        xn = rms_norm(h, norm_mix_g[l])
        proj = xn @ w_in[l]
        qa, ka, va, qd, kd, vd, g = jnp.split(proj, IN_SPLIT_IDX, axis=-1)
        ya = sliding_window_attention(
            qa.reshape(B, S, SWA_Q_HEADS, HEAD_DIM),
            ka.reshape(B, S, SWA_KV_HEADS, HEAD_DIM),
            va.reshape(B, S, SWA_KV_HEADS, HEAD_DIM),
            sinks[l])
        lam_init = lambda_init_fn(l)
        lam = (jnp.exp(jnp.sum(lambda_q1[l].astype(jnp.float32) * lambda_k1[l].astype(jnp.float32)))
               - jnp.exp(jnp.sum(lambda_q2[l].astype(jnp.float32) * lambda_k2[l].astype(jnp.float32)))
               + lam_init)
        yd = differential_attention(
            qd.reshape(B, S, DIFF_HEADS, 2, HEAD_DIM),
            kd.reshape(B, S, DIFF_HEADS, 2, HEAD_DIM),
            vd.reshape(B, S, DIFF_HEADS, 2 * HEAD_DIM),
            lam, diff_subln_g[l], lam_init)
        branch_gates = jax.nn.sigmoid(g.reshape(B, S, N_BRANCH, D))
        branches = jnp.stack([ya, yd], axis=2)
        up = jnp.einsum('bsnw,nwd->bsnd', branches, w_branch[l])
        merged = jnp.sum(branch_gates * up, axis=2)
        h = h + merged @ w_o[l]
        h = h + memory_cross_attention(rms_norm(h, norm_cross_g[l]), rms_norm(mem, norm_mem_g[l]),
                                       w_cq[l], w_ckv[l], w_co[l])
        h = h + moe_ffn(rms_norm(h, norm_ffn_g[l]), w_router[l], b_router[l],
                        w_gate_up[l], b_gate_up[l], w_down[l], b_down[l])
    return rms_norm(h, norm_final_g)
```

```python
import functools
import math

import jax
import jax.numpy as jnp
import numpy as np
from jax import lax
from jax.experimental import pallas as pl
from jax.experimental.pallas import tpu as pltpu

BF16 = jnp.bfloat16
F32 = jnp.float32
NEG_INF = float("-inf")

HEAD_DIM = 64
SWA_Q_HEADS = 16
SWA_KV_HEADS = 4
SWA_GROUP = SWA_Q_HEADS // SWA_KV_HEADS
SWA_BLOCK = 128
SWA_WIDTH = SWA_Q_HEADS * HEAD_DIM
SWA_KV_WIDTH = SWA_KV_HEADS * HEAD_DIM
DIFF_HEADS = 8
DIFF_HEAD_WIDTH = 2 * HEAD_DIM
DIFF_WIDTH = DIFF_HEADS * DIFF_HEAD_WIDTH
X_HEADS = 4
X_HEAD_DIM = 128
X_WIDTH = X_HEADS * X_HEAD_DIM
N_EXPERTS = 32
TOP_K = 4
SWIGLU_LIMIT = 7.0
SWIGLU_ALPHA = 1.702
RMS_EPS = 1e-5
LAMBDA_INIT = 0.8 - 0.6 * math.exp(-0.3 * 0)

V7X_VMEM_BYTES = 64 * 1024 * 1024
VMEM_LIMIT = 56 * 1024 * 1024

INPROJ_TM = 1024
INPROJ_TN = 512
DIFF_T = 512
MERGE_TM = 256
CROSS_TM = 256
DISPATCH_TM = 256
EXPERT_ROWS = 1024
EXPERT_TF = 256
EXPERT_SUB = 256
COMBINE_TM = 128


def _cparams(semantics):
    return pltpu.CompilerParams(dimension_semantics=semantics, vmem_limit_bytes=VMEM_LIMIT)


def _rms(x, g):
    ms = jnp.mean(x * x, axis=-1, keepdims=True)
    return x * lax.rsqrt(ms + RMS_EPS) * g


def _inproj_kernel(x_ref, g_ref, w_ref, o_ref, xn_ref):
    @pl.when(pl.program_id(1) == 0)
    def _():
        xn_ref[...] = _rms(x_ref[...], g_ref[...]).astype(BF16)

    o_ref[...] = jnp.dot(xn_ref[...], w_ref[...], preferred_element_type=F32).astype(o_ref.dtype)


def _inproj(x2, g, w):
    t, d = x2.shape
    n = w.shape[1]
    tm, tn = min(INPROJ_TM, t), INPROJ_TN
    return pl.pallas_call(
        _inproj_kernel,
        grid=(t // tm, n // tn),
        in_specs=[
            pl.BlockSpec((tm, d), lambda i, j: (i, 0)),
            pl.BlockSpec((1, d), lambda i, j: (0, 0)),
            pl.BlockSpec((d, tn), lambda i, j: (0, j)),
        ],
        out_specs=pl.BlockSpec((tm, tn), lambda i, j: (i, j)),
        out_shape=jax.ShapeDtypeStruct((t, n), BF16),
        scratch_shapes=[pltpu.VMEM((tm, d), BF16)],
        compiler_params=_cparams(("parallel", "arbitrary")),
        name="inproj",
    )(x2, g, w)


def _swa_kernel(qT_ref, kp_ref, kc_ref, vp_ref, vc_ref, bias_ref, sink_ref, o_ref):
    n = pl.program_id(1)
    blk = SWA_BLOCK
    kband = jnp.concatenate([kp_ref[...], kc_ref[...]], axis=0)
    vband = jnp.concatenate([vp_ref[...], vc_ref[...]], axis=1)
    krow = lax.broadcasted_iota(jnp.int32, (2 * blk, SWA_GROUP * blk), 0)
    has_prev = n > 0
    zeros = jnp.zeros((HEAD_DIM, blk), BF16)
    for h in range(SWA_KV_HEADS):
        lane0 = 2 * HEAD_DIM * (h // 2)
        k128 = kband[:, lane0:lane0 + 2 * HEAD_DIM]
        pieces = []
        for g in range(SWA_GROUP):
            hq = SWA_GROUP * h + g
            qh = qT_ref[HEAD_DIM * hq:HEAD_DIM * (hq + 1), :]
            pieces.append(jnp.concatenate([qh, zeros] if h % 2 == 0 else [zeros, qh], axis=0))
        qz = jnp.concatenate(pieces, axis=1)
        s = jnp.dot(k128, qz, preferred_element_type=F32) * (HEAD_DIM ** -0.5)
        s = s + bias_ref[h]
        s = jnp.where((krow >= blk) | has_prev, s, NEG_INF)
        sink = sink_ref[h]
        m = jnp.maximum(jnp.max(s, axis=0, keepdims=True), sink)
        p = jnp.exp(s - m)
        denom = jnp.sum(p, axis=0, keepdims=True) + jnp.exp(sink - m)
        vh = vband[HEAD_DIM * h:HEAD_DIM * (h + 1), :]
        o = jnp.dot(vh, p.astype(BF16), preferred_element_type=F32) / denom
        for g in range(SWA_GROUP):
            hq = SWA_GROUP * h + g
            o_ref[HEAD_DIM * hq:HEAD_DIM * (hq + 1), :] = o[:, blk * g:blk * (g + 1)].astype(o_ref.dtype)


def _swa_tables(sinks):
    blk = SWA_BLOCK
    slopes = (2.0 ** (-8.0 * jnp.arange(1, SWA_Q_HEADS + 1, dtype=F32) / SWA_Q_HEADS)).reshape(SWA_KV_HEADS, SWA_GROUP)
    q_pos = jnp.arange(blk)[None, :] + blk
    k_pos = jnp.arange(2 * blk)[:, None]
    dist = q_pos - k_pos
    allowed = (dist >= 0) & (dist < SWA_BLOCK)
    bias = -slopes[:, :, None, None] * dist.astype(F32)[None, None]
    bias = jnp.where(allowed[None, None], bias, NEG_INF)
    bias = jnp.transpose(bias, (0, 2, 1, 3)).reshape(SWA_KV_HEADS, 2 * blk, SWA_GROUP * blk)
    sink = sinks.astype(F32).reshape(SWA_KV_HEADS, SWA_GROUP, 1)
    sink = jnp.broadcast_to(sink, (SWA_KV_HEADS, SWA_GROUP, blk)).reshape(SWA_KV_HEADS, 1, SWA_GROUP * blk)
    return bias, sink


def _swa(qT, k, vT, sinks):
    b, _, s = qT.shape
    blk = SWA_BLOCK
    bias, sink = _swa_tables(sinks)
    prev = lambda n: jnp.maximum(n - 1, 0)
    return pl.pallas_call(
        _swa_kernel,
        grid=(b, s // blk),
        in_specs=[
            pl.BlockSpec((None, SWA_WIDTH, blk), lambda bi, n: (bi, 0, n)),
            pl.BlockSpec((None, blk, SWA_KV_WIDTH), lambda bi, n: (bi, prev(n), 0)),
            pl.BlockSpec((None, blk, SWA_KV_WIDTH), lambda bi, n: (bi, n, 0)),
            pl.BlockSpec((None, SWA_KV_WIDTH, blk), lambda bi, n: (bi, 0, prev(n))),
            pl.BlockSpec((None, SWA_KV_WIDTH, blk), lambda bi, n: (bi, 0, n)),
            pl.BlockSpec(bias.shape, lambda bi, n: (0, 0, 0)),
            pl.BlockSpec(sink.shape, lambda bi, n: (0, 0, 0)),
        ],
        out_specs=pl.BlockSpec((None, SWA_WIDTH, blk), lambda bi, n: (bi, 0, n)),
        out_shape=jax.ShapeDtypeStruct((b, SWA_WIDTH, s), BF16),
        compiler_params=_cparams(("parallel", "arbitrary")),
        name="swa",
    )(qT, k, k, vT, vT, bias, sink)


def _diff_kernel(slope_ref, qT_ref, k_ref, v_ref, lamv_ref, g_ref, o_ref,
                 q2_ref, rb_ref, m_ref, l_ref, acc_ref, *, t):
    h = pl.program_id(1)
    qi = pl.program_id(2)
    slope = slope_ref[h]

    @pl.when(qi == 0)
    def _():
        rb_ref[...] = slope * lax.broadcasted_iota(jnp.int32, rb_ref.shape, 0).astype(F32)

    qt = qT_ref[...]
    row = lax.broadcasted_iota(jnp.int32, qt.shape, 0)
    zero = jnp.zeros_like(qt)
    q2_ref[:, :t] = jnp.where(row < HEAD_DIM, qt, zero)
    q2_ref[:, t:] = jnp.where(row >= HEAD_DIM, qt, zero)
    m_ref[...] = jnp.full(m_ref.shape, NEG_INF, F32)
    l_ref[...] = jnp.zeros(l_ref.shape, F32)
    acc_ref[...] = jnp.zeros(acc_ref.shape, F32)

    def step(ki, masked):
        s = jnp.dot(k_ref[ki], q2_ref[...], preferred_element_type=F32) * (HEAD_DIM ** -0.5)
        z = s + rb_ref[...]
        if masked:
            kr = lax.broadcasted_iota(jnp.int32, z.shape, 0)
            qc = lax.broadcasted_iota(jnp.int32, z.shape, 1)
            qc = jnp.where(qc >= t, qc - t, qc)
            z = jnp.where(kr <= qc, z, NEG_INF)
        c = slope * ((ki - qi) * t).astype(F32)
        m_old = m_ref[...]
        m_new = jnp.maximum(m_old, jnp.max(z, axis=0, keepdims=True) + c)
        p = jnp.exp(z - (m_new - c))
        alpha = jnp.exp(m_old - m_new)
        l_ref[...] = alpha * l_ref[...] + jnp.sum(p, axis=0, keepdims=True)
        pv = jnp.dot(v_ref[ki], p.astype(BF16), preferred_element_type=F32)
        acc_ref[...] = alpha * acc_ref[...] + pv
        m_ref[...] = m_new

    def body(ki, carry):
        step(ki, False)
        return carry

    lax.fori_loop(0, qi, body, 0)
    step(qi, True)

    lamv = lamv_ref[...]
    lam = (jnp.exp(jnp.sum(lamv[0:1] * lamv[1:2], axis=-1, keepdims=True))
           - jnp.exp(jnp.sum(lamv[2:3] * lamv[3:4], axis=-1, keepdims=True)) + LAMBDA_INIT)
    acc = acc_ref[...]
    l = l_ref[...]
    o = acc[:, :t] / l[:, :t] - lam * (acc[:, t:] / l[:, t:])
    ms = jnp.mean(o * o, axis=0, keepdims=True)
    y = o * lax.rsqrt(ms + RMS_EPS) * g_ref[...] * (1.0 - LAMBDA_INIT)
    o_ref[...] = y.astype(o_ref.dtype)


def _diff(qT, k5, v5, lamv, subln_g):
    b, _, s = qT.shape
    t = k5.shape[3]
    nk = s // t
    slopes = 2.0 ** (-8.0 * jnp.arange(1, DIFF_HEADS + 1, dtype=F32) / DIFF_HEADS)
    w = DIFF_HEAD_WIDTH
    return pl.pallas_call(
        functools.partial(_diff_kernel, t=t),
        grid=(b, DIFF_HEADS, nk),
        in_specs=[
            pl.BlockSpec(memory_space=pltpu.SMEM),
            pl.BlockSpec((None, w, t), lambda bi, h, qi: (bi, h, qi)),
            pl.BlockSpec((None, None, nk, t, w), lambda bi, h, qi: (bi, h, 0, 0, 0)),
            pl.BlockSpec((None, None, nk, w, t), lambda bi, h, qi: (bi, h, 0, 0, 0)),
            pl.BlockSpec((4, HEAD_DIM), lambda bi, h, qi: (0, 0)),
            pl.BlockSpec((w, 1), lambda bi, h, qi: (0, 0)),
        ],
        out_specs=pl.BlockSpec((None, w, t), lambda bi, h, qi: (bi, h, qi)),
        out_shape=jax.ShapeDtypeStruct((b, DIFF_WIDTH, s), BF16),
        scratch_shapes=[
            pltpu.VMEM((w, 2 * t), BF16),
            pltpu.VMEM((t, 2 * t), F32),
            pltpu.VMEM((1, 2 * t), F32),
            pltpu.VMEM((1, 2 * t), F32),
            pltpu.VMEM((w, 2 * t), F32),
        ],
        compiler_params=_cparams(("arbitrary", "arbitrary", "arbitrary")),
        name="diffattn",
    )(slopes, qT, k5, v5, lamv, subln_g)


def _merge_kernel(ya_ref, yd_ref, g0_ref, g1_ref, x_ref, wb0_ref, wb1_ref, wo_ref, o_ref):
    ua = jnp.dot(ya_ref[...], wb0_ref[...], preferred_element_type=F32)
    ud = jnp.dot(yd_ref[...], wb1_ref[...], preferred_element_type=F32)
    merged = jax.nn.sigmoid(g0_ref[...].astype(F32)) * ua + jax.nn.sigmoid(g1_ref[...].astype(F32)) * ud
    o_ref[...] = x_ref[...] + jnp.dot(merged.astype(BF16), wo_ref[...], preferred_element_type=F32)


def _merge(ya, yd, proj, x2, wb, wo):
    t, d = x2.shape
    tm = min(MERGE_TM, t)
    bw = ya.shape[1]
    const = dict(pipeline_mode=pl.Buffered(1))
    return pl.pallas_call(
        _merge_kernel,
        grid=(t // tm,),
        in_specs=[
            pl.BlockSpec((tm, bw), lambda i: (i, 0)),
            pl.BlockSpec((tm, bw), lambda i: (i, 0)),
            pl.BlockSpec((tm, d), lambda i: (i, 0)),
            pl.BlockSpec((tm, d), lambda i: (i, 1)),
            pl.BlockSpec((tm, d), lambda i: (i, 0)),
            pl.BlockSpec((None, bw, d), lambda i: (0, 0, 0), **const),
            pl.BlockSpec((None, bw, d), lambda i: (1, 0, 0), **const),
            pl.BlockSpec((d, d), lambda i: (0, 0), **const),
        ],
        out_specs=pl.BlockSpec((tm, d), lambda i: (i, 0)),
        out_shape=jax.ShapeDtypeStruct((t, d), F32),
        compiler_params=_cparams(("parallel",)),
        name="merge",
    )(ya, yd, proj, proj, x2, wb, wb, wo)


def _memkv_kernel(mem_ref, g_ref, w_ref, k_ref, v_ref):
    mn = _rms(mem_ref[...], g_ref[...]).astype(BF16)
    kv = jnp.dot(mn, w_ref[...], preferred_element_type=F32)
    k_ref[...] = kv[:, :X_WIDTH].astype(k_ref.dtype)
    v_ref[...] = kv[:, X_WIDTH:].astype(v_ref.dtype)


def _memkv(mem, g, w):
    b, m, d = mem.shape
    out = jax.ShapeDtypeStruct((b, m, X_WIDTH), BF16)
    return pl.pallas_call(
        _memkv_kernel,
        grid=(b,),
        in_specs=[
            pl.BlockSpec((None, m, d), lambda i: (i, 0, 0)),
            pl.BlockSpec((1, d), lambda i: (0, 0)),
            pl.BlockSpec((d, 2 * X_WIDTH), lambda i: (0, 0)),
        ],
        out_specs=[pl.BlockSpec((None, m, X_WIDTH), lambda i: (i, 0, 0))] * 2,
        out_shape=[out, out],
        compiler_params=_cparams(("parallel",)),
        name="memkv",
    )(mem, g, w)


def _cross_router_kernel(h_ref, k_ref, v_ref, gc_ref, wq_ref, wo_ref, gf_ref, wr_ref, br_ref,
                         h2_ref, xn_ref, idx_ref, gate_ref, rank_ref, cnt_ref, run_ref):
    i = pl.program_id(0)
    tm = h_ref.shape[0]

    @pl.when(i == 0)
    def _():
        run_ref[...] = jnp.zeros(run_ref.shape, F32)

    h = h_ref[...]
    hn = _rms(h, gc_ref[...]).astype(BF16)
    q = jnp.dot(hn, wq_ref[...], preferred_element_type=F32).astype(BF16)
    k = k_ref[...]
    v = v_ref[...]
    outs = []
    for hd in range(X_HEADS):
        sl = slice(X_HEAD_DIM * hd, X_HEAD_DIM * (hd + 1))
        s = lax.dot_general(q[:, sl], k[:, sl], (((1,), (1,)), ((), ())),
                            preferred_element_type=F32) * (X_HEAD_DIM ** -0.5)
        s = s - jnp.max(s, axis=-1, keepdims=True)
        p = jnp.exp(s)
        p = p / jnp.sum(p, axis=-1, keepdims=True)
        outs.append(jnp.dot(p.astype(BF16), v[:, sl], preferred_element_type=F32))
    o = jnp.concatenate(outs, axis=1).astype(BF16)
    h2 = h + jnp.dot(o, wo_ref[...], preferred_element_type=F32)
    h2_ref[...] = h2

    xn = _rms(h2, gf_ref[...])
    xn_ref[...] = xn
    logits = jnp.dot(xn.astype(BF16), wr_ref[...], preferred_element_type=F32) + br_ref[...]

    lane = lax.broadcasted_iota(jnp.int32, logits.shape, 1).astype(F32)
    work = logits
    vals, idxs = [], []
    onehot = jnp.zeros(logits.shape, F32)
    for _ in range(TOP_K):
        mx = jnp.max(work, axis=-1, keepdims=True)
        ix = jnp.min(jnp.where(work == mx, lane, float(N_EXPERTS)), axis=-1, keepdims=True)
        sel = lane == ix
        vals.append(mx)
        idxs.append(ix)
        onehot = jnp.where(sel, 1.0, onehot)
        work = jnp.where(sel, NEG_INF, work)
    es = [jnp.exp(vv - vals[0]) for vv in vals]
    tot = es[0] + es[1] + es[2] + es[3]

    r = lax.broadcasted_iota(jnp.int32, (tm, tm), 0)
    c = lax.broadcasted_iota(jnp.int32, (tm, tm), 1)
    lower = jnp.where(c < r, 1.0, 0.0).astype(BF16)
    prefix = jnp.dot(lower, onehot.astype(BF16), preferred_element_type=F32) + run_ref[...]
    for kk in range(TOP_K):
        idx_ref[:, kk:kk + 1] = idxs[kk].astype(jnp.int32)
        gate_ref[:, kk:kk + 1] = es[kk] / tot
        rk = jnp.sum(jnp.where(lane == idxs[kk], prefix, 0.0), axis=-1, keepdims=True)
        rank_ref[:, kk:kk + 1] = rk.astype(jnp.int32)
    run_ref[...] = run_ref[...] + jnp.sum(onehot, axis=0, keepdims=True)
    cnt_ref[...] = run_ref[...].astype(jnp.int32)


def _cross_router(h1, kmem, vmem, gc, wq, wo, gf, wr, br, seq):
    t, d = h1.shape
    tm = min(CROSS_TM, seq)
    per_b = seq // tm
    m = kmem.shape[1]
    const = dict(pipeline_mode=pl.Buffered(1))
    outs = pl.pallas_call(
        _cross_router_kernel,
        grid=(t // tm,),
        in_specs=[
            pl.BlockSpec((tm, d), lambda i: (i, 0)),
            pl.BlockSpec((None, m, X_WIDTH), lambda i: (i // per_b, 0, 0)),
            pl.BlockSpec((None, m, X_WIDTH), lambda i: (i // per_b, 0, 0)),
            pl.BlockSpec((1, d), lambda i: (0, 0)),
            pl.BlockSpec((d, X_WIDTH), lambda i: (0, 0), **const),
            pl.BlockSpec((X_WIDTH, d), lambda i: (0, 0), **const),
            pl.BlockSpec((1, d), lambda i: (0, 0)),
            pl.BlockSpec((d, N_EXPERTS), lambda i: (0, 0), **const),
            pl.BlockSpec((1, N_EXPERTS), lambda i: (0, 0)),
        ],
        out_specs=[
            pl.BlockSpec((tm, d), lambda i: (i, 0)),
            pl.BlockSpec((tm, d), lambda i: (i, 0)),
            pl.BlockSpec((tm, TOP_K), lambda i: (i, 0)),
            pl.BlockSpec((tm, TOP_K), lambda i: (i, 0)),
            pl.BlockSpec((tm, TOP_K), lambda i: (i, 0)),
            pl.BlockSpec((1, N_EXPERTS), lambda i: (0, 0)),
        ],
        out_shape=[
            jax.ShapeDtypeStruct((t, d), F32),
            jax.ShapeDtypeStruct((t, d), F32),
            jax.ShapeDtypeStruct((t, TOP_K), jnp.int32),
            jax.ShapeDtypeStruct((t, TOP_K), F32),
            jax.ShapeDtypeStruct((t, TOP_K), jnp.int32),
            jax.ShapeDtypeStruct((1, N_EXPERTS), jnp.int32),
        ],
        scratch_shapes=[pltpu.VMEM((1, N_EXPERTS), F32)],
        compiler_params=_cparams(("arbitrary",)),
        name="cross_router",
    )(h1, kmem, vmem, gc, wq, wo, gf, wr, br)
    return outs


def _dispatch_kernel(pos_ref, x_ref, xs_in_ref, xs_ref, sem):
    del xs_in_ref
    tm = x_ref.shape[0]

    def row_copy(r, p):
        return pltpu.make_async_copy(x_ref.at[pl.ds(r, 1)], xs_ref.at[pl.ds(p, 1)], sem)

    def issue(r, carry):
        for kk in range(TOP_K):
            row_copy(r, pos_ref[0, 0, r * TOP_K + kk]).start()
        return carry

    lax.fori_loop(0, tm, issue, 0)

    def drain(r, carry):
        for kk in range(TOP_K):
            row_copy(r, pos_ref[0, 0, r * TOP_K + kk]).wait()
        return carry

    lax.fori_loop(0, tm, drain, 0)


def _dispatch(xn, pos, n_rows):
    t, d = xn.shape
    tm = min(DISPATCH_TM, t)
    pos3 = pos.reshape(t // tm, 1, tm * TOP_K)
    xs0 = jnp.zeros((n_rows, d), xn.dtype)
    return pl.pallas_call(
        _dispatch_kernel,
        grid=(t // tm,),
        in_specs=[
            pl.BlockSpec((1, 1, tm * TOP_K), lambda i: (i, 0, 0), memory_space=pltpu.SMEM),
            pl.BlockSpec((tm, d), lambda i: (i, 0)),
            pl.BlockSpec(memory_space=pl.ANY),
        ],
        out_specs=pl.BlockSpec(memory_space=pl.ANY),
        out_shape=jax.ShapeDtypeStruct((n_rows, d), xn.dtype),
        scratch_shapes=[pltpu.SemaphoreType.DMA(())],
        input_output_aliases={2: 0},
        compiler_params=_cparams(("arbitrary",)),
        name="dispatch",
    )(pos3, xn, xs0)


def _expert_kernel(te_ref, tr_ref, tb_ref, x_ref, wg_ref, wl_ref, bg_ref, bl_ref, wd_ref, bd_ref, o_ref,
                   wgb_ref, wlb_ref, wdb_ref):
    del te_ref, tb_ref
    i = pl.program_id(0)
    j = pl.program_id(1)
    rows = tr_ref[i]
    sub = EXPERT_SUB

    @pl.when(rows > 0)
    def _():
        wgb_ref[...] = wg_ref[...].astype(BF16)
        wlb_ref[...] = wl_ref[...].astype(BF16)
        wdb_ref[...] = wd_ref[...].astype(BF16)

    for sb in range(x_ref.shape[0] // sub):
        sl = pl.ds(sb * sub, sub)

        @pl.when(sb * sub < rows)
        def _():
            xb = x_ref[sl, :].astype(BF16)
            glu = jnp.dot(xb, wgb_ref[...], preferred_element_type=F32) + bg_ref[...]
            lin = jnp.dot(xb, wlb_ref[...], preferred_element_type=F32) + bl_ref[...]
            glu = jnp.minimum(glu, SWIGLU_LIMIT)
            lin = jnp.clip(lin, -SWIGLU_LIMIT, SWIGLU_LIMIT)
            act = glu * jax.nn.sigmoid(SWIGLU_ALPHA * glu) * (lin + 1.0)
            part = jnp.dot(act.astype(BF16), wdb_ref[...], preferred_element_type=F32)

            @pl.when(j == 0)
            def _():
                o_ref[sl, :] = part + bd_ref[...]

            @pl.when(j > 0)
            def _():
                o_ref[sl, :] = o_ref[sl, :] + part

        @pl.when((sb * sub >= rows) & (j == 0))
        def _():
            o_ref[sl, :] = jnp.zeros((sub, o_ref.shape[1]), F32)


def _experts(xs, tile_expert, tile_rows, tile_blk, w_gu, b_gu, w_down, b_down):
    n_rows, d = xs.shape
    e, _, ff2 = w_gu.shape
    ff = ff2 // 2
    r, tf = EXPERT_ROWS, EXPERT_TF
    nj = ff // tf
    b_gu3 = b_gu.reshape(e, 1, ff2)
    b_down3 = b_down.reshape(e, 1, d)

    def jj(i, j, tr):
        return jnp.where(tr[i] > 0, j, nj - 1)

    grid_spec = pltpu.PrefetchScalarGridSpec(
        num_scalar_prefetch=3,
        grid=(n_rows // r, nj),
        in_specs=[
            pl.BlockSpec((r, d), lambda i, j, te, tr, tb: (tb[i], 0)),
            pl.BlockSpec((None, d, tf), lambda i, j, te, tr, tb: (te[i], 0, jj(i, j, tr))),
            pl.BlockSpec((None, d, tf), lambda i, j, te, tr, tb: (te[i], 0, nj + jj(i, j, tr))),
            pl.BlockSpec((None, 1, tf), lambda i, j, te, tr, tb: (te[i], 0, jj(i, j, tr))),
            pl.BlockSpec((None, 1, tf), lambda i, j, te, tr, tb: (te[i], 0, nj + jj(i, j, tr))),
            pl.BlockSpec((None, tf, d), lambda i, j, te, tr, tb: (te[i], jj(i, j, tr), 0)),
            pl.BlockSpec((None, 1, d), lambda i, j, te, tr, tb: (te[i], 0, 0)),
        ],
        out_specs=pl.BlockSpec((r, d), lambda i, j, te, tr, tb: (i, 0)),
        scratch_shapes=[pltpu.VMEM((d, tf), BF16), pltpu.VMEM((d, tf), BF16), pltpu.VMEM((tf, d), BF16)],
    )
    return pl.pallas_call(
        _expert_kernel,
        grid_spec=grid_spec,
        out_shape=jax.ShapeDtypeStruct((n_rows, d), F32),
        compiler_params=_cparams(("arbitrary", "arbitrary")),
        name="experts",
    )(tile_expert, tile_rows, tile_blk, xs, w_gu, w_gu, b_gu3, b_gu3, w_down, b_down3)


def _combine_kernel(pos_ref, h_ref, gate_ref, g_ref, ys_ref, o_ref, buf_ref, sem):
    tm = h_ref.shape[0]

    def row_copy(r, kk, p):
        return pltpu.make_async_copy(ys_ref.at[pl.ds(p, 1)], buf_ref.at[kk, pl.ds(r, 1)], sem)

    def issue(r, carry):
        for kk in range(TOP_K):
            row_copy(r, kk, pos_ref[0, 0, r * TOP_K + kk]).start()
        return carry

    lax.fori_loop(0, tm, issue, 0)

    def drain(r, carry):
        for kk in range(TOP_K):
            row_copy(r, kk, pos_ref[0, 0, r * TOP_K + kk]).wait()
        return carry

    lax.fori_loop(0, tm, drain, 0)

    gates = gate_ref[...]
    y = h_ref[...]
    for kk in range(TOP_K):
        y = y + gates[:, kk:kk + 1] * buf_ref[kk]
    o_ref[...] = _rms(y, g_ref[...])


def _combine(h2, gates, pos, ys, g_final):
    t, d = h2.shape
    tm = min(COMBINE_TM, t)
    pos3 = pos.reshape(t // tm, 1, tm * TOP_K)
    return pl.pallas_call(
        _combine_kernel,
        grid=(t // tm,),
        in_specs=[
            pl.BlockSpec((1, 1, tm * TOP_K), lambda i: (i, 0, 0), memory_space=pltpu.SMEM),
            pl.BlockSpec((tm, d), lambda i: (i, 0)),
            pl.BlockSpec((tm, TOP_K), lambda i: (i, 0)),
            pl.BlockSpec((1, d), lambda i: (0, 0)),
            pl.BlockSpec(memory_space=pl.ANY),
        ],
        out_specs=pl.BlockSpec((tm, d), lambda i: (i, 0)),
        out_shape=jax.ShapeDtypeStruct((t, d), F32),
        scratch_shapes=[pltpu.VMEM((TOP_K, tm, d), F32), pltpu.SemaphoreType.DMA(())],
        compiler_params=_cparams(("arbitrary",)),
        name="combine",
    )(pos3, h2, gates, g_final, ys)


def _routing_tables(counts, top_idx, rank, n_rows):
    r = EXPERT_ROWS
    counts = counts.reshape(N_EXPERTS)
    padded = (counts + r - 1) // r * r
    ends = jnp.cumsum(padded)
    starts = ends - padded
    pos = starts[top_idx] + rank
    n_tiles = n_rows // r
    tile_row0 = jnp.arange(n_tiles, dtype=jnp.int32) * r
    owner = jnp.sum((ends[None, :] <= tile_row0[:, None]).astype(jnp.int32), axis=1)
    tile_expert = jnp.minimum(owner, N_EXPERTS - 1).astype(jnp.int32)
    tile_rows = jnp.clip(counts[tile_expert] - (tile_row0 - starts[tile_expert]), 0, r).astype(jnp.int32)
    n_used = (ends[-1] // r).astype(jnp.int32)
    last = jnp.maximum(n_used - 1, 0)
    active = jnp.arange(n_tiles, dtype=jnp.int32) < n_used
    tile_rows = jnp.where(active, tile_rows, 0)
    tile_blk = jnp.where(active, jnp.arange(n_tiles, dtype=jnp.int32), last)
    tile_expert = jnp.where(active, tile_expert, tile_expert[last])
    return pos.astype(jnp.int32), tile_expert, tile_rows, tile_blk


def kernel(x, mem, norm_mix_g, w_in, sinks, lambda_q1, lambda_k1, lambda_q2, lambda_k2, diff_subln_g, w_branch, w_o, norm_cross_g, norm_mem_g, w_cq, w_ckv, w_co, norm_ffn_g, w_router, b_router, w_gate_up, b_gate_up, w_down, b_down, norm_final_g):
    b, s, d = x.shape
    t = b * s
    x2 = x.reshape(t, d)
    l = 0

    w = w_in[l]
    o_qa, o_ka, o_va = 0, SWA_WIDTH, SWA_WIDTH + SWA_KV_WIDTH
    o_qd = o_va + SWA_KV_WIDTH
    o_kd, o_vd, o_g = o_qd + DIFF_WIDTH, o_qd + 2 * DIFF_WIDTH, o_qd + 3 * DIFF_WIDTH
    w_perm = jnp.concatenate([
        w[:, o_g:], w[:, o_qd:o_kd], w[:, o_kd:o_vd], w[:, o_vd:o_g],
        w[:, o_qa:o_ka], w[:, o_ka:o_va], w[:, o_va:o_qd]], axis=1).astype(BF16)
    proj = _inproj(x2, norm_mix_g[l].reshape(1, d), w_perm)
    c_qd = 2 * d
    c_kd, c_vd, c_qa = c_qd + DIFF_WIDTH, c_qd + 2 * DIFF_WIDTH, c_qd + 3 * DIFF_WIDTH
    c_ka, c_va = c_qa + SWA_WIDTH, c_qa + SWA_WIDTH + SWA_KV_WIDTH

    def seg(c0, width):
        return proj[:, c0:c0 + width].reshape(b, s, width)

    qaT = jnp.swapaxes(seg(c_qa, SWA_WIDTH), 1, 2)
    vaT = jnp.swapaxes(seg(c_va, SWA_KV_WIDTH), 1, 2)
    yaT = _swa(qaT, seg(c_ka, SWA_KV_WIDTH), vaT, sinks[l])
    ya = jnp.swapaxes(yaT, 1, 2).reshape(t, SWA_WIDTH)

    td = min(DIFF_T, s)
    nk = s // td
    qdT = jnp.swapaxes(seg(c_qd, DIFF_WIDTH), 1, 2)
    k5 = seg(c_kd, DIFF_WIDTH).reshape(b, nk, td, DIFF_HEADS, DIFF_HEAD_WIDTH).transpose(0, 3, 1, 2, 4)
    v5 = seg(c_vd, DIFF_WIDTH).reshape(b, nk, td, DIFF_HEADS, DIFF_HEAD_WIDTH).transpose(0, 3, 1, 4, 2)
    lamv = jnp.stack([lambda_q1[l], lambda_k1[l], lambda_q2[l], lambda_k2[l]]).astype(F32)
    ydT = _diff(qdT, k5, v5, lamv, diff_subln_g[l].astype(F32).reshape(DIFF_HEAD_WIDTH, 1))
    yd = jnp.swapaxes(ydT, 1, 2).reshape(t, DIFF_WIDTH)

    h1 = _merge(ya, yd, proj, x2, w_branch[l].astype(BF16), w_o[l].astype(BF16))

    kmem, vmem = _memkv(mem, norm_mem_g[l].reshape(1, d), w_ckv[l].astype(BF16))
    h2, xn, top_idx, gates, rank, counts = _cross_router(
        h1, kmem, vmem, norm_cross_g[l].reshape(1, d), w_cq[l].astype(BF16), w_co[l].astype(BF16),
        norm_ffn_g[l].reshape(1, d), w_router[l].astype(BF16), b_router[l].reshape(1, N_EXPERTS), s)

    n_rows = t * TOP_K + N_EXPERTS * EXPERT_ROWS
    pos, tile_expert, tile_rows, tile_blk = _routing_tables(counts, top_idx, rank, n_rows)
    xs = _dispatch(xn, pos, n_rows)
    ys = _experts(xs, tile_expert, tile_rows, tile_blk, w_gate_up[l], b_gate_up[l], w_down[l], b_down[l])
    out = _combine(h2, gates, pos, ys, norm_final_g.reshape(1, d))
    return out.reshape(b, s, d)
```

```python
import functools
import math

import jax
import jax.numpy as jnp
from jax import lax
from jax.experimental import pallas as pl
from jax.experimental.pallas import tpu as pltpu

BF16 = jnp.bfloat16
F32 = jnp.float32
U32 = jnp.uint32
NEG_INF = float("-inf")
LOG2E = math.log2(math.e)

HEAD_DIM = 64
SWA_Q_HEADS = 16
SWA_KV_HEADS = 4
SWA_GROUP = SWA_Q_HEADS // SWA_KV_HEADS
SWA_BLOCK = 128
SWA_WIDTH = SWA_Q_HEADS * HEAD_DIM
SWA_KV_WIDTH = SWA_KV_HEADS * HEAD_DIM
DIFF_HEADS = 8
DIFF_HEAD_WIDTH = 2 * HEAD_DIM
DIFF_WIDTH = DIFF_HEADS * DIFF_HEAD_WIDTH
X_HEADS = 4
X_HEAD_DIM = 128
X_WIDTH = X_HEADS * X_HEAD_DIM
N_EXPERTS = 32
TOP_K = 4
SWIGLU_LIMIT = 7.0
SWIGLU_ALPHA = 1.702
RMS_EPS = 1e-5
LAMBDA_INIT = 0.8 - 0.6 * math.exp(-0.3 * 0)

V7X_VMEM_BYTES = 64 * 1024 * 1024
VMEM_LIMIT = V7X_VMEM_BYTES - 8 * 1024 * 1024
V7X_LANES = 128
V7X_SUBLANES = 8

INPROJ_TM = 1024
INPROJ_TN = 512
DIFF_T = 512
DIFF_POS_SPLIT = 256
MERGE_TM = 256
CROSS_TM = 256
DISPATCH_TM = 256
EXPERT_CHUNK = 768
EXPERT_ROWS = 3 * EXPERT_CHUNK
EXPERT_TF = 256
COMBINE_TM = 256
DMA_ISSUE_UNROLL = 8


def _cparams(semantics):
    return pltpu.CompilerParams(dimension_semantics=semantics, vmem_limit_bytes=VMEM_LIMIT)


def _rms(x, g):
    ms = jnp.mean(x * x, axis=-1, keepdims=True)
    return x * lax.rsqrt(ms + RMS_EPS) * g


def _pack_pair(lo, hi):
    lo_bits = lax.bitcast_convert_type(lo.astype(BF16).astype(F32), U32)
    hi_bits = lax.bitcast_convert_type(hi.astype(BF16).astype(F32), U32)
    return hi_bits | (lo_bits >> 16)


def _unpack_lo(w):
    return lax.bitcast_convert_type(w << 16, F32)


def _unpack_hi(w):
    return lax.bitcast_convert_type(w & jnp.uint32(0xFFFF0000), F32)


def _inproj_kernel(x_ref, g_ref, w_ref, o_ref, xn_ref):
    @pl.when(pl.program_id(1) == 0)
    def _():
        xn_ref[...] = _rms(x_ref[...], g_ref[...]).astype(BF16)

    o_ref[...] = jnp.dot(xn_ref[...], w_ref[...], preferred_element_type=F32).astype(o_ref.dtype)


def _inproj(x2, g, w):
    t, d = x2.shape
    n = w.shape[1]
    tm, tn = min(INPROJ_TM, t), INPROJ_TN
    return pl.pallas_call(
        _inproj_kernel,
        grid=(t // tm, n // tn),
        in_specs=[
            pl.BlockSpec((tm, d), lambda i, j: (i, 0)),
            pl.BlockSpec((1, d), lambda i, j: (0, 0)),
            pl.BlockSpec((d, tn), lambda i, j: (0, j)),
        ],
        out_specs=pl.BlockSpec((tm, tn), lambda i, j: (i, j)),
        out_shape=jax.ShapeDtypeStruct((t, n), BF16),
        scratch_shapes=[pltpu.VMEM((tm, d), BF16)],
        compiler_params=_cparams(("parallel", "arbitrary")),
        name="inproj",
    )(x2, g, w)


def _swa_kernel(qT_ref, kp_ref, kc_ref, vp_ref, vc_ref, bias_ref, sink_ref, o_ref):
    n = pl.program_id(1)
    blk = SWA_BLOCK
    kband = jnp.concatenate([kp_ref[...], kc_ref[...]], axis=0)
    vband = jnp.concatenate([vp_ref[...], vc_ref[...]], axis=1)
    krow = lax.broadcasted_iota(jnp.int32, (2 * blk, SWA_GROUP * blk), 0)
    has_prev = n > 0
    zeros = jnp.zeros((HEAD_DIM, blk), BF16)
    for h in range(SWA_KV_HEADS):
        lane0 = 2 * HEAD_DIM * (h // 2)
        k128 = kband[:, lane0:lane0 + 2 * HEAD_DIM]
        pieces = []
        for g in range(SWA_GROUP):
            hq = SWA_GROUP * h + g
            qh = qT_ref[HEAD_DIM * hq:HEAD_DIM * (hq + 1), :]
            pieces.append(jnp.concatenate([qh, zeros] if h % 2 == 0 else [zeros, qh], axis=0))
        qz = jnp.concatenate(pieces, axis=1)
        s = jnp.dot(k128, qz, preferred_element_type=F32) * (HEAD_DIM ** -0.5)
        s = s + bias_ref[h]
        s = jnp.where((krow >= blk) | has_prev, s, NEG_INF)
        sink = sink_ref[h]
        m = jnp.maximum(jnp.max(s, axis=0, keepdims=True), sink)
        p = jnp.exp(s - m)
        denom = jnp.sum(p, axis=0, keepdims=True) + jnp.exp(sink - m)
        vh = vband[HEAD_DIM * h:HEAD_DIM * (h + 1), :]
        o = jnp.dot(vh, p.astype(BF16), preferred_element_type=F32) / denom
        for g in range(SWA_GROUP):
            hq = SWA_GROUP * h + g
            o_ref[HEAD_DIM * hq:HEAD_DIM * (hq + 1), :] = o[:, blk * g:blk * (g + 1)].astype(o_ref.dtype)


def _swa_tables(sinks):
    blk = SWA_BLOCK
    slopes = (2.0 ** (-8.0 * jnp.arange(1, SWA_Q_HEADS + 1, dtype=F32) / SWA_Q_HEADS)).reshape(SWA_KV_HEADS, SWA_GROUP)
    q_pos = jnp.arange(blk)[None, :] + blk
    k_pos = jnp.arange(2 * blk)[:, None]
    dist = q_pos - k_pos
    allowed = (dist >= 0) & (dist < SWA_BLOCK)
    bias = -slopes[:, :, None, None] * dist.astype(F32)[None, None]
    bias = jnp.where(allowed[None, None], bias, NEG_INF)
    bias = jnp.transpose(bias, (0, 2, 1, 3)).reshape(SWA_KV_HEADS, 2 * blk, SWA_GROUP * blk)
    sink = sinks.astype(F32).reshape(SWA_KV_HEADS, SWA_GROUP, 1)
    sink = jnp.broadcast_to(sink, (SWA_KV_HEADS, SWA_GROUP, blk)).reshape(SWA_KV_HEADS, 1, SWA_GROUP * blk)
    return bias, sink


def _swa(qT, k, vT, sinks):
    b, _, s = qT.shape
    blk = SWA_BLOCK
    bias, sink = _swa_tables(sinks)
    prev = lambda n: jnp.maximum(n - 1, 0)
    return pl.pallas_call(
        _swa_kernel,
        grid=(b, s // blk),
        in_specs=[
            pl.BlockSpec((None, SWA_WIDTH, blk), lambda bi, n: (bi, 0, n)),
            pl.BlockSpec((None, blk, SWA_KV_WIDTH), lambda bi, n: (bi, prev(n), 0)),
            pl.BlockSpec((None, blk, SWA_KV_WIDTH), lambda bi, n: (bi, n, 0)),
            pl.BlockSpec((None, SWA_KV_WIDTH, blk), lambda bi, n: (bi, 0, prev(n))),
            pl.BlockSpec((None, SWA_KV_WIDTH, blk), lambda bi, n: (bi, 0, n)),
            pl.BlockSpec(bias.shape, lambda bi, n: (0, 0, 0)),
            pl.BlockSpec(sink.shape, lambda bi, n: (0, 0, 0)),
        ],
        out_specs=pl.BlockSpec((None, SWA_WIDTH, blk), lambda bi, n: (bi, 0, n)),
        out_shape=jax.ShapeDtypeStruct((b, SWA_WIDTH, s), BF16),
        compiler_params=_cparams(("parallel", "arbitrary")),
        name="swa",
    )(qT, k, k, vT, vT, bias, sink)


def _diff_kernel(c_ref, qT_ref, qx_ref, k_ref, v_ref, lamv_ref, g_ref, o_ref,
                 q2_ref, z_ref, p_ref, a_ref, m_ref, acc_ref, *, t):
    h = pl.program_id(1)
    qi = pl.program_id(2)
    w = DIFF_HEAD_WIDTH
    c = c_ref[h]

    @pl.when(qi == 0)
    def _():
        q2_ref[w:, :] = qx_ref[...]

    qt = qT_ref[...]
    row = lax.broadcasted_iota(jnp.int32, qt.shape, 0)
    zero = jnp.zeros_like(qt)
    q2_ref[:w, :t] = jnp.where(row < HEAD_DIM, qt, zero)
    q2_ref[:w, t:] = jnp.where(row >= HEAD_DIM, qt, zero)
    m_ref[...] = jnp.full(m_ref.shape, NEG_INF, F32)
    acc_ref[...] = jnp.zeros(acc_ref.shape, F32)
    p_ref[1] = jnp.zeros(p_ref.shape[1:], BF16)
    a_ref[1] = jnp.ones(a_ref.shape[1:], F32)

    def scores(k, slot):
        z_ref[slot] = jnp.dot(k_ref[k], q2_ref[...], preferred_element_type=F32)

    def values(k, slot):
        pv = jnp.dot(v_ref[jnp.maximum(k, 0)], p_ref[slot], preferred_element_type=F32)
        acc_ref[...] = a_ref[slot] * acc_ref[...] + pv

    def softmax(k, slot, masked):
        z = z_ref[slot]
        if masked:
            kr = lax.broadcasted_iota(jnp.int32, z.shape, 0)
            qc = lax.broadcasted_iota(jnp.int32, z.shape, 1)
            qc = jnp.where(qc >= t, qc - t, qc)
            z = jnp.where(kr <= qc, z, NEG_INF)
        off = c * ((k - qi) * t).astype(F32)
        m_old = m_ref[...]
        m_new = jnp.maximum(m_old, jnp.max(z, axis=0, keepdims=True) + off)
        p_ref[slot] = jnp.exp2(z - (m_new - off)).astype(BF16)
        a_ref[slot] = jnp.exp2(m_old - m_new)
        m_ref[...] = m_new

    scores(0, 0)

    def pair(j, carry):
        k = 2 * j
        scores(k + 1, 1)
        values(k - 1, 1)
        softmax(k, 0, False)
        scores(k + 2, 0)
        values(k, 0)
        softmax(k + 1, 1, False)
        return carry

    lax.fori_loop(0, qi >> 1, pair, 0)

    @pl.when((qi & 1) == 0)
    def _():
        values(qi - 1, 1)
        softmax(qi, 0, True)
        values(qi, 0)

    @pl.when((qi & 1) == 1)
    def _():
        scores(qi, 1)
        values(qi - 2, 1)
        softmax(qi - 1, 0, False)
        values(qi - 1, 0)
        softmax(qi, 1, True)
        values(qi, 1)

    lamv = lamv_ref[...]
    lam = (jnp.exp(jnp.sum(lamv[0:1] * lamv[1:2], axis=-1, keepdims=True))
           - jnp.exp(jnp.sum(lamv[2:3] * lamv[3:4], axis=-1, keepdims=True)) + LAMBDA_INIT)
    acc = acc_ref[...]
    l = acc[w:w + 1, :]
    o = acc[:w, :t] / l[:, :t] - lam * (acc[:w, t:] / l[:, t:])
    ms = jnp.mean(o * o, axis=0, keepdims=True)
    y = o * lax.rsqrt(ms + RMS_EPS) * g_ref[...] * (1.0 - LAMBDA_INIT)
    o_ref[...] = y.astype(o_ref.dtype)


def _diff_tables(t):
    w = DIFF_HEAD_WIDTH
    slopes = 2.0 ** (-8.0 * jnp.arange(1, DIFF_HEADS + 1, dtype=F32) / DIFF_HEADS)
    c = slopes * jnp.float32(LOG2E)
    c1 = c.astype(BF16)
    c2 = (c - c1.astype(F32)).astype(BF16)
    c3 = (c - c1.astype(F32) - c2.astype(F32)).astype(BF16)
    parts = jnp.stack([c1, c2, c3, c1, c2, c3], axis=1)
    qx = jnp.zeros((DIFF_HEADS, w, 2 * t), BF16).at[:, :6, :].set(parts[:, :, None])
    r = jnp.arange(t)
    r_lo = (r % DIFF_POS_SPLIT).astype(BF16)
    r_hi = ((r // DIFF_POS_SPLIT) * DIFF_POS_SPLIT).astype(BF16)
    kx = jnp.zeros((t, w), BF16).at[:, 0:3].set(r_lo[:, None]).at[:, 3:6].set(r_hi[:, None])
    return c, qx, kx


def _diff(qT, k5, v5, qx, cs, lamv, subln_g):
    b, _, s = qT.shape
    t = k5.shape[3]
    nk = s // t
    w = DIFF_HEAD_WIDTH
    wv = v5.shape[3]
    return pl.pallas_call(
        functools.partial(_diff_kernel, t=t),
        grid=(b, DIFF_HEADS, nk),
        in_specs=[
            pl.BlockSpec(memory_space=pltpu.SMEM),
            pl.BlockSpec((None, w, t), lambda bi, h, qi: (bi, h, qi)),
            pl.BlockSpec((None, w, 2 * t), lambda bi, h, qi: (h, 0, 0)),
            pl.BlockSpec((None, None, nk, t, 2 * w), lambda bi, h, qi: (bi, h, 0, 0, 0)),
            pl.BlockSpec((None, None, nk, wv, t), lambda bi, h, qi: (bi, h, 0, 0, 0)),
            pl.BlockSpec((4, HEAD_DIM), lambda bi, h, qi: (0, 0)),
            pl.BlockSpec((w, 1), lambda bi, h, qi: (0, 0)),
        ],
        out_specs=pl.BlockSpec((None, w, t), lambda bi, h, qi: (bi, h, qi)),
        out_shape=jax.ShapeDtypeStruct((b, DIFF_WIDTH, s), BF16),
        scratch_shapes=[
            pltpu.VMEM((2 * w, 2 * t), BF16),
            pltpu.VMEM((2, t, 2 * t), F32),
            pltpu.VMEM((2, t, 2 * t), BF16),
            pltpu.VMEM((2, 1, 2 * t), F32),
            pltpu.VMEM((1, 2 * t), F32),
            pltpu.VMEM((wv, 2 * t), F32),
        ],
        compiler_params=_cparams(("arbitrary", "arbitrary", "arbitrary")),
        name="diffattn",
    )(cs, qT, qx, k5, v5, lamv, subln_g)


def _merge_kernel(ya_ref, yd_ref, g0_ref, g1_ref, x_ref, wb0_ref, wb1_ref, wo_ref, o_ref):
    ua = jnp.dot(ya_ref[...], wb0_ref[...], preferred_element_type=F32)
    ud = jnp.dot(yd_ref[...], wb1_ref[...], preferred_element_type=F32)
    merged = jax.nn.sigmoid(g0_ref[...].astype(F32)) * ua + jax.nn.sigmoid(g1_ref[...].astype(F32)) * ud
    o_ref[...] = x_ref[...] + jnp.dot(merged.astype(BF16), wo_ref[...], preferred_element_type=F32)


def _merge(ya, yd, proj, x2, wb, wo):
    t, d = x2.shape
    tm = min(MERGE_TM, t)
    bw = ya.shape[1]
    const = dict(pipeline_mode=pl.Buffered(1))
    return pl.pallas_call(
        _merge_kernel,
        grid=(t // tm,),
        in_specs=[
            pl.BlockSpec((tm, bw), lambda i: (i, 0)),
            pl.BlockSpec((tm, bw), lambda i: (i, 0)),
            pl.BlockSpec((tm, d), lambda i: (i, 0)),
            pl.BlockSpec((tm, d), lambda i: (i, 1)),
            pl.BlockSpec((tm, d), lambda i: (i, 0)),
            pl.BlockSpec((None, bw, d), lambda i: (0, 0, 0), **const),
            pl.BlockSpec((None, bw, d), lambda i: (1, 0, 0), **const),
            pl.BlockSpec((d, d), lambda i: (0, 0), **const),
        ],
        out_specs=pl.BlockSpec((tm, d), lambda i: (i, 0)),
        out_shape=jax.ShapeDtypeStruct((t, d), F32),
        compiler_params=_cparams(("parallel",)),
        name="merge",
    )(ya, yd, proj, proj, x2, wb, wb, wo)


def _memkv_kernel(mem_ref, g_ref, w_ref, k_ref, v_ref):
    mn = _rms(mem_ref[...], g_ref[...]).astype(BF16)
    kv = jnp.dot(mn, w_ref[...], preferred_element_type=F32)
    k_ref[...] = kv[:, :X_WIDTH].astype(k_ref.dtype)
    v_ref[...] = kv[:, X_WIDTH:].astype(v_ref.dtype)


def _memkv(mem, g, w):
    b, m, d = mem.shape
    out = jax.ShapeDtypeStruct((b, m, X_WIDTH), BF16)
    return pl.pallas_call(
        _memkv_kernel,
        grid=(b,),
        in_specs=[
            pl.BlockSpec((None, m, d), lambda i: (i, 0, 0)),
            pl.BlockSpec((1, d), lambda i: (0, 0)),
            pl.BlockSpec((d, 2 * X_WIDTH), lambda i: (0, 0)),
        ],
        out_specs=[pl.BlockSpec((None, m, X_WIDTH), lambda i: (i, 0, 0))] * 2,
        out_shape=[out, out],
        compiler_params=_cparams(("parallel",)),
        name="memkv",
    )(mem, g, w)


def _cross_router_kernel(h_ref, k_ref, v_ref, gc_ref, wq_ref, wo_ref, gf_ref, wr_ref, br_ref,
                         h2_ref, xn_ref, idx_ref, gate_ref, rank_ref, cnt_ref, run_ref):
    i = pl.program_id(0)
    tm, d = h_ref.shape

    @pl.when(i == 0)
    def _():
        run_ref[...] = jnp.zeros(run_ref.shape, F32)

    h = h_ref[...]
    hn = _rms(h, gc_ref[...]).astype(BF16)
    q = jnp.dot(hn, wq_ref[...], preferred_element_type=F32).astype(BF16)
    k = k_ref[...]
    v = v_ref[...]
    outs = []
    for hd in range(X_HEADS):
        sl = slice(X_HEAD_DIM * hd, X_HEAD_DIM * (hd + 1))
        s = lax.dot_general(q[:, sl], k[:, sl], (((1,), (1,)), ((), ())),
                            preferred_element_type=F32) * (X_HEAD_DIM ** -0.5)
        s = s - jnp.max(s, axis=-1, keepdims=True)
        p = jnp.exp(s)
        p = p / jnp.sum(p, axis=-1, keepdims=True)
        outs.append(jnp.dot(p.astype(BF16), v[:, sl], preferred_element_type=F32))
    o = jnp.concatenate(outs, axis=1).astype(BF16)
    h2 = h + jnp.dot(o, wo_ref[...], preferred_element_type=F32)
    h2_ref[...] = h2

    xn = _rms(h2, gf_ref[...])
    xn_ref[...] = _pack_pair(xn[:, :d // 2], xn[:, d // 2:])
    logits = jnp.dot(xn.astype(BF16), wr_ref[...], preferred_element_type=F32) + br_ref[...]

    lane = lax.broadcasted_iota(jnp.int32, logits.shape, 1).astype(F32)
    work = logits
    vals, idxs = [], []
    onehot = jnp.zeros(logits.shape, F32)
    for _ in range(TOP_K):
        mx = jnp.max(work, axis=-1, keepdims=True)
        ix = jnp.min(jnp.where(work == mx, lane, float(N_EXPERTS)), axis=-1, keepdims=True)
        sel = lane == ix
        vals.append(mx)
        idxs.append(ix)
        onehot = jnp.where(sel, 1.0, onehot)
        work = jnp.where(sel, NEG_INF, work)
    es = [jnp.exp(vv - vals[0]) for vv in vals]
    tot = es[0] + es[1] + es[2] + es[3]

    r = lax.broadcasted_iota(jnp.int32, (tm, tm), 0)
    c = lax.broadcasted_iota(jnp.int32, (tm, tm), 1)
    lower = jnp.where(c < r, 1.0, 0.0).astype(BF16)
    prefix = jnp.dot(lower, onehot.astype(BF16), preferred_element_type=F32) + run_ref[...]
    for kk in range(TOP_K):
        idx_ref[:, kk:kk + 1] = idxs[kk].astype(jnp.int32)
        gate_ref[:, kk:kk + 1] = es[kk] / tot
        rk = jnp.sum(jnp.where(lane == idxs[kk], prefix, 0.0), axis=-1, keepdims=True)
        rank_ref[:, kk:kk + 1] = rk.astype(jnp.int32)
    run_ref[...] = run_ref[...] + jnp.sum(onehot, axis=0, keepdims=True)
    cnt_ref[...] = run_ref[...].astype(jnp.int32)


def _cross_router(h1, kmem, vmem, gc, wq, wo, gf, wr, br, seq):
    t, d = h1.shape
    tm = min(CROSS_TM, seq)
    per_b = seq // tm
    m = kmem.shape[1]
    const = dict(pipeline_mode=pl.Buffered(1))
    outs = pl.pallas_call(
        _cross_router_kernel,
        grid=(t // tm,),
        in_specs=[
            pl.BlockSpec((tm, d), lambda i: (i, 0)),
            pl.BlockSpec((None, m, X_WIDTH), lambda i: (i // per_b, 0, 0)),
            pl.BlockSpec((None, m, X_WIDTH), lambda i: (i // per_b, 0, 0)),
            pl.BlockSpec((1, d), lambda i: (0, 0)),
            pl.BlockSpec((d, X_WIDTH), lambda i: (0, 0), **const),
            pl.BlockSpec((X_WIDTH, d), lambda i: (0, 0), **const),
            pl.BlockSpec((1, d), lambda i: (0, 0)),
            pl.BlockSpec((d, N_EXPERTS), lambda i: (0, 0), **const),
            pl.BlockSpec((1, N_EXPERTS), lambda i: (0, 0)),
        ],
        out_specs=[
            pl.BlockSpec((tm, d), lambda i: (i, 0)),
            pl.BlockSpec((tm, d // 2), lambda i: (i, 0)),
            pl.BlockSpec((tm, TOP_K), lambda i: (i, 0)),
            pl.BlockSpec((tm, TOP_K), lambda i: (i, 0)),
            pl.BlockSpec((tm, TOP_K), lambda i: (i, 0)),
            pl.BlockSpec((1, N_EXPERTS), lambda i: (0, 0)),
        ],
        out_shape=[
            jax.ShapeDtypeStruct((t, d), F32),
            jax.ShapeDtypeStruct((t, d // 2), U32),
            jax.ShapeDtypeStruct((t, TOP_K), jnp.int32),
            jax.ShapeDtypeStruct((t, TOP_K), F32),
            jax.ShapeDtypeStruct((t, TOP_K), jnp.int32),
            jax.ShapeDtypeStruct((1, N_EXPERTS), jnp.int32),
        ],
        scratch_shapes=[pltpu.VMEM((1, N_EXPERTS), F32)],
        compiler_params=_cparams(("arbitrary",)),
        name="cross_router",
    )(h1, kmem, vmem, gc, wq, wo, gf, wr, br)
    return outs


def _dispatch_kernel(pos_ref, tail_ref, x_ref, xs_ref, zero_ref, sem, zsem):
    i = pl.program_id(0)
    tm = x_ref.shape[0]

    @pl.when(i == 0)
    def _():
        zero_ref[...] = jnp.zeros(zero_ref.shape, zero_ref.dtype)

        def tail_copy(e):
            row0 = pl.multiple_of(tail_ref[e], V7X_SUBLANES)
            return pltpu.make_async_copy(zero_ref, xs_ref.at[pl.ds(row0, EXPERT_CHUNK)], zsem)

        for e in range(N_EXPERTS):
            @pl.when(tail_ref[e] >= 0)
            def _():
                tail_copy(e).start()
        for e in range(N_EXPERTS):
            @pl.when(tail_ref[e] >= 0)
            def _():
                tail_copy(e).wait()

    def row_copy(r, p):
        return pltpu.make_async_copy(x_ref.at[pl.ds(r, 1)], xs_ref.at[pl.ds(p, 1)], sem)

    def issue(r, carry):
        for kk in range(TOP_K):
            row_copy(r, pos_ref[0, 0, r * TOP_K + kk]).start()
        return carry

    lax.fori_loop(0, tm, issue, 0, unroll=DMA_ISSUE_UNROLL)
    for _ in range(tm * TOP_K):
        row_copy(0, 0).wait()


def _dispatch(xn, pos, tails, n_rows):
    t, dw = xn.shape
    tm = min(DISPATCH_TM, t)
    pos3 = pos.reshape(t // tm, 1, tm * TOP_K)
    return pl.pallas_call(
        _dispatch_kernel,
        grid=(t // tm,),
        in_specs=[
            pl.BlockSpec((1, 1, tm * TOP_K), lambda i: (i, 0, 0), memory_space=pltpu.SMEM),
            pl.BlockSpec(memory_space=pltpu.SMEM),
            pl.BlockSpec((tm, dw), lambda i: (i, 0)),
        ],
        out_specs=pl.BlockSpec(memory_space=pl.ANY),
        out_shape=jax.ShapeDtypeStruct((n_rows, dw), xn.dtype),
        scratch_shapes=[pltpu.VMEM((EXPERT_CHUNK, dw), xn.dtype), pltpu.SemaphoreType.DMA(()),
                        pltpu.SemaphoreType.DMA(())],
        compiler_params=_cparams(("arbitrary",)),
        name="dispatch",
    )(pos3, tails, xn)


def _expert_kernel(ve_ref, vr_ref, vb_ref, x_ref, wg_ref, wl_ref, bg_ref, bl_ref, wd_ref, bd_ref, o_ref,
                   act_ref, wa_ref, wb_ref):
    del ve_ref, vb_ref
    v = pl.program_id(0)
    s = pl.program_id(1)
    rows = vr_ref[v]
    nj = act_ref.shape[0]
    tf = act_ref.shape[2]
    ch = EXPERT_CHUNK
    n_chunks = x_ref.shape[0] // ch
    half = o_ref.shape[1]

    @pl.when((rows > 0) & (s < nj))
    def _():
        wa_ref[...] = wg_ref[...].astype(BF16)
        wb_ref[...] = wl_ref[...].astype(BF16)
        for c in range(n_chunks):
            @pl.when(c * ch < rows)
            def _():
                xw = x_ref[pl.ds(c * ch, ch), :]
                xb = jnp.concatenate([_unpack_lo(xw).astype(BF16), _unpack_hi(xw).astype(BF16)], axis=1)
                glu = jnp.dot(xb, wa_ref[...], preferred_element_type=F32) + bg_ref[...]
                lin = jnp.dot(xb, wb_ref[...], preferred_element_type=F32) + bl_ref[...]
                glu = jnp.minimum(glu, SWIGLU_LIMIT)
                lin = jnp.clip(lin, -SWIGLU_LIMIT, SWIGLU_LIMIT)
                act = glu * jax.nn.sigmoid(SWIGLU_ALPHA * glu) * (lin + 1.0)
                act_ref[s, pl.ds(c * ch, ch), :] = act.astype(BF16)

    @pl.when((rows > 0) & (s >= nj))
    def _():
        wa_ref[...] = wd_ref[...].astype(BF16)
        for c in range(n_chunks):
            sl = pl.ds(c * ch, ch)

            @pl.when(c * ch < rows)
            def _():
                acc = jnp.dot(act_ref[0, sl, :], wa_ref[0:tf, :], preferred_element_type=F32)
                for j in range(1, nj):
                    acc = acc + jnp.dot(act_ref[j, sl, :], wa_ref[j * tf:(j + 1) * tf, :],
                                        preferred_element_type=F32)
                acc = acc + bd_ref[...]
                o_ref[sl, :] = _pack_pair(acc[:, :half], acc[:, half:])

            @pl.when(c * ch >= rows)
            def _():
                o_ref[sl, :] = jnp.zeros((ch, half), o_ref.dtype)


def _experts(xs, visit_expert, visit_rows, visit_blk, w_gu, b_gu, w_down, b_down):
    n_rows, dw = xs.shape
    e, d, ff2 = w_gu.shape
    ff = ff2 // 2
    r, tf = EXPERT_ROWS, EXPERT_TF
    nj = ff // tf
    nn = d // tf
    b_gu3 = b_gu.reshape(e, 1, ff2)
    b_down3 = b_down.reshape(e, 1, d)

    def up(v, s, vr):
        return jnp.where(vr[v] > 0, jnp.minimum(s, nj - 1), nj - 1)

    def down(v, s, vr):
        return jnp.where(vr[v] > 0, jnp.maximum(s - nj, 0), nn - 1)

    grid_spec = pltpu.PrefetchScalarGridSpec(
        num_scalar_prefetch=3,
        grid=(n_rows // r, nj + nn),
        in_specs=[
            pl.BlockSpec((r, dw), lambda v, s, ve, vr, vb: (vb[v], 0)),
            pl.BlockSpec((None, d, tf), lambda v, s, ve, vr, vb: (ve[v], 0, up(v, s, vr))),
            pl.BlockSpec((None, d, tf), lambda v, s, ve, vr, vb: (ve[v], 0, nj + up(v, s, vr))),
            pl.BlockSpec((None, 1, tf), lambda v, s, ve, vr, vb: (ve[v], 0, up(v, s, vr))),
            pl.BlockSpec((None, 1, tf), lambda v, s, ve, vr, vb: (ve[v], 0, nj + up(v, s, vr))),
            pl.BlockSpec((None, ff, tf), lambda v, s, ve, vr, vb: (ve[v], 0, down(v, s, vr))),
            pl.BlockSpec((None, 1, tf), lambda v, s, ve, vr, vb: (ve[v], 0, down(v, s, vr))),
        ],
        out_specs=pl.BlockSpec((r, tf // 2), lambda v, s, ve, vr, vb: (vb[v], down(v, s, vr))),
        scratch_shapes=[
            pltpu.VMEM((nj, r, tf), BF16),
            pltpu.VMEM((d, tf), BF16),
            pltpu.VMEM((d, tf), BF16),
        ],
    )
    return pl.pallas_call(
        _expert_kernel,
        grid_spec=grid_spec,
        out_shape=jax.ShapeDtypeStruct((n_rows, d // 2), U32),
        compiler_params=_cparams(("arbitrary", "arbitrary")),
        name="experts",
    )(visit_expert, visit_rows, visit_blk, xs, w_gu, w_gu, b_gu3, b_gu3, w_down, b_down3)


def _combine_kernel(pos_ref, h_ref, gate_ref, g_ref, ys_ref, o_ref, buf_ref, sem):
    tm = h_ref.shape[0]
    half = EXPERT_TF // 2

    def row_copy(r, kk, p):
        return pltpu.make_async_copy(ys_ref.at[pl.ds(p, 1)], buf_ref.at[kk, pl.ds(r, 1)], sem)

    def issue(r, carry):
        for kk in range(TOP_K):
            row_copy(r, kk, pos_ref[0, 0, r * TOP_K + kk]).start()
        return carry

    lax.fori_loop(0, tm, issue, 0, unroll=DMA_ISSUE_UNROLL)
    for _ in range(tm * TOP_K):
        row_copy(0, 0, 0).wait()

    gates = gate_ref[...]
    y = h_ref[...]
    for kk in range(TOP_K):
        w = buf_ref[kk]
        lo = _unpack_lo(w)
        hi = _unpack_hi(w)
        pieces = []
        for n in range(w.shape[1] // half):
            pieces += [lo[:, n * half:(n + 1) * half], hi[:, n * half:(n + 1) * half]]
        y = y + gates[:, kk:kk + 1] * jnp.concatenate(pieces, axis=1)
    o_ref[...] = _rms(y, g_ref[...])


def _combine(h2, gates, pos, ys, g_final):
    t, d = h2.shape
    tm = min(COMBINE_TM, t)
    pos3 = pos.reshape(t // tm, 1, tm * TOP_K)
    return pl.pallas_call(
        _combine_kernel,
        grid=(t // tm,),
        in_specs=[
            pl.BlockSpec((1, 1, tm * TOP_K), lambda i: (i, 0, 0), memory_space=pltpu.SMEM),
            pl.BlockSpec((tm, d), lambda i: (i, 0)),
            pl.BlockSpec((tm, TOP_K), lambda i: (i, 0)),
            pl.BlockSpec((1, d), lambda i: (0, 0)),
            pl.BlockSpec(memory_space=pl.ANY),
        ],
        out_specs=pl.BlockSpec((tm, d), lambda i: (i, 0)),
        out_shape=jax.ShapeDtypeStruct((t, d), F32),
        scratch_shapes=[pltpu.VMEM((TOP_K, tm, ys.shape[1]), ys.dtype), pltpu.SemaphoreType.DMA(())],
        compiler_params=_cparams(("arbitrary",)),
        name="combine",
    )(pos3, h2, gates, g_final, ys)


def _routing_tables(counts, top_idx, rank, n_visits):
    r, ch = EXPERT_ROWS, EXPERT_CHUNK
    counts = counts.reshape(N_EXPERTS)
    padded = (counts + r - 1) // r * r
    ends = jnp.cumsum(padded)
    starts = ends - padded
    pos = starts[top_idx] + rank
    visit_row0 = jnp.arange(n_visits, dtype=jnp.int32) * r
    owner = jnp.sum((ends[None, :] <= visit_row0[:, None]).astype(jnp.int32), axis=1)
    visit_expert = jnp.minimum(owner, N_EXPERTS - 1).astype(jnp.int32)
    visit_rows = jnp.clip(counts[visit_expert] - (visit_row0 - starts[visit_expert]), 0, r).astype(jnp.int32)
    n_used = (ends[-1] // r).astype(jnp.int32)
    last = jnp.maximum(n_used - 1, 0)
    ids = jnp.arange(n_visits, dtype=jnp.int32)
    active = ids < n_used
    visit_rows = jnp.where(active, visit_rows, 0)
    visit_blk = jnp.where(active, ids, last)
    visit_expert = jnp.where(active, visit_expert, visit_expert[last])
    tail = starts + jnp.minimum(counts // ch * ch, padded - ch)
    tail = jnp.where(counts > 0, tail, -1)
    return pos.astype(jnp.int32), visit_expert, visit_rows, visit_blk, tail.astype(jnp.int32)


def kernel(x, mem, norm_mix_g, w_in, sinks, lambda_q1, lambda_k1, lambda_q2, lambda_k2, diff_subln_g, w_branch, w_o, norm_cross_g, norm_mem_g, w_cq, w_ckv, w_co, norm_ffn_g, w_router, b_router, w_gate_up, b_gate_up, w_down, b_down, norm_final_g):
    b, s, d = x.shape
    t = b * s
    x2 = x.reshape(t, d)
    l = 0

    w = w_in[l]
    o_qa, o_ka, o_va = 0, SWA_WIDTH, SWA_WIDTH + SWA_KV_WIDTH
    o_qd = o_va + SWA_KV_WIDTH
    o_kd, o_vd, o_g = o_qd + DIFF_WIDTH, o_qd + 2 * DIFF_WIDTH, o_qd + 3 * DIFF_WIDTH
    w_perm = jnp.concatenate([
        w[:, o_g:], w[:, o_qd:o_kd] * jnp.float32(HEAD_DIM ** -0.5 * LOG2E), w[:, o_kd:o_vd], w[:, o_vd:o_g],
        w[:, o_qa:o_ka], w[:, o_ka:o_va], w[:, o_va:o_qd]], axis=1).astype(BF16)
    proj = _inproj(x2, norm_mix_g[l].reshape(1, d), w_perm)
    c_qd = 2 * d
    c_kd, c_vd, c_qa = c_qd + DIFF_WIDTH, c_qd + 2 * DIFF_WIDTH, c_qd + 3 * DIFF_WIDTH
    c_ka, c_va = c_qa + SWA_WIDTH, c_qa + SWA_WIDTH + SWA_KV_WIDTH

    def seg(c0, width):
        return proj[:, c0:c0 + width].reshape(b, s, width)

    qaT = jnp.swapaxes(seg(c_qa, SWA_WIDTH), 1, 2)
    vaT = jnp.swapaxes(seg(c_va, SWA_KV_WIDTH), 1, 2)
    yaT = _swa(qaT, seg(c_ka, SWA_KV_WIDTH), vaT, sinks[l])
    ya = jnp.swapaxes(yaT, 1, 2).reshape(t, SWA_WIDTH)

    td = min(DIFF_T, s)
    nk = s // td
    hw = DIFF_HEAD_WIDTH
    cs, qx, kx = _diff_tables(td)
    qdT = jnp.swapaxes(seg(c_qd, DIFF_WIDTH), 1, 2)
    k5 = seg(c_kd, DIFF_WIDTH).reshape(b, nk, td, DIFF_HEADS, hw).transpose(0, 3, 1, 2, 4)
    k5 = jnp.concatenate([k5, jnp.broadcast_to(kx, (b, DIFF_HEADS, nk, td, hw))], axis=-1)
    v5 = seg(c_vd, DIFF_WIDTH).reshape(b, nk, td, DIFF_HEADS, hw).transpose(0, 3, 1, 4, 2)
    ones = jnp.zeros((V7X_SUBLANES, td), BF16).at[0].set(1.0)
    v5 = jnp.concatenate([v5, jnp.broadcast_to(ones, (b, DIFF_HEADS, nk, V7X_SUBLANES, td))], axis=-2)
    lamv = jnp.stack([lambda_q1[l], lambda_k1[l], lambda_q2[l], lambda_k2[l]]).astype(F32)
    ydT = _diff(qdT, k5, v5, qx, cs, lamv, diff_subln_g[l].astype(F32).reshape(hw, 1))
    yd = jnp.swapaxes(ydT, 1, 2).reshape(t, DIFF_WIDTH)

    h1 = _merge(ya, yd, proj, x2, w_branch[l].astype(BF16), w_o[l].astype(BF16))

    kmem, vmem = _memkv(mem, norm_mem_g[l].reshape(1, d), w_ckv[l].astype(BF16))
    h2, xn, top_idx, gates, rank, counts = _cross_router(
        h1, kmem, vmem, norm_cross_g[l].reshape(1, d), w_cq[l].astype(BF16), w_co[l].astype(BF16),
        norm_ffn_g[l].reshape(1, d), w_router[l].astype(BF16), b_router[l].reshape(1, N_EXPERTS), s)

    n_visits = (t * TOP_K) // EXPERT_ROWS + N_EXPERTS
    n_rows = n_visits * EXPERT_ROWS
    pos, visit_expert, visit_rows, visit_blk, tails = _routing_tables(counts, top_idx, rank, n_visits)
    xs = _dispatch(xn, pos, tails, n_rows)
    ys = _experts(xs, visit_expert, visit_rows, visit_blk, w_gate_up[l], b_gate_up[l], w_down[l], b_down[l])
    out = _combine(h2, gates, pos, ys, norm_final_g.reshape(1, d))
    return out.reshape(b, s, d)
```

```python
import functools
import math

import jax
import jax.numpy as jnp
from jax import lax
from jax.experimental import pallas as pl
from jax.experimental.pallas import tpu as pltpu

BF16 = jnp.bfloat16
F32 = jnp.float32
U32 = jnp.uint32
NEG_INF = float("-inf")
LOG2E = math.log2(math.e)

HEAD_DIM = 64
SWA_Q_HEADS = 16
SWA_KV_HEADS = 4
SWA_GROUP = SWA_Q_HEADS // SWA_KV_HEADS
SWA_BLOCK = 128
SWA_WIDTH = SWA_Q_HEADS * HEAD_DIM
SWA_KV_WIDTH = SWA_KV_HEADS * HEAD_DIM
DIFF_HEADS = 8
DIFF_HEAD_WIDTH = 2 * HEAD_DIM
DIFF_WIDTH = DIFF_HEADS * DIFF_HEAD_WIDTH
X_HEADS = 4
X_HEAD_DIM = 128
X_WIDTH = X_HEADS * X_HEAD_DIM
N_EXPERTS = 32
TOP_K = 4
SWIGLU_LIMIT = 7.0
SWIGLU_ALPHA = 1.702
RMS_EPS = 1e-5
LAMBDA_INIT = 0.8 - 0.6 * math.exp(-0.3 * 0)

V7X_VMEM_BYTES = 64 * 1024 * 1024
VMEM_LIMIT = V7X_VMEM_BYTES - 8 * 1024 * 1024
V7X_LANES = 128
V7X_SUBLANES = 8

INPROJ_TM = 512
INPROJ_TN = 2176
DIFF_T = 512
DIFF_POS_SPLIT = 256
DIFF_DENOM_ROWS = 16
MERGE_TM = 256
CROSS_TM = 256
DISPATCH_TM = 256
EXPERT_CHUNK = 768
EXPERT_ROWS = 3 * EXPERT_CHUNK
EXPERT_TF = 256
COMBINE_TM = 256
DMA_ISSUE_UNROLL = 8


def _cparams(semantics):
    return pltpu.CompilerParams(dimension_semantics=semantics, vmem_limit_bytes=VMEM_LIMIT)


def _rms(x, g):
    ms = jnp.mean(x * x, axis=-1, keepdims=True)
    return x * lax.rsqrt(ms + RMS_EPS) * g


def _pack_pair(lo, hi):
    lo_bits = lax.bitcast_convert_type(lo.astype(BF16).astype(F32), U32)
    hi_bits = lax.bitcast_convert_type(hi.astype(BF16).astype(F32), U32)
    return hi_bits | (lo_bits >> 16)


def _unpack_lo(w):
    return lax.bitcast_convert_type(w << 16, F32)


def _unpack_hi(w):
    return lax.bitcast_convert_type(w & jnp.uint32(0xFFFF0000), F32)


def _inproj_kernel(x_ref, g_ref, w_ref, o_ref, xn_ref):
    @pl.when(pl.program_id(1) == 0)
    def _():
        xn_ref[...] = _rms(x_ref[...], g_ref[...]).astype(BF16)

    o_ref[...] = jnp.dot(xn_ref[...], w_ref[...], preferred_element_type=F32).astype(o_ref.dtype)


def _inproj(x2, g, w):
    t, d = x2.shape
    n = w.shape[1]
    tm, tn = min(INPROJ_TM, t), INPROJ_TN
    return pl.pallas_call(
        _inproj_kernel,
        grid=(t // tm, n // tn),
        in_specs=[
            pl.BlockSpec((tm, d), lambda i, j: (i, 0)),
            pl.BlockSpec((1, d), lambda i, j: (0, 0)),
            pl.BlockSpec((d, tn), lambda i, j: (0, j)),
        ],
        out_specs=pl.BlockSpec((tm, tn), lambda i, j: (i, j)),
        out_shape=jax.ShapeDtypeStruct((t, n), BF16),
        scratch_shapes=[pltpu.VMEM((tm, d), BF16)],
        compiler_params=_cparams(("parallel", "arbitrary")),
        name="inproj",
    )(x2, g, w)


def _swa_kernel(qT_ref, kp_ref, kc_ref, vp_ref, vc_ref, bias_ref, sink_ref, o_ref):
    n = pl.program_id(1)
    blk = SWA_BLOCK
    kband = jnp.concatenate([kp_ref[...], kc_ref[...]], axis=0)
    vband = jnp.concatenate([vp_ref[...], vc_ref[...]], axis=1)
    krow = lax.broadcasted_iota(jnp.int32, (2 * blk, SWA_GROUP * blk), 0)
    has_prev = n > 0
    zeros = jnp.zeros((HEAD_DIM, blk), BF16)
    for h in range(SWA_KV_HEADS):
        lane0 = 2 * HEAD_DIM * (h // 2)
        k128 = kband[:, lane0:lane0 + 2 * HEAD_DIM]
        pieces = []
        for g in range(SWA_GROUP):
            hq = SWA_GROUP * h + g
            qh = qT_ref[HEAD_DIM * hq:HEAD_DIM * (hq + 1), :]
            pieces.append(jnp.concatenate([qh, zeros] if h % 2 == 0 else [zeros, qh], axis=0))
        qz = jnp.concatenate(pieces, axis=1)
        s = jnp.dot(k128, qz, preferred_element_type=F32) * (HEAD_DIM ** -0.5)
        s = s + bias_ref[h]
        s = jnp.where((krow >= blk) | has_prev, s, NEG_INF)
        sink = sink_ref[h]
        m = jnp.maximum(jnp.max(s, axis=0, keepdims=True), sink)
        p = jnp.exp(s - m)
        denom = jnp.sum(p, axis=0, keepdims=True) + jnp.exp(sink - m)
        vh = vband[HEAD_DIM * h:HEAD_DIM * (h + 1), :]
        o = jnp.dot(vh, p.astype(BF16), preferred_element_type=F32) / denom
        for g in range(SWA_GROUP):
            hq = SWA_GROUP * h + g
            o_ref[HEAD_DIM * hq:HEAD_DIM * (hq + 1), :] = o[:, blk * g:blk * (g + 1)].astype(o_ref.dtype)


def _swa_tables(sinks):
    blk = SWA_BLOCK
    slopes = (2.0 ** (-8.0 * jnp.arange(1, SWA_Q_HEADS + 1, dtype=F32) / SWA_Q_HEADS)).reshape(SWA_KV_HEADS, SWA_GROUP)
    q_pos = jnp.arange(blk)[None, :] + blk
    k_pos = jnp.arange(2 * blk)[:, None]
    dist = q_pos - k_pos
    allowed = (dist >= 0) & (dist < SWA_BLOCK)
    bias = -slopes[:, :, None, None] * dist.astype(F32)[None, None]
    bias = jnp.where(allowed[None, None], bias, NEG_INF)
    bias = jnp.transpose(bias, (0, 2, 1, 3)).reshape(SWA_KV_HEADS, 2 * blk, SWA_GROUP * blk)
    sink = sinks.astype(F32).reshape(SWA_KV_HEADS, SWA_GROUP, 1)
    sink = jnp.broadcast_to(sink, (SWA_KV_HEADS, SWA_GROUP, blk)).reshape(SWA_KV_HEADS, 1, SWA_GROUP * blk)
    return bias, sink


def _swa(qT, k, vT, sinks):
    b, _, s = qT.shape
    blk = SWA_BLOCK
    bias, sink = _swa_tables(sinks)
    prev = lambda n: jnp.maximum(n - 1, 0)
    return pl.pallas_call(
        _swa_kernel,
        grid=(b, s // blk),
        in_specs=[
            pl.BlockSpec((None, SWA_WIDTH, blk), lambda bi, n: (bi, 0, n)),
            pl.BlockSpec((None, blk, SWA_KV_WIDTH), lambda bi, n: (bi, prev(n), 0)),
            pl.BlockSpec((None, blk, SWA_KV_WIDTH), lambda bi, n: (bi, n, 0)),
            pl.BlockSpec((None, SWA_KV_WIDTH, blk), lambda bi, n: (bi, 0, prev(n))),
            pl.BlockSpec((None, SWA_KV_WIDTH, blk), lambda bi, n: (bi, 0, n)),
            pl.BlockSpec(bias.shape, lambda bi, n: (0, 0, 0)),
            pl.BlockSpec(sink.shape, lambda bi, n: (0, 0, 0)),
        ],
        out_specs=pl.BlockSpec((None, SWA_WIDTH, blk), lambda bi, n: (bi, 0, n)),
        out_shape=jax.ShapeDtypeStruct((b, SWA_WIDTH, s), BF16),
        compiler_params=_cparams(("parallel", "arbitrary")),
        name="swa",
    )(qT, k, k, vT, vT, bias, sink)


def _diff_kernel(c_ref, qT_ref, qx_ref, k_ref, kx_ref, v_ref, ones_ref, lamv_ref, g_ref, o_ref,
                 q2_ref, z_ref, zm_ref, p_ref, a_ref, m_ref, acc_ref, *, t):
    h = pl.program_id(1)
    qi = pl.program_id(2)
    w = DIFF_HEAD_WIDTH
    c = c_ref[h]

    @pl.when(qi == 0)
    def _():
        q2_ref[w:, :] = qx_ref[...]

    qt = qT_ref[...]
    row = lax.broadcasted_iota(jnp.int32, qt.shape, 0)
    zero = jnp.zeros_like(qt)
    q2_ref[:w, :t] = jnp.where(row < HEAD_DIM, qt, zero)
    q2_ref[:w, t:] = jnp.where(row >= HEAD_DIM, qt, zero)
    m_ref[...] = jnp.full(m_ref.shape, NEG_INF, F32)
    acc_ref[...] = jnp.zeros(acc_ref.shape, F32)
    p_ref[1] = jnp.zeros(p_ref.shape[1:], BF16)
    a_ref[1] = jnp.ones(a_ref.shape[1:], F32)

    def scores(k, slot):
        keys = jnp.concatenate([k_ref[k], kx_ref[...]], axis=1)
        z = jnp.dot(keys, q2_ref[...], preferred_element_type=F32)
        z_ref[slot] = z
        zm_ref[slot] = jnp.max(z, axis=0, keepdims=True)

    def values(k, slot):
        vals = jnp.concatenate([v_ref[jnp.maximum(k, 0)], ones_ref[...]], axis=0)
        pv = jnp.dot(vals, p_ref[slot], preferred_element_type=F32)
        acc_ref[...] = a_ref[slot] * acc_ref[...] + pv

    def softmax(k, slot, masked):
        z = z_ref[slot]
        if masked:
            kr = lax.broadcasted_iota(jnp.int32, z.shape, 0)
            qc = lax.broadcasted_iota(jnp.int32, z.shape, 1)
            qc = jnp.where(qc >= t, qc - t, qc)
            z = jnp.where(kr <= qc, z, NEG_INF)
            zmax = jnp.max(z, axis=0, keepdims=True)
        else:
            zmax = zm_ref[slot]
        off = c * ((k - qi) * t).astype(F32)
        m_old = m_ref[...]
        m_new = jnp.maximum(m_old, zmax + off)
        p_ref[slot] = jnp.exp2(z - (m_new - off)).astype(BF16)
        a_ref[slot] = jnp.exp2(m_old - m_new)
        m_ref[...] = m_new

    scores(0, 0)

    def pair(j, carry):
        k = 2 * j
        scores(k + 1, 1)
        values(k - 1, 1)
        softmax(k, 0, False)
        scores(k + 2, 0)
        values(k, 0)
        softmax(k + 1, 1, False)
        return carry

    lax.fori_loop(0, qi >> 1, pair, 0)

    @pl.when((qi & 1) == 0)
    def _():
        values(qi - 1, 1)
        softmax(qi, 0, True)
        values(qi, 0)

    @pl.when((qi & 1) == 1)
    def _():
        scores(qi, 1)
        values(qi - 2, 1)
        softmax(qi - 1, 0, False)
        values(qi - 1, 0)
        softmax(qi, 1, True)
        values(qi, 1)

    lamv = lamv_ref[...]
    lam = (jnp.exp(jnp.sum(lamv[0:1] * lamv[1:2], axis=-1, keepdims=True))
           - jnp.exp(jnp.sum(lamv[2:3] * lamv[3:4], axis=-1, keepdims=True)) + LAMBDA_INIT)
    acc = acc_ref[...]
    l = acc[w:w + 1, :]
    o = acc[:w, :t] / l[:, :t] - lam * (acc[:w, t:] / l[:, t:])
    ms = jnp.mean(o * o, axis=0, keepdims=True)
    y = o * lax.rsqrt(ms + RMS_EPS) * g_ref[...] * (1.0 - LAMBDA_INIT)
    o_ref[...] = y.astype(o_ref.dtype)


def _diff_tables(t):
    w = DIFF_HEAD_WIDTH
    slopes = 2.0 ** (-8.0 * jnp.arange(1, DIFF_HEADS + 1, dtype=F32) / DIFF_HEADS)
    c = slopes * jnp.float32(LOG2E)
    c1 = c.astype(BF16)
    c2 = (c - c1.astype(F32)).astype(BF16)
    c3 = (c - c1.astype(F32) - c2.astype(F32)).astype(BF16)
    parts = jnp.stack([c1, c2, c3, c1, c2, c3], axis=1)
    qx = jnp.zeros((DIFF_HEADS, w, 2 * t), BF16).at[:, :6, :].set(parts[:, :, None])
    r = jnp.arange(t)
    r_lo = (r % DIFF_POS_SPLIT).astype(BF16)
    r_hi = ((r // DIFF_POS_SPLIT) * DIFF_POS_SPLIT).astype(BF16)
    kx = jnp.zeros((t, w), BF16).at[:, 0:3].set(r_lo[:, None]).at[:, 3:6].set(r_hi[:, None])
    return c, qx, kx


def _diff(qT, k5, v5, qx, kx, ones, cs, lamv, subln_g):
    b, _, s = qT.shape
    t = k5.shape[3]
    nk = s // t
    w = DIFF_HEAD_WIDTH
    wv = w + ones.shape[0]
    return pl.pallas_call(
        functools.partial(_diff_kernel, t=t),
        grid=(b, DIFF_HEADS, nk),
        in_specs=[
            pl.BlockSpec(memory_space=pltpu.SMEM),
            pl.BlockSpec((None, w, t), lambda bi, h, qi: (bi, h, qi)),
            pl.BlockSpec((None, w, 2 * t), lambda bi, h, qi: (h, 0, 0)),
            pl.BlockSpec((None, None, nk, t, w), lambda bi, h, qi: (bi, h, 0, 0, 0)),
            pl.BlockSpec((t, w), lambda bi, h, qi: (0, 0)),
            pl.BlockSpec((None, None, nk, w, t), lambda bi, h, qi: (bi, h, 0, 0, 0)),
            pl.BlockSpec(ones.shape, lambda bi, h, qi: (0, 0)),
            pl.BlockSpec((4, HEAD_DIM), lambda bi, h, qi: (0, 0)),
            pl.BlockSpec((w, 1), lambda bi, h, qi: (0, 0)),
        ],
        out_specs=pl.BlockSpec((None, w, t), lambda bi, h, qi: (bi, h, qi)),
        out_shape=jax.ShapeDtypeStruct((b, DIFF_WIDTH, s), BF16),
        scratch_shapes=[
            pltpu.VMEM((2 * w, 2 * t), BF16),
            pltpu.VMEM((2, t, 2 * t), F32),
            pltpu.VMEM((2, 1, 2 * t), F32),
            pltpu.VMEM((2, t, 2 * t), BF16),
            pltpu.VMEM((2, 1, 2 * t), F32),
            pltpu.VMEM((1, 2 * t), F32),
            pltpu.VMEM((wv, 2 * t), F32),
        ],
        compiler_params=_cparams(("arbitrary", "arbitrary", "arbitrary")),
        name="diffattn",
    )(cs, qT, qx, k5, kx, v5, ones, lamv, subln_g)


def _merge_kernel(ya_ref, yd_ref, g0_ref, g1_ref, x_ref, wb0_ref, wb1_ref, wo_ref, o_ref):
    ua = jnp.dot(ya_ref[...], wb0_ref[...], preferred_element_type=F32)
    ud = jnp.dot(yd_ref[...], wb1_ref[...], preferred_element_type=F32)
    merged = jax.nn.sigmoid(g0_ref[...].astype(F32)) * ua + jax.nn.sigmoid(g1_ref[...].astype(F32)) * ud
    o_ref[...] = x_ref[...] + jnp.dot(merged.astype(BF16), wo_ref[...], preferred_element_type=F32)


def _merge(ya, yd, proj, x2, wb, wo):
    t, d = x2.shape
    tm = min(MERGE_TM, t)
    bw = ya.shape[1]
    const = dict(pipeline_mode=pl.Buffered(1))
    return pl.pallas_call(
        _merge_kernel,
        grid=(t // tm,),
        in_specs=[
            pl.BlockSpec((tm, bw), lambda i: (i, 0)),
            pl.BlockSpec((tm, bw), lambda i: (i, 0)),
            pl.BlockSpec((tm, d), lambda i: (i, 0)),
            pl.BlockSpec((tm, d), lambda i: (i, 1)),
            pl.BlockSpec((tm, d), lambda i: (i, 0)),
            pl.BlockSpec((None, bw, d), lambda i: (0, 0, 0), **const),
            pl.BlockSpec((None, bw, d), lambda i: (1, 0, 0), **const),
            pl.BlockSpec((d, d), lambda i: (0, 0), **const),
        ],
        out_specs=pl.BlockSpec((tm, d), lambda i: (i, 0)),
        out_shape=jax.ShapeDtypeStruct((t, d), F32),
        compiler_params=_cparams(("parallel",)),
        name="merge",
    )(ya, yd, proj, proj, x2, wb, wb, wo)


def _memkv_kernel(mem_ref, g_ref, w_ref, k_ref, v_ref):
    mn = _rms(mem_ref[...], g_ref[...]).astype(BF16)
    kv = jnp.dot(mn, w_ref[...], preferred_element_type=F32)
    k_ref[...] = kv[:, :X_WIDTH].astype(k_ref.dtype)
    v_ref[...] = kv[:, X_WIDTH:].astype(v_ref.dtype)


def _memkv(mem, g, w):
    b, m, d = mem.shape
    out = jax.ShapeDtypeStruct((b, m, X_WIDTH), BF16)
    return pl.pallas_call(
        _memkv_kernel,
        grid=(b,),
        in_specs=[
            pl.BlockSpec((None, m, d), lambda i: (i, 0, 0)),
            pl.BlockSpec((1, d), lambda i: (0, 0)),
            pl.BlockSpec((d, 2 * X_WIDTH), lambda i: (0, 0)),
        ],
        out_specs=[pl.BlockSpec((None, m, X_WIDTH), lambda i: (i, 0, 0))] * 2,
        out_shape=[out, out],
        compiler_params=_cparams(("parallel",)),
        name="memkv",
    )(mem, g, w)


def _cross_router_kernel(h_ref, k_ref, v_ref, gc_ref, wq_ref, wo_ref, gf_ref, wr_ref, br_ref,
                         h2_ref, xn_ref, idx_ref, gate_ref, rank_ref, cnt_ref, run_ref):
    i = pl.program_id(0)
    tm, d = h_ref.shape

    @pl.when(i == 0)
    def _():
        run_ref[...] = jnp.zeros(run_ref.shape, F32)

    h = h_ref[...]
    hn = _rms(h, gc_ref[...]).astype(BF16)
    q = jnp.dot(hn, wq_ref[...], preferred_element_type=F32).astype(BF16)
    k = k_ref[...]
    v = v_ref[...]
    outs = []
    for hd in range(X_HEADS):
        sl = slice(X_HEAD_DIM * hd, X_HEAD_DIM * (hd + 1))
        s = lax.dot_general(q[:, sl], k[:, sl], (((1,), (1,)), ((), ())),
                            preferred_element_type=F32) * (X_HEAD_DIM ** -0.5)
        s = s - jnp.max(s, axis=-1, keepdims=True)
        p = jnp.exp(s)
        p = p / jnp.sum(p, axis=-1, keepdims=True)
        outs.append(jnp.dot(p.astype(BF16), v[:, sl], preferred_element_type=F32))
    o = jnp.concatenate(outs, axis=1).astype(BF16)
    h2 = h + jnp.dot(o, wo_ref[...], preferred_element_type=F32)
    h2_ref[...] = h2

    xn = _rms(h2, gf_ref[...])
    packed = _pack_pair(xn[:, :d // 2], xn[:, d // 2:])
    for j in range(V7X_SUBLANES):
        xn_ref[pl.ds(j, tm, stride=V7X_SUBLANES), :] = packed[:, j * V7X_LANES:(j + 1) * V7X_LANES]
    logits = jnp.dot(xn.astype(BF16), wr_ref[...], preferred_element_type=F32) + br_ref[...]

    lane = lax.broadcasted_iota(jnp.int32, logits.shape, 1).astype(F32)
    work = logits
    vals, idxs = [], []
    onehot = jnp.zeros(logits.shape, F32)
    for _ in range(TOP_K):
        mx = jnp.max(work, axis=-1, keepdims=True)
        ix = jnp.min(jnp.where(work == mx, lane, float(N_EXPERTS)), axis=-1, keepdims=True)
        sel = lane == ix
        vals.append(mx)
        idxs.append(ix)
        onehot = jnp.where(sel, 1.0, onehot)
        work = jnp.where(sel, NEG_INF, work)
    es = [jnp.exp(vv - vals[0]) for vv in vals]
    tot = es[0] + es[1] + es[2] + es[3]

    r = lax.broadcasted_iota(jnp.int32, (tm, tm), 0)
    c = lax.broadcasted_iota(jnp.int32, (tm, tm), 1)
    lower = jnp.where(c < r, 1.0, 0.0).astype(BF16)
    prefix = jnp.dot(lower, onehot.astype(BF16), preferred_element_type=F32) + run_ref[...]
    for kk in range(TOP_K):
        idx_ref[:, kk:kk + 1] = idxs[kk].astype(jnp.int32)
        gate_ref[:, kk:kk + 1] = es[kk] / tot
        rk = jnp.sum(jnp.where(lane == idxs[kk], prefix, 0.0), axis=-1, keepdims=True)
        rank_ref[:, kk:kk + 1] = rk.astype(jnp.int32)
    run_ref[...] = run_ref[...] + jnp.sum(onehot, axis=0, keepdims=True)
    cnt_ref[...] = run_ref[...].astype(jnp.int32)


def _cross_router(h1, kmem, vmem, gc, wq, wo, gf, wr, br, seq):
    t, d = h1.shape
    tm = min(CROSS_TM, seq)
    per_b = seq // tm
    m = kmem.shape[1]
    const = dict(pipeline_mode=pl.Buffered(1))
    outs = pl.pallas_call(
        _cross_router_kernel,
        grid=(t // tm,),
        in_specs=[
            pl.BlockSpec((tm, d), lambda i: (i, 0)),
            pl.BlockSpec((None, m, X_WIDTH), lambda i: (i // per_b, 0, 0)),
            pl.BlockSpec((None, m, X_WIDTH), lambda i: (i // per_b, 0, 0)),
            pl.BlockSpec((1, d), lambda i: (0, 0)),
            pl.BlockSpec((d, X_WIDTH), lambda i: (0, 0), **const),
            pl.BlockSpec((X_WIDTH, d), lambda i: (0, 0), **const),
            pl.BlockSpec((1, d), lambda i: (0, 0)),
            pl.BlockSpec((d, N_EXPERTS), lambda i: (0, 0), **const),
            pl.BlockSpec((1, N_EXPERTS), lambda i: (0, 0)),
        ],
        out_specs=[
            pl.BlockSpec((tm, d), lambda i: (i, 0)),
            pl.BlockSpec((tm * V7X_SUBLANES, V7X_LANES), lambda i: (i, 0)),
            pl.BlockSpec((tm, TOP_K), lambda i: (i, 0)),
            pl.BlockSpec((tm, TOP_K), lambda i: (i, 0)),
            pl.BlockSpec((tm, TOP_K), lambda i: (i, 0)),
            pl.BlockSpec((1, N_EXPERTS), lambda i: (0, 0)),
        ],
        out_shape=[
            jax.ShapeDtypeStruct((t, d), F32),
            jax.ShapeDtypeStruct((t * V7X_SUBLANES, V7X_LANES), U32),
            jax.ShapeDtypeStruct((t, TOP_K), jnp.int32),
            jax.ShapeDtypeStruct((t, TOP_K), F32),
            jax.ShapeDtypeStruct((t, TOP_K), jnp.int32),
            jax.ShapeDtypeStruct((1, N_EXPERTS), jnp.int32),
        ],
        scratch_shapes=[pltpu.VMEM((1, N_EXPERTS), F32)],
        compiler_params=_cparams(("arbitrary",)),
        name="cross_router",
    )(h1, kmem, vmem, gc, wq, wo, gf, wr, br)
    return outs


def _tile_rows(row):
    return pl.ds(pl.multiple_of(row * V7X_SUBLANES, V7X_SUBLANES), V7X_SUBLANES)


def _dispatch_kernel(pos_ref, tail_ref, x_ref, xs_ref, zero_ref, sem, zsem):
    i = pl.program_id(0)
    tm = x_ref.shape[0] // V7X_SUBLANES
    chunk_rows = zero_ref.shape[0]

    @pl.when(i == 0)
    def _():
        zero_ref[...] = jnp.zeros(zero_ref.shape, zero_ref.dtype)

        def tail_copy(e):
            row0 = pl.multiple_of(tail_ref[e] * V7X_SUBLANES, V7X_SUBLANES)
            return pltpu.make_async_copy(zero_ref, xs_ref.at[pl.ds(row0, chunk_rows)], zsem)

        for e in range(N_EXPERTS):
            @pl.when(tail_ref[e] >= 0)
            def _():
                tail_copy(e).start()
        for e in range(N_EXPERTS):
            @pl.when(tail_ref[e] >= 0)
            def _():
                tail_copy(e).wait()

    def row_copy(r, p):
        return pltpu.make_async_copy(x_ref.at[_tile_rows(r)], xs_ref.at[_tile_rows(p)], sem)

    def issue(r, carry):
        for kk in range(TOP_K):
            row_copy(r, pos_ref[0, 0, r * TOP_K + kk]).start()
        return carry

    lax.fori_loop(0, tm, issue, 0, unroll=DMA_ISSUE_UNROLL)
    for _ in range(tm * TOP_K):
        row_copy(0, 0).wait()


def _dispatch(xn, pos, tails, n_rows):
    t = xn.shape[0] // V7X_SUBLANES
    tm = min(DISPATCH_TM, t)
    pos3 = pos.reshape(t // tm, 1, tm * TOP_K)
    return pl.pallas_call(
        _dispatch_kernel,
        grid=(t // tm,),
        in_specs=[
            pl.BlockSpec((1, 1, tm * TOP_K), lambda i: (i, 0, 0), memory_space=pltpu.SMEM),
            pl.BlockSpec(memory_space=pltpu.SMEM),
            pl.BlockSpec((tm * V7X_SUBLANES, V7X_LANES), lambda i: (i, 0)),
        ],
        out_specs=pl.BlockSpec(memory_space=pl.ANY),
        out_shape=jax.ShapeDtypeStruct((n_rows * V7X_SUBLANES, V7X_LANES), xn.dtype),
        scratch_shapes=[pltpu.VMEM((EXPERT_CHUNK * V7X_SUBLANES, V7X_LANES), xn.dtype),
                        pltpu.SemaphoreType.DMA(()), pltpu.SemaphoreType.DMA(())],
        compiler_params=_cparams(("arbitrary",)),
        name="dispatch",
    )(pos3, tails, xn)


def _expert_kernel(ve_ref, vr_ref, vb_ref, x_ref, wg_ref, wl_ref, bg_ref, bl_ref, wd_ref, bd_ref, o_ref,
                   act_ref, wa_ref, wb_ref):
    del ve_ref, vb_ref
    v = pl.program_id(0)
    s = pl.program_id(1)
    rows = vr_ref[v]
    nj = act_ref.shape[0]
    tf = act_ref.shape[2]
    ch = EXPERT_CHUNK
    n_chunks = x_ref.shape[0] // (ch * V7X_SUBLANES)
    half = o_ref.shape[1]

    @pl.when((rows > 0) & (s < nj))
    def _():
        wa_ref[...] = wg_ref[...].astype(BF16)
        wb_ref[...] = wl_ref[...].astype(BF16)
        for c in range(n_chunks):
            @pl.when(c * ch < rows)
            def _():
                xw = [x_ref[pl.ds(c * ch * V7X_SUBLANES + j, ch, stride=V7X_SUBLANES), :]
                      for j in range(V7X_SUBLANES)]
                xb = jnp.concatenate([_unpack_lo(u).astype(BF16) for u in xw]
                                     + [_unpack_hi(u).astype(BF16) for u in xw], axis=1)
                glu = jnp.dot(xb, wa_ref[...], preferred_element_type=F32) + bg_ref[...]
                lin = jnp.dot(xb, wb_ref[...], preferred_element_type=F32) + bl_ref[...]
                glu = jnp.minimum(glu, SWIGLU_LIMIT)
                lin = jnp.clip(lin, -SWIGLU_LIMIT, SWIGLU_LIMIT)
                act = glu * jax.nn.sigmoid(SWIGLU_ALPHA * glu) * (lin + 1.0)
                act_ref[s, pl.ds(c * ch, ch), :] = act.astype(BF16)

    @pl.when((rows > 0) & (s >= nj))
    def _():
        wa_ref[...] = wd_ref[...].astype(BF16)
        for c in range(n_chunks):
            sl = pl.ds(c * ch, ch)

            @pl.when(c * ch < rows)
            def _():
                acc = jnp.dot(act_ref[0, sl, :], wa_ref[0:tf, :], preferred_element_type=F32)
                for j in range(1, nj):
                    acc = acc + jnp.dot(act_ref[j, sl, :], wa_ref[j * tf:(j + 1) * tf, :],
                                        preferred_element_type=F32)
                acc = acc + bd_ref[...]
                o_ref[sl, :] = _pack_pair(acc[:, :half], acc[:, half:])

            @pl.when(c * ch >= rows)
            def _():
                o_ref[sl, :] = jnp.zeros((ch, half), o_ref.dtype)


def _experts(xs, visit_expert, visit_rows, visit_blk, w_gu, b_gu, w_down, b_down):
    n_rows = xs.shape[0] // V7X_SUBLANES
    e, d, ff2 = w_gu.shape
    ff = ff2 // 2
    r, tf = EXPERT_ROWS, EXPERT_TF
    nj = ff // tf
    nn = d // tf
    b_gu3 = b_gu.reshape(e, 1, ff2)
    b_down3 = b_down.reshape(e, 1, d)

    def up(v, s, vr):
        return jnp.where(vr[v] > 0, jnp.minimum(s, nj - 1), nj - 1)

    def down(v, s, vr):
        return jnp.where(vr[v] > 0, jnp.maximum(s - nj, 0), nn - 1)

    grid_spec = pltpu.PrefetchScalarGridSpec(
        num_scalar_prefetch=3,
        grid=(n_rows // r, nj + nn),
        in_specs=[
            pl.BlockSpec((r * V7X_SUBLANES, V7X_LANES), lambda v, s, ve, vr, vb: (vb[v], 0)),
            pl.BlockSpec((None, d, tf), lambda v, s, ve, vr, vb: (ve[v], 0, up(v, s, vr))),
            pl.BlockSpec((None, d, tf), lambda v, s, ve, vr, vb: (ve[v], 0, nj + up(v, s, vr))),
            pl.BlockSpec((None, 1, tf), lambda v, s, ve, vr, vb: (ve[v], 0, up(v, s, vr))),
            pl.BlockSpec((None, 1, tf), lambda v, s, ve, vr, vb: (ve[v], 0, nj + up(v, s, vr))),
            pl.BlockSpec((None, ff, tf), lambda v, s, ve, vr, vb: (ve[v], 0, down(v, s, vr))),
            pl.BlockSpec((None, 1, tf), lambda v, s, ve, vr, vb: (ve[v], 0, down(v, s, vr))),
        ],
        out_specs=pl.BlockSpec((r, tf // 2), lambda v, s, ve, vr, vb: (vb[v], down(v, s, vr))),
        scratch_shapes=[
            pltpu.VMEM((nj, r, tf), BF16),
            pltpu.VMEM((d, tf), BF16),
            pltpu.VMEM((d, tf), BF16),
        ],
    )
    return pl.pallas_call(
        _expert_kernel,
        grid_spec=grid_spec,
        out_shape=jax.ShapeDtypeStruct((n_rows, d // 2), U32),
        compiler_params=_cparams(("arbitrary", "arbitrary")),
        name="experts",
    )(visit_expert, visit_rows, visit_blk, xs, w_gu, w_gu, b_gu3, b_gu3, w_down, b_down3)


def _combine_kernel(pos_ref, h_ref, gate_ref, g_ref, ys_ref, o_ref, buf_ref, sem):
    tm = h_ref.shape[0]
    half = EXPERT_TF // 2

    def row_copy(r, kk, p):
        return pltpu.make_async_copy(ys_ref.at[pl.ds(p, 1)], buf_ref.at[kk, pl.ds(r, 1)], sem)

    def issue(r, carry):
        for kk in range(TOP_K):
            row_copy(r, kk, pos_ref[0, 0, r * TOP_K + kk]).start()
        return carry

    lax.fori_loop(0, tm, issue, 0, unroll=DMA_ISSUE_UNROLL)
    for _ in range(tm * TOP_K):
        row_copy(0, 0, 0).wait()

    gates = gate_ref[...]
    y = h_ref[...]
    for kk in range(TOP_K):
        w = buf_ref[kk]
        lo = _unpack_lo(w)
        hi = _unpack_hi(w)
        pieces = []
        for n in range(w.shape[1] // half):
            pieces += [lo[:, n * half:(n + 1) * half], hi[:, n * half:(n + 1) * half]]
        y = y + gates[:, kk:kk + 1] * jnp.concatenate(pieces, axis=1)
    o_ref[...] = _rms(y, g_ref[...])


def _combine(h2, gates, pos, ys, g_final):
    t, d = h2.shape
    tm = min(COMBINE_TM, t)
    pos3 = pos.reshape(t // tm, 1, tm * TOP_K)
    return pl.pallas_call(
        _combine_kernel,
        grid=(t // tm,),
        in_specs=[
            pl.BlockSpec((1, 1, tm * TOP_K), lambda i: (i, 0, 0), memory_space=pltpu.SMEM),
            pl.BlockSpec((tm, d), lambda i: (i, 0)),
            pl.BlockSpec((tm, TOP_K), lambda i: (i, 0)),
            pl.BlockSpec((1, d), lambda i: (0, 0)),
            pl.BlockSpec(memory_space=pl.ANY),
        ],
        out_specs=pl.BlockSpec((tm, d), lambda i: (i, 0)),
        out_shape=jax.ShapeDtypeStruct((t, d), F32),
        scratch_shapes=[pltpu.VMEM((TOP_K, tm, ys.shape[1]), ys.dtype), pltpu.SemaphoreType.DMA(())],
        compiler_params=_cparams(("arbitrary",)),
        name="combine",
    )(pos3, h2, gates, g_final, ys)


def _routing_tables(counts, top_idx, rank, n_visits):
    r, ch = EXPERT_ROWS, EXPERT_CHUNK
    counts = counts.reshape(N_EXPERTS)
    padded = (counts + r - 1) // r * r
    ends = jnp.cumsum(padded)
    starts = ends - padded
    pos = starts[top_idx] + rank
    visit_row0 = jnp.arange(n_visits, dtype=jnp.int32) * r
    owner = jnp.sum((ends[None, :] <= visit_row0[:, None]).astype(jnp.int32), axis=1)
    visit_expert = jnp.minimum(owner, N_EXPERTS - 1).astype(jnp.int32)
    visit_rows = jnp.clip(counts[visit_expert] - (visit_row0 - starts[visit_expert]), 0, r).astype(jnp.int32)
    n_used = (ends[-1] // r).astype(jnp.int32)
    last = jnp.maximum(n_used - 1, 0)
    ids = jnp.arange(n_visits, dtype=jnp.int32)
    active = ids < n_used
    visit_rows = jnp.where(active, visit_rows, 0)
    visit_blk = jnp.where(active, ids, last)
    visit_expert = jnp.where(active, visit_expert, visit_expert[last])
    tail = starts + jnp.minimum(counts // ch * ch, padded - ch)
    tail = jnp.where(counts > 0, tail, -1)
    return pos.astype(jnp.int32), visit_expert, visit_rows, visit_blk, tail.astype(jnp.int32)


def kernel(x, mem, norm_mix_g, w_in, sinks, lambda_q1, lambda_k1, lambda_q2, lambda_k2, diff_subln_g, w_branch, w_o, norm_cross_g, norm_mem_g, w_cq, w_ckv, w_co, norm_ffn_g, w_router, b_router, w_gate_up, b_gate_up, w_down, b_down, norm_final_g):
    b, s, d = x.shape
    t = b * s
    x2 = x.reshape(t, d)
    l = 0

    w = w_in[l]
    o_qa, o_ka, o_va = 0, SWA_WIDTH, SWA_WIDTH + SWA_KV_WIDTH
    o_qd = o_va + SWA_KV_WIDTH
    o_kd, o_vd, o_g = o_qd + DIFF_WIDTH, o_qd + 2 * DIFF_WIDTH, o_qd + 3 * DIFF_WIDTH
    w_perm = jnp.concatenate([
        w[:, o_g:], w[:, o_qd:o_kd] * jnp.float32(HEAD_DIM ** -0.5 * LOG2E), w[:, o_kd:o_vd], w[:, o_vd:o_g],
        w[:, o_qa:o_ka], w[:, o_ka:o_va], w[:, o_va:o_qd]], axis=1).astype(BF16)
    proj = _inproj(x2, norm_mix_g[l].reshape(1, d), w_perm)
    c_qd = 2 * d
    c_kd, c_vd, c_qa = c_qd + DIFF_WIDTH, c_qd + 2 * DIFF_WIDTH, c_qd + 3 * DIFF_WIDTH
    c_ka, c_va = c_qa + SWA_WIDTH, c_qa + SWA_WIDTH + SWA_KV_WIDTH

    def seg(c0, width):
        return proj[:, c0:c0 + width].reshape(b, s, width)

    qaT = jnp.swapaxes(seg(c_qa, SWA_WIDTH), 1, 2)
    vaT = jnp.swapaxes(seg(c_va, SWA_KV_WIDTH), 1, 2)
    yaT = _swa(qaT, seg(c_ka, SWA_KV_WIDTH), vaT, sinks[l])
    ya = jnp.swapaxes(yaT, 1, 2).reshape(t, SWA_WIDTH)

    td = min(DIFF_T, s)
    nk = s // td
    hw = DIFF_HEAD_WIDTH
    cs, qx, kx = _diff_tables(td)
    qdT = jnp.swapaxes(seg(c_qd, DIFF_WIDTH), 1, 2)
    k5 = seg(c_kd, DIFF_WIDTH).reshape(b, nk, td, DIFF_HEADS, hw).transpose(0, 3, 1, 2, 4)
    v5 = seg(c_vd, DIFF_WIDTH).reshape(b, nk, td, DIFF_HEADS, hw).transpose(0, 3, 1, 4, 2)
    ones = jnp.zeros((DIFF_DENOM_ROWS, td), BF16).at[0].set(1.0)
    lamv = jnp.stack([lambda_q1[l], lambda_k1[l], lambda_q2[l], lambda_k2[l]]).astype(F32)
    ydT = _diff(qdT, k5, v5, qx, kx, ones, cs, lamv, diff_subln_g[l].astype(F32).reshape(hw, 1))
    yd = jnp.swapaxes(ydT, 1, 2).reshape(t, DIFF_WIDTH)

    h1 = _merge(ya, yd, proj, x2, w_branch[l].astype(BF16), w_o[l].astype(BF16))

    kmem, vmem = _memkv(mem, norm_mem_g[l].reshape(1, d), w_ckv[l].astype(BF16))
    h2, xn, top_idx, gates, rank, counts = _cross_router(
        h1, kmem, vmem, norm_cross_g[l].reshape(1, d), w_cq[l].astype(BF16), w_co[l].astype(BF16),
        norm_ffn_g[l].reshape(1, d), w_router[l].astype(BF16), b_router[l].reshape(1, N_EXPERTS), s)

    assert d // 2 == V7X_SUBLANES * V7X_LANES and EXPERT_TF // 2 == V7X_LANES
    n_visits = (t * TOP_K) // EXPERT_ROWS + N_EXPERTS
    n_rows = n_visits * EXPERT_ROWS
    pos, visit_expert, visit_rows, visit_blk, tails = _routing_tables(counts, top_idx, rank, n_visits)
    xs = _dispatch(xn, pos, tails, n_rows)
    ys = _experts(xs, visit_expert, visit_rows, visit_blk, w_gate_up[l], b_gate_up[l], w_down[l], b_down[l])
    out = _combine(h2, gates, pos, ys, norm_final_g.reshape(1, d))
    return out.reshape(b, s, d)
```

```python
import functools
import math

import jax
import jax.numpy as jnp
from jax import lax
from jax.experimental import pallas as pl
from jax.experimental.pallas import tpu as pltpu

BF16 = jnp.bfloat16
F32 = jnp.float32
U32 = jnp.uint32
NEG_INF = float("-inf")
LOG2E = math.log2(math.e)

HEAD_DIM = 64
SWA_Q_HEADS = 16
SWA_KV_HEADS = 4
SWA_GROUP = SWA_Q_HEADS // SWA_KV_HEADS
SWA_BLOCK = 128
SWA_WIDTH = SWA_Q_HEADS * HEAD_DIM
SWA_KV_WIDTH = SWA_KV_HEADS * HEAD_DIM
DIFF_HEADS = 8
DIFF_HEAD_WIDTH = 2 * HEAD_DIM
DIFF_WIDTH = DIFF_HEADS * DIFF_HEAD_WIDTH
X_HEADS = 4
X_HEAD_DIM = 128
X_WIDTH = X_HEADS * X_HEAD_DIM
N_EXPERTS = 32
TOP_K = 4
SWIGLU_LIMIT = 7.0
SWIGLU_ALPHA = 1.702
RMS_EPS = 1e-5
LAMBDA_INIT = 0.8 - 0.6 * math.exp(-0.3 * 0)

V7X_VMEM_BYTES = 64 * 1024 * 1024
VMEM_LIMIT = V7X_VMEM_BYTES - 8 * 1024 * 1024
V7X_LANES = 128
V7X_SUBLANES = 8

INPROJ_TM = 512
INPROJ_TN = 2176
DIFF_T = 512
DIFF_POS_SPLIT = 256
DIFF_DENOM_ROWS = 16
MERGE_TM = 256
CROSS_TM = 256
DISPATCH_TM = 256
EXPERT_CHUNK = 768
EXPERT_ROWS = 3 * EXPERT_CHUNK
EXPERT_TF = 256
COMBINE_TM = 256
DMA_ISSUE_UNROLL = 8


def _cparams(semantics):
    return pltpu.CompilerParams(dimension_semantics=semantics, vmem_limit_bytes=VMEM_LIMIT)


def _rms(x, g):
    ms = jnp.mean(x * x, axis=-1, keepdims=True)
    return x * lax.rsqrt(ms + RMS_EPS) * g


def _pack_pair(lo, hi):
    lo_bits = lax.bitcast_convert_type(lo.astype(BF16).astype(F32), U32)
    hi_bits = lax.bitcast_convert_type(hi.astype(BF16).astype(F32), U32)
    return hi_bits | (lo_bits >> 16)


def _unpack_lo(w):
    return lax.bitcast_convert_type(w << 16, F32)


def _unpack_hi(w):
    return lax.bitcast_convert_type(w & jnp.uint32(0xFFFF0000), F32)


def _inproj_kernel(x_ref, g_ref, w_ref, o_ref, xn_ref):
    @pl.when(pl.program_id(1) == 0)
    def _():
        xn_ref[...] = _rms(x_ref[...], g_ref[...]).astype(BF16)

    o_ref[...] = jnp.dot(xn_ref[...], w_ref[...], preferred_element_type=F32).astype(o_ref.dtype)


def _inproj(x2, g, w):
    t, d = x2.shape
    n = w.shape[1]
    tm, tn = min(INPROJ_TM, t), INPROJ_TN
    return pl.pallas_call(
        _inproj_kernel,
        grid=(t // tm, n // tn),
        in_specs=[
            pl.BlockSpec((tm, d), lambda i, j: (i, 0)),
            pl.BlockSpec((1, d), lambda i, j: (0, 0)),
            pl.BlockSpec((d, tn), lambda i, j: (0, j)),
        ],
        out_specs=pl.BlockSpec((tm, tn), lambda i, j: (i, j)),
        out_shape=jax.ShapeDtypeStruct((t, n), BF16),
        scratch_shapes=[pltpu.VMEM((tm, d), BF16)],
        compiler_params=_cparams(("parallel", "arbitrary")),
        name="inproj",
    )(x2, g, w)


def _swa_kernel(qT_ref, kp_ref, kc_ref, vp_ref, vc_ref, bias_ref, sink_ref, o_ref):
    n = pl.program_id(1)
    blk = SWA_BLOCK
    kband = jnp.concatenate([kp_ref[...], kc_ref[...]], axis=0)
    vband = jnp.concatenate([vp_ref[...], vc_ref[...]], axis=1)
    krow = lax.broadcasted_iota(jnp.int32, (2 * blk, SWA_GROUP * blk), 0)
    has_prev = n > 0
    zeros = jnp.zeros((HEAD_DIM, blk), BF16)
    for h in range(SWA_KV_HEADS):
        lane0 = 2 * HEAD_DIM * (h // 2)
        k128 = kband[:, lane0:lane0 + 2 * HEAD_DIM]
        pieces = []
        for g in range(SWA_GROUP):
            hq = SWA_GROUP * h + g
            qh = qT_ref[HEAD_DIM * hq:HEAD_DIM * (hq + 1), :]
            pieces.append(jnp.concatenate([qh, zeros] if h % 2 == 0 else [zeros, qh], axis=0))
        qz = jnp.concatenate(pieces, axis=1)
        s = jnp.dot(k128, qz, preferred_element_type=F32) * (HEAD_DIM ** -0.5)
        s = s + bias_ref[h]
        s = jnp.where((krow >= blk) | has_prev, s, NEG_INF)
        sink = sink_ref[h]
        m = jnp.maximum(jnp.max(s, axis=0, keepdims=True), sink)
        p = jnp.exp(s - m)
        denom = jnp.sum(p, axis=0, keepdims=True) + jnp.exp(sink - m)
        vh = vband[HEAD_DIM * h:HEAD_DIM * (h + 1), :]
        o = jnp.dot(vh, p.astype(BF16), preferred_element_type=F32) / denom
        for g in range(SWA_GROUP):
            hq = SWA_GROUP * h + g
            o_ref[HEAD_DIM * hq:HEAD_DIM * (hq + 1), :] = o[:, blk * g:blk * (g + 1)].astype(o_ref.dtype)


def _swa_tables(sinks):
    blk = SWA_BLOCK
    slopes = (2.0 ** (-8.0 * jnp.arange(1, SWA_Q_HEADS + 1, dtype=F32) / SWA_Q_HEADS)).reshape(SWA_KV_HEADS, SWA_GROUP)
    q_pos = jnp.arange(blk)[None, :] + blk
    k_pos = jnp.arange(2 * blk)[:, None]
    dist = q_pos - k_pos
    allowed = (dist >= 0) & (dist < SWA_BLOCK)
    bias = -slopes[:, :, None, None] * dist.astype(F32)[None, None]
    bias = jnp.where(allowed[None, None], bias, NEG_INF)
    bias = jnp.transpose(bias, (0, 2, 1, 3)).reshape(SWA_KV_HEADS, 2 * blk, SWA_GROUP * blk)
    sink = sinks.astype(F32).reshape(SWA_KV_HEADS, SWA_GROUP, 1)
    sink = jnp.broadcast_to(sink, (SWA_KV_HEADS, SWA_GROUP, blk)).reshape(SWA_KV_HEADS, 1, SWA_GROUP * blk)
    return bias, sink


def _swa(qT, k, vT, sinks):
    b, _, s = qT.shape
    blk = SWA_BLOCK
    bias, sink = _swa_tables(sinks)
    prev = lambda n: jnp.maximum(n - 1, 0)
    return pl.pallas_call(
        _swa_kernel,
        grid=(b, s // blk),
        in_specs=[
            pl.BlockSpec((None, SWA_WIDTH, blk), lambda bi, n: (bi, 0, n)),
            pl.BlockSpec((None, blk, SWA_KV_WIDTH), lambda bi, n: (bi, prev(n), 0)),
            pl.BlockSpec((None, blk, SWA_KV_WIDTH), lambda bi, n: (bi, n, 0)),
            pl.BlockSpec((None, SWA_KV_WIDTH, blk), lambda bi, n: (bi, 0, prev(n))),
            pl.BlockSpec((None, SWA_KV_WIDTH, blk), lambda bi, n: (bi, 0, n)),
            pl.BlockSpec(bias.shape, lambda bi, n: (0, 0, 0)),
            pl.BlockSpec(sink.shape, lambda bi, n: (0, 0, 0)),
        ],
        out_specs=pl.BlockSpec((None, SWA_WIDTH, blk), lambda bi, n: (bi, 0, n)),
        out_shape=jax.ShapeDtypeStruct((b, SWA_WIDTH, s), BF16),
        compiler_params=_cparams(("parallel", "arbitrary")),
        name="swa",
    )(qT, k, k, vT, vT, bias, sink)


def _diff_kernel(c_ref, qT_ref, qx_ref, k_ref, kx_ref, v_ref, ones_ref, lamv_ref, g_ref, o_ref,
                 q2_ref, z_ref, zm_ref, p_ref, a_ref, m_ref, acc_ref, *, t):
    h = pl.program_id(1)
    qi = pl.program_id(2)
    w = DIFF_HEAD_WIDTH
    c = c_ref[h]

    @pl.when(qi == 0)
    def _():
        q2_ref[w:, :] = qx_ref[...]

    qt = qT_ref[...]
    row = lax.broadcasted_iota(jnp.int32, qt.shape, 0)
    zero = jnp.zeros_like(qt)
    q2_ref[:w, :t] = jnp.where(row < HEAD_DIM, qt, zero)
    q2_ref[:w, t:] = jnp.where(row >= HEAD_DIM, qt, zero)
    m_ref[...] = jnp.full(m_ref.shape, NEG_INF, F32)
    acc_ref[...] = jnp.zeros(acc_ref.shape, F32)
    p_ref[1] = jnp.zeros(p_ref.shape[1:], BF16)
    a_ref[1] = jnp.ones(a_ref.shape[1:], F32)

    def scores(k, slot):
        keys = jnp.concatenate([k_ref[k], kx_ref[...]], axis=1)
        z = jnp.dot(keys, q2_ref[...], preferred_element_type=F32)
        z_ref[slot] = z
        zm_ref[slot] = jnp.max(z, axis=0, keepdims=True)

    def values(k, slot):
        vals = jnp.concatenate([v_ref[jnp.maximum(k, 0)], ones_ref[...]], axis=0)
        pv = jnp.dot(vals, p_ref[slot], preferred_element_type=F32)
        acc_ref[...] = a_ref[slot] * acc_ref[...] + pv

    def softmax(k, slot, masked):
        z = z_ref[slot]
        if masked:
            kr = lax.broadcasted_iota(jnp.int32, z.shape, 0)
            qc = lax.broadcasted_iota(jnp.int32, z.shape, 1)
            qc = jnp.where(qc >= t, qc - t, qc)
            z = jnp.where(kr <= qc, z, NEG_INF)
            zmax = jnp.max(z, axis=0, keepdims=True)
        else:
            zmax = zm_ref[slot]
        off = c * ((k - qi) * t).astype(F32)
        m_old = m_ref[...]
        m_new = jnp.maximum(m_old, zmax + off)
        p_ref[slot] = jnp.exp2(z - (m_new - off)).astype(BF16)
        a_ref[slot] = jnp.exp2(m_old - m_new)
        m_ref[...] = m_new

    scores(0, 0)

    def pair(j, carry):
        k = 2 * j
        scores(k + 1, 1)
        values(k - 1, 1)
        softmax(k, 0, False)
        scores(k + 2, 0)
        values(k, 0)
        softmax(k + 1, 1, False)
        return carry

    lax.fori_loop(0, qi >> 1, pair, 0)

    @pl.when((qi & 1) == 0)
    def _():
        values(qi - 1, 1)
        softmax(qi, 0, True)
        values(qi, 0)

    @pl.when((qi & 1) == 1)
    def _():
        scores(qi, 1)
        values(qi - 2, 1)
        softmax(qi - 1, 0, False)
        values(qi - 1, 0)
        softmax(qi, 1, True)
        values(qi, 1)

    lamv = lamv_ref[...]
    lam = (jnp.exp(jnp.sum(lamv[0:1] * lamv[1:2], axis=-1, keepdims=True))
           - jnp.exp(jnp.sum(lamv[2:3] * lamv[3:4], axis=-1, keepdims=True)) + LAMBDA_INIT)
    acc = acc_ref[...]
    l = acc[w:w + 1, :]
    o = acc[:w, :t] / l[:, :t] - lam * (acc[:w, t:] / l[:, t:])
    ms = jnp.mean(o * o, axis=0, keepdims=True)
    y = o * lax.rsqrt(ms + RMS_EPS) * g_ref[...] * (1.0 - LAMBDA_INIT)
    o_ref[...] = y.astype(o_ref.dtype)


def _diff_tables(t):
    w = DIFF_HEAD_WIDTH
    slopes = 2.0 ** (-8.0 * jnp.arange(1, DIFF_HEADS + 1, dtype=F32) / DIFF_HEADS)
    c = slopes * jnp.float32(LOG2E)
    c1 = c.astype(BF16)
    c2 = (c - c1.astype(F32)).astype(BF16)
    c3 = (c - c1.astype(F32) - c2.astype(F32)).astype(BF16)
    parts = jnp.stack([c1, c2, c3, c1, c2, c3], axis=1)
    qx = jnp.zeros((DIFF_HEADS, w, 2 * t), BF16).at[:, :6, :].set(parts[:, :, None])
    r = jnp.arange(t)
    r_lo = (r % DIFF_POS_SPLIT).astype(BF16)
    r_hi = ((r // DIFF_POS_SPLIT) * DIFF_POS_SPLIT).astype(BF16)
    kx = jnp.zeros((t, w), BF16).at[:, 0:3].set(r_lo[:, None]).at[:, 3:6].set(r_hi[:, None])
    return c, qx, kx


def _diff(qT, k5, v5, qx, kx, ones, cs, lamv, subln_g):
    b, _, s = qT.shape
    t = k5.shape[3]
    nk = s // t
    w = DIFF_HEAD_WIDTH
    wv = w + ones.shape[0]
    return pl.pallas_call(
        functools.partial(_diff_kernel, t=t),
        grid=(b, DIFF_HEADS, nk),
        in_specs=[
            pl.BlockSpec(memory_space=pltpu.SMEM),
            pl.BlockSpec((None, w, t), lambda bi, h, qi: (bi, h, qi)),
            pl.BlockSpec((None, w, 2 * t), lambda bi, h, qi: (h, 0, 0)),
            pl.BlockSpec((None, None, nk, t, w), lambda bi, h, qi: (bi, h, 0, 0, 0)),
            pl.BlockSpec((t, w), lambda bi, h, qi: (0, 0)),
            pl.BlockSpec((None, None, nk, w, t), lambda bi, h, qi: (bi, h, 0, 0, 0)),
            pl.BlockSpec(ones.shape, lambda bi, h, qi: (0, 0)),
            pl.BlockSpec((4, HEAD_DIM), lambda bi, h, qi: (0, 0)),
            pl.BlockSpec((w, 1), lambda bi, h, qi: (0, 0)),
        ],
        out_specs=pl.BlockSpec((None, w, t), lambda bi, h, qi: (bi, h, qi)),
        out_shape=jax.ShapeDtypeStruct((b, DIFF_WIDTH, s), BF16),
        scratch_shapes=[
            pltpu.VMEM((2 * w, 2 * t), BF16),
            pltpu.VMEM((2, t, 2 * t), F32),
            pltpu.VMEM((2, 1, 2 * t), F32),
            pltpu.VMEM((2, t, 2 * t), BF16),
            pltpu.VMEM((2, 1, 2 * t), F32),
            pltpu.VMEM((1, 2 * t), F32),
            pltpu.VMEM((wv, 2 * t), F32),
        ],
        compiler_params=_cparams(("arbitrary", "arbitrary", "arbitrary")),
        name="diffattn",
    )(cs, qT, qx, k5, kx, v5, ones, lamv, subln_g)


def _merge_kernel(ya_ref, yd_ref, g0_ref, g1_ref, x_ref, wb0_ref, wb1_ref, wo_ref, o_ref):
    ua = jnp.dot(ya_ref[...], wb0_ref[...], preferred_element_type=F32)
    ud = jnp.dot(yd_ref[...], wb1_ref[...], preferred_element_type=F32)
    merged = jax.nn.sigmoid(g0_ref[...].astype(F32)) * ua + jax.nn.sigmoid(g1_ref[...].astype(F32)) * ud
    o_ref[...] = x_ref[...] + jnp.dot(merged.astype(BF16), wo_ref[...], preferred_element_type=F32)


def _merge(ya, yd, proj, x2, wb, wo):
    t, d = x2.shape
    tm = min(MERGE_TM, t)
    bw = ya.shape[1]
    const = dict(pipeline_mode=pl.Buffered(1))
    return pl.pallas_call(
        _merge_kernel,
        grid=(t // tm,),
        in_specs=[
            pl.BlockSpec((tm, bw), lambda i: (i, 0)),
            pl.BlockSpec((tm, bw), lambda i: (i, 0)),
            pl.BlockSpec((tm, d), lambda i: (i, 0)),
            pl.BlockSpec((tm, d), lambda i: (i, 1)),
            pl.BlockSpec((tm, d), lambda i: (i, 0)),
            pl.BlockSpec((None, bw, d), lambda i: (0, 0, 0), **const),
            pl.BlockSpec((None, bw, d), lambda i: (1, 0, 0), **const),
            pl.BlockSpec((d, d), lambda i: (0, 0), **const),
        ],
        out_specs=pl.BlockSpec((tm, d), lambda i: (i, 0)),
        out_shape=jax.ShapeDtypeStruct((t, d), F32),
        compiler_params=_cparams(("parallel",)),
        name="merge",
    )(ya, yd, proj, proj, x2, wb, wb, wo)


def _memkv_kernel(mem_ref, g_ref, w_ref, k_ref, v_ref):
    mn = _rms(mem_ref[...], g_ref[...]).astype(BF16)
    kv = jnp.dot(mn, w_ref[...], preferred_element_type=F32)
    k_ref[...] = kv[:, :X_WIDTH].astype(k_ref.dtype)
    v_ref[...] = kv[:, X_WIDTH:].astype(v_ref.dtype)


def _memkv(mem, g, w):
    b, m, d = mem.shape
    out = jax.ShapeDtypeStruct((b, m, X_WIDTH), BF16)
    return pl.pallas_call(
        _memkv_kernel,
        grid=(b,),
        in_specs=[
            pl.BlockSpec((None, m, d), lambda i: (i, 0, 0)),
            pl.BlockSpec((1, d), lambda i: (0, 0)),
            pl.BlockSpec((d, 2 * X_WIDTH), lambda i: (0, 0)),
        ],
        out_specs=[pl.BlockSpec((None, m, X_WIDTH), lambda i: (i, 0, 0))] * 2,
        out_shape=[out, out],
        compiler_params=_cparams(("parallel",)),
        name="memkv",
    )(mem, g, w)


def _cross_router_kernel(h_ref, k_ref, v_ref, gc_ref, wq_ref, wo_ref, gf_ref, wr_ref, br_ref,
                         h2_ref, xn_ref, idx_ref, gate_ref, rank_ref, cnt_ref, run_ref):
    i = pl.program_id(0)
    tm, d = h_ref.shape

    @pl.when(i == 0)
    def _():
        run_ref[...] = jnp.zeros(run_ref.shape, F32)

    h = h_ref[...]
    hn = _rms(h, gc_ref[...]).astype(BF16)
    q = jnp.dot(hn, wq_ref[...], preferred_element_type=F32).astype(BF16)
    k = k_ref[...]
    v = v_ref[...]
    outs = []
    for hd in range(X_HEADS):
        sl = slice(X_HEAD_DIM * hd, X_HEAD_DIM * (hd + 1))
        s = lax.dot_general(q[:, sl], k[:, sl], (((1,), (1,)), ((), ())),
                            preferred_element_type=F32) * (X_HEAD_DIM ** -0.5)
        s = s - jnp.max(s, axis=-1, keepdims=True)
        p = jnp.exp(s)
        p = p / jnp.sum(p, axis=-1, keepdims=True)
        outs.append(jnp.dot(p.astype(BF16), v[:, sl], preferred_element_type=F32))
    o = jnp.concatenate(outs, axis=1).astype(BF16)
    h2 = h + jnp.dot(o, wo_ref[...], preferred_element_type=F32)
    h2_ref[...] = h2

    xn = _rms(h2, gf_ref[...])
    packed = _pack_pair(xn[:, :d // 2], xn[:, d // 2:])
    for j in range(V7X_SUBLANES):
        xn_ref[pl.ds(j, tm, stride=V7X_SUBLANES), :] = packed[:, j * V7X_LANES:(j + 1) * V7X_LANES]
    logits = jnp.dot(xn.astype(BF16), wr_ref[...], preferred_element_type=F32) + br_ref[...]

    lane = lax.broadcasted_iota(jnp.int32, logits.shape, 1).astype(F32)
    work = logits
    vals, idxs = [], []
    onehot = jnp.zeros(logits.shape, F32)
    for _ in range(TOP_K):
        mx = jnp.max(work, axis=-1, keepdims=True)
        ix = jnp.min(jnp.where(work == mx, lane, float(N_EXPERTS)), axis=-1, keepdims=True)
        sel = lane == ix
        vals.append(mx)
        idxs.append(ix)
        onehot = jnp.where(sel, 1.0, onehot)
        work = jnp.where(sel, NEG_INF, work)
    es = [jnp.exp(vv - vals[0]) for vv in vals]
    tot = es[0] + es[1] + es[2] + es[3]

    r = lax.broadcasted_iota(jnp.int32, (tm, tm), 0)
    c = lax.broadcasted_iota(jnp.int32, (tm, tm), 1)
    lower = jnp.where(c < r, 1.0, 0.0).astype(BF16)
    prefix = jnp.dot(lower, onehot.astype(BF16), preferred_element_type=F32) + run_ref[...]
    for kk in range(TOP_K):
        idx_ref[:, kk:kk + 1] = idxs[kk].astype(jnp.int32)
        gate_ref[:, kk:kk + 1] = es[kk] / tot
        rk = jnp.sum(jnp.where(lane == idxs[kk], prefix, 0.0), axis=-1, keepdims=True)
        rank_ref[:, kk:kk + 1] = rk.astype(jnp.int32)
    run_ref[...] = run_ref[...] + jnp.sum(onehot, axis=0, keepdims=True)
    cnt_ref[...] = run_ref[...].astype(jnp.int32)


def _cross_router(h1, kmem, vmem, gc, wq, wo, gf, wr, br, seq):
    t, d = h1.shape
    tm = min(CROSS_TM, seq)
    per_b = seq // tm
    m = kmem.shape[1]
    const = dict(pipeline_mode=pl.Buffered(1))
    outs = pl.pallas_call(
        _cross_router_kernel,
        grid=(t // tm,),
        in_specs=[
            pl.BlockSpec((tm, d), lambda i: (i, 0)),
            pl.BlockSpec((None, m, X_WIDTH), lambda i: (i // per_b, 0, 0)),
            pl.BlockSpec((None, m, X_WIDTH), lambda i: (i // per_b, 0, 0)),
            pl.BlockSpec((1, d), lambda i: (0, 0)),
            pl.BlockSpec((d, X_WIDTH), lambda i: (0, 0), **const),
            pl.BlockSpec((X_WIDTH, d), lambda i: (0, 0), **const),
            pl.BlockSpec((1, d), lambda i: (0, 0)),
            pl.BlockSpec((d, N_EXPERTS), lambda i: (0, 0), **const),
            pl.BlockSpec((1, N_EXPERTS), lambda i: (0, 0)),
        ],
        out_specs=[
            pl.BlockSpec((tm, d), lambda i: (i, 0)),
            pl.BlockSpec((tm * V7X_SUBLANES, V7X_LANES), lambda i: (i, 0)),
            pl.BlockSpec((tm, TOP_K), lambda i: (i, 0)),
            pl.BlockSpec((tm, TOP_K), lambda i: (i, 0)),
            pl.BlockSpec((tm, TOP_K), lambda i: (i, 0)),
            pl.BlockSpec((1, N_EXPERTS), lambda i: (0, 0)),
        ],
        out_shape=[
            jax.ShapeDtypeStruct((t, d), F32),
            jax.ShapeDtypeStruct((t * V7X_SUBLANES, V7X_LANES), U32),
            jax.ShapeDtypeStruct((t, TOP_K), jnp.int32),
            jax.ShapeDtypeStruct((t, TOP_K), F32),
            jax.ShapeDtypeStruct((t, TOP_K), jnp.int32),
            jax.ShapeDtypeStruct((1, N_EXPERTS), jnp.int32),
        ],
        scratch_shapes=[pltpu.VMEM((1, N_EXPERTS), F32)],
        compiler_params=_cparams(("arbitrary",)),
        name="cross_router",
    )(h1, kmem, vmem, gc, wq, wo, gf, wr, br)
    return outs


def _tile_rows(row):
    return pl.ds(pl.multiple_of(row * V7X_SUBLANES, V7X_SUBLANES), V7X_SUBLANES)


def _dispatch_kernel(pos_ref, tail_ref, x_ref, xs_ref, zero_ref, sem, zsem):
    i = pl.program_id(0)
    tm = x_ref.shape[0] // V7X_SUBLANES
    chunk_rows = zero_ref.shape[0]

    @pl.when(i == 0)
    def _():
        zero_ref[...] = jnp.zeros(zero_ref.shape, zero_ref.dtype)

        def tail_copy(e):
            row0 = pl.multiple_of(tail_ref[e] * V7X_SUBLANES, V7X_SUBLANES)
            return pltpu.make_async_copy(zero_ref, xs_ref.at[pl.ds(row0, chunk_rows)], zsem)

        for e in range(N_EXPERTS):
            @pl.when(tail_ref[e] >= 0)
            def _():
                tail_copy(e).start()
        for e in range(N_EXPERTS):
            @pl.when(tail_ref[e] >= 0)
            def _():
                tail_copy(e).wait()

    def row_copy(r, p):
        return pltpu.make_async_copy(x_ref.at[_tile_rows(r)], xs_ref.at[_tile_rows(p)], sem)

    def issue(r, carry):
        for kk in range(TOP_K):
            row_copy(r, pos_ref[0, 0, r * TOP_K + kk]).start(priority=kk % 2)
        return carry

    lax.fori_loop(0, tm, issue, 0, unroll=DMA_ISSUE_UNROLL)
    for _ in range(tm * TOP_K):
        row_copy(0, 0).wait()


def _dispatch(xn, pos, tails, n_rows):
    t = xn.shape[0] // V7X_SUBLANES
    tm = min(DISPATCH_TM, t)
    pos3 = pos.reshape(t // tm, 1, tm * TOP_K)
    return pl.pallas_call(
        _dispatch_kernel,
        grid=(t // tm,),
        in_specs=[
            pl.BlockSpec((1, 1, tm * TOP_K), lambda i: (i, 0, 0), memory_space=pltpu.SMEM),
            pl.BlockSpec(memory_space=pltpu.SMEM),
            pl.BlockSpec((tm * V7X_SUBLANES, V7X_LANES), lambda i: (i, 0)),
        ],
        out_specs=pl.BlockSpec(memory_space=pl.ANY),
        out_shape=jax.ShapeDtypeStruct((n_rows * V7X_SUBLANES, V7X_LANES), xn.dtype),
        scratch_shapes=[pltpu.VMEM((EXPERT_CHUNK * V7X_SUBLANES, V7X_LANES), xn.dtype),
                        pltpu.SemaphoreType.DMA(()), pltpu.SemaphoreType.DMA(())],
        compiler_params=_cparams(("arbitrary",)),
        name="dispatch",
    )(pos3, tails, xn)


def _expert_kernel(ve_ref, vr_ref, vb_ref, x_ref, wg_ref, wl_ref, bg_ref, bl_ref, wd_ref, bd_ref, o_ref,
                   act_ref, wa_ref, wb_ref):
    del ve_ref, vb_ref
    v = pl.program_id(0)
    s = pl.program_id(1)
    rows = vr_ref[v]
    nj = act_ref.shape[0]
    tf = act_ref.shape[2]
    ch = EXPERT_CHUNK
    n_chunks = x_ref.shape[0] // (ch * V7X_SUBLANES)
    half = o_ref.shape[1]

    @pl.when((rows > 0) & (s < nj))
    def _():
        wa_ref[...] = wg_ref[...].astype(BF16)
        wb_ref[...] = wl_ref[...].astype(BF16)
        for c in range(n_chunks):
            @pl.when(c * ch < rows)
            def _():
                xw = [x_ref[pl.ds(c * ch * V7X_SUBLANES + j, ch, stride=V7X_SUBLANES), :]
                      for j in range(V7X_SUBLANES)]
                xb = jnp.concatenate([_unpack_lo(u).astype(BF16) for u in xw]
                                     + [_unpack_hi(u).astype(BF16) for u in xw], axis=1)
                glu = jnp.dot(xb, wa_ref[...], preferred_element_type=F32) + bg_ref[...]
                lin = jnp.dot(xb, wb_ref[...], preferred_element_type=F32) + bl_ref[...]
                glu = jnp.minimum(glu, SWIGLU_LIMIT)
                lin = jnp.clip(lin, -SWIGLU_LIMIT, SWIGLU_LIMIT)
                act = glu * jax.nn.sigmoid(SWIGLU_ALPHA * glu) * (lin + 1.0)
                act_ref[s, pl.ds(c * ch, ch), :] = act.astype(BF16)

    @pl.when((rows > 0) & (s >= nj))
    def _():
        wa_ref[...] = wd_ref[...].astype(BF16)
        for c in range(n_chunks):
            sl = pl.ds(c * ch, ch)

            @pl.when(c * ch < rows)
            def _():
                acc = jnp.dot(act_ref[0, sl, :], wa_ref[0:tf, :], preferred_element_type=F32)
                for j in range(1, nj):
                    acc = acc + jnp.dot(act_ref[j, sl, :], wa_ref[j * tf:(j + 1) * tf, :],
                                        preferred_element_type=F32)
                acc = acc + bd_ref[...]
                o_ref[sl, :] = _pack_pair(acc[:, :half], acc[:, half:])

            @pl.when(c * ch >= rows)
            def _():
                o_ref[sl, :] = jnp.zeros((ch, half), o_ref.dtype)


def _experts(xs, visit_expert, visit_rows, visit_blk, n_used, w_gu, b_gu, w_down, b_down):
    n_rows = xs.shape[0] // V7X_SUBLANES
    e, d, ff2 = w_gu.shape
    ff = ff2 // 2
    r, tf = EXPERT_ROWS, EXPERT_TF
    nj = ff // tf
    nn = d // tf
    b_gu3 = b_gu.reshape(e, 1, ff2)
    b_down3 = b_down.reshape(e, 1, d)

    def up(v, s, vr):
        return jnp.where(vr[v] > 0, jnp.minimum(s, nj - 1), nj - 1)

    def down(v, s, vr):
        return jnp.where(vr[v] > 0, jnp.maximum(s - nj, 0), nn - 1)

    grid_spec = pltpu.PrefetchScalarGridSpec(
        num_scalar_prefetch=3,
        grid=(n_used, nj + nn),
        in_specs=[
            pl.BlockSpec((r * V7X_SUBLANES, V7X_LANES), lambda v, s, ve, vr, vb: (vb[v], 0)),
            pl.BlockSpec((None, d, tf), lambda v, s, ve, vr, vb: (ve[v], 0, up(v, s, vr))),
            pl.BlockSpec((None, d, tf), lambda v, s, ve, vr, vb: (ve[v], 0, nj + up(v, s, vr))),
            pl.BlockSpec((None, 1, tf), lambda v, s, ve, vr, vb: (ve[v], 0, up(v, s, vr))),
            pl.BlockSpec((None, 1, tf), lambda v, s, ve, vr, vb: (ve[v], 0, nj + up(v, s, vr))),
            pl.BlockSpec((None, ff, tf), lambda v, s, ve, vr, vb: (ve[v], 0, down(v, s, vr))),
            pl.BlockSpec((None, 1, tf), lambda v, s, ve, vr, vb: (ve[v], 0, down(v, s, vr))),
        ],
        out_specs=pl.BlockSpec((r, tf // 2), lambda v, s, ve, vr, vb: (vb[v], down(v, s, vr))),
        scratch_shapes=[
            pltpu.VMEM((nj, r, tf), BF16),
            pltpu.VMEM((d, tf), BF16),
            pltpu.VMEM((d, tf), BF16),
        ],
    )
    return pl.pallas_call(
        _expert_kernel,
        grid_spec=grid_spec,
        out_shape=jax.ShapeDtypeStruct((n_rows, d // 2), U32),
        compiler_params=_cparams(("arbitrary", "arbitrary")),
        name="experts",
    )(visit_expert, visit_rows, visit_blk, xs, w_gu, w_gu, b_gu3, b_gu3, w_down, b_down3)


def _combine_kernel(pos_ref, h_ref, gate_ref, g_ref, ys_ref, o_ref, buf_ref, sem):
    tm = h_ref.shape[0]
    half = EXPERT_TF // 2

    def row_copy(r, kk, p):
        return pltpu.make_async_copy(ys_ref.at[pl.ds(p, 1)], buf_ref.at[kk, pl.ds(r, 1)], sem)

    def issue(r, carry):
        for kk in range(TOP_K):
            row_copy(r, kk, pos_ref[0, 0, r * TOP_K + kk]).start(priority=kk % 2)
        return carry

    lax.fori_loop(0, tm, issue, 0, unroll=DMA_ISSUE_UNROLL)
    for _ in range(tm * TOP_K):
        row_copy(0, 0, 0).wait()

    gates = gate_ref[...]
    y = h_ref[...]
    for kk in range(TOP_K):
        w = buf_ref[kk]
        lo = _unpack_lo(w)
        hi = _unpack_hi(w)
        pieces = []
        for n in range(w.shape[1] // half):
            pieces += [lo[:, n * half:(n + 1) * half], hi[:, n * half:(n + 1) * half]]
        y = y + gates[:, kk:kk + 1] * jnp.concatenate(pieces, axis=1)
    o_ref[...] = _rms(y, g_ref[...])


def _combine(h2, gates, pos, ys, g_final):
    t, d = h2.shape
    tm = min(COMBINE_TM, t)
    pos3 = pos.reshape(t // tm, 1, tm * TOP_K)
    return pl.pallas_call(
        _combine_kernel,
        grid=(t // tm,),
        in_specs=[
            pl.BlockSpec((1, 1, tm * TOP_K), lambda i: (i, 0, 0), memory_space=pltpu.SMEM),
            pl.BlockSpec((tm, d), lambda i: (i, 0)),
            pl.BlockSpec((tm, TOP_K), lambda i: (i, 0)),
            pl.BlockSpec((1, d), lambda i: (0, 0)),
            pl.BlockSpec(memory_space=pl.ANY),
        ],
        out_specs=pl.BlockSpec((tm, d), lambda i: (i, 0)),
        out_shape=jax.ShapeDtypeStruct((t, d), F32),
        scratch_shapes=[pltpu.VMEM((TOP_K, tm, ys.shape[1]), ys.dtype), pltpu.SemaphoreType.DMA(())],
        compiler_params=_cparams(("arbitrary",)),
        name="combine",
    )(pos3, h2, gates, g_final, ys)


def _routing_tables(counts, top_idx, rank, n_visits):
    r, ch = EXPERT_ROWS, EXPERT_CHUNK
    counts = counts.reshape(N_EXPERTS)
    padded = (counts + r - 1) // r * r
    ends = jnp.cumsum(padded)
    starts = ends - padded
    experts = jnp.arange(N_EXPERTS, dtype=jnp.int32)
    start_of = jnp.sum(jnp.where(top_idx[..., None] == experts, starts, 0), axis=-1)
    pos = start_of + rank
    visit_row0 = jnp.arange(n_visits, dtype=jnp.int32) * r
    owner = jnp.sum((ends[None, :] <= visit_row0[:, None]).astype(jnp.int32), axis=1)
    visit_expert = jnp.minimum(owner, N_EXPERTS - 1).astype(jnp.int32)
    visit_rows = jnp.clip(counts[visit_expert] - (visit_row0 - starts[visit_expert]), 0, r).astype(jnp.int32)
    n_used = (ends[-1] // r).astype(jnp.int32)
    last = jnp.maximum(n_used - 1, 0)
    ids = jnp.arange(n_visits, dtype=jnp.int32)
    active = ids < n_used
    visit_rows = jnp.where(active, visit_rows, 0)
    visit_blk = jnp.where(active, ids, last)
    visit_expert = jnp.where(active, visit_expert, visit_expert[last])
    tail = starts + jnp.minimum(counts // ch * ch, padded - ch)
    tail = jnp.where(counts > 0, tail, -1)
    return pos.astype(jnp.int32), visit_expert, visit_rows, visit_blk, tail.astype(jnp.int32), n_used


def kernel(x, mem, norm_mix_g, w_in, sinks, lambda_q1, lambda_k1, lambda_q2, lambda_k2, diff_subln_g, w_branch, w_o, norm_cross_g, norm_mem_g, w_cq, w_ckv, w_co, norm_ffn_g, w_router, b_router, w_gate_up, b_gate_up, w_down, b_down, norm_final_g):
    b, s, d = x.shape
    t = b * s
    x2 = x.reshape(t, d)
    l = 0

    w = w_in[l]
    o_qa, o_ka, o_va = 0, SWA_WIDTH, SWA_WIDTH + SWA_KV_WIDTH
    o_qd = o_va + SWA_KV_WIDTH
    o_kd, o_vd, o_g = o_qd + DIFF_WIDTH, o_qd + 2 * DIFF_WIDTH, o_qd + 3 * DIFF_WIDTH
    w_perm = jnp.concatenate([
        w[:, o_g:], w[:, o_qd:o_kd] * jnp.float32(HEAD_DIM ** -0.5 * LOG2E), w[:, o_kd:o_vd], w[:, o_vd:o_g],
        w[:, o_qa:o_ka], w[:, o_ka:o_va], w[:, o_va:o_qd]], axis=1).astype(BF16)
    proj = _inproj(x2, norm_mix_g[l].reshape(1, d), w_perm)
    c_qd = 2 * d
    c_kd, c_vd, c_qa = c_qd + DIFF_WIDTH, c_qd + 2 * DIFF_WIDTH, c_qd + 3 * DIFF_WIDTH
    c_ka, c_va = c_qa + SWA_WIDTH, c_qa + SWA_WIDTH + SWA_KV_WIDTH

    def seg(c0, width):
        return proj[:, c0:c0 + width].reshape(b, s, width)

    qaT = jnp.swapaxes(seg(c_qa, SWA_WIDTH), 1, 2)
    vaT = jnp.swapaxes(seg(c_va, SWA_KV_WIDTH), 1, 2)
    yaT = _swa(qaT, seg(c_ka, SWA_KV_WIDTH), vaT, sinks[l])
    ya = jnp.swapaxes(yaT, 1, 2).reshape(t, SWA_WIDTH)

    td = min(DIFF_T, s)
    nk = s // td
    hw = DIFF_HEAD_WIDTH
    cs, qx, kx = _diff_tables(td)
    qdT = jnp.swapaxes(seg(c_qd, DIFF_WIDTH), 1, 2)
    k5 = seg(c_kd, DIFF_WIDTH).reshape(b, nk, td, DIFF_HEADS, hw).transpose(0, 3, 1, 2, 4)
    v5 = seg(c_vd, DIFF_WIDTH).reshape(b, nk, td, DIFF_HEADS, hw).transpose(0, 3, 1, 4, 2)
    ones = jnp.zeros((DIFF_DENOM_ROWS, td), BF16).at[0].set(1.0)
    lamv = jnp.stack([lambda_q1[l], lambda_k1[l], lambda_q2[l], lambda_k2[l]]).astype(F32)
    ydT = _diff(qdT, k5, v5, qx, kx, ones, cs, lamv, diff_subln_g[l].astype(F32).reshape(hw, 1))
    yd = jnp.swapaxes(ydT, 1, 2).reshape(t, DIFF_WIDTH)

    h1 = _merge(ya, yd, proj, x2, w_branch[l].astype(BF16), w_o[l].astype(BF16))

    kmem, vmem = _memkv(mem, norm_mem_g[l].reshape(1, d), w_ckv[l].astype(BF16))
    h2, xn, top_idx, gates, rank, counts = _cross_router(
        h1, kmem, vmem, norm_cross_g[l].reshape(1, d), w_cq[l].astype(BF16), w_co[l].astype(BF16),
        norm_ffn_g[l].reshape(1, d), w_router[l].astype(BF16), b_router[l].reshape(1, N_EXPERTS), s)

    assert d // 2 == V7X_SUBLANES * V7X_LANES and EXPERT_TF // 2 == V7X_LANES
    n_visits = (t * TOP_K) // EXPERT_ROWS + N_EXPERTS
    n_rows = n_visits * EXPERT_ROWS
    pos, visit_expert, visit_rows, visit_blk, tails, n_used = _routing_tables(counts, top_idx, rank, n_visits)
    xs = _dispatch(xn, pos, tails, n_rows)
    ys = _experts(xs, visit_expert, visit_rows, visit_blk, n_used, w_gate_up[l], b_gate_up[l], w_down[l], b_down[l])
    out = _combine(h2, gates, pos, ys, norm_final_g.reshape(1, d))
    return out.reshape(b, s, d)
```

```python
import functools
import math

import jax
import jax.numpy as jnp
from jax import lax
from jax.experimental import pallas as pl
from jax.experimental.pallas import tpu as pltpu

BF16 = jnp.bfloat16
F32 = jnp.float32
U32 = jnp.uint32
NEG_INF = float("-inf")
LOG2E = math.log2(math.e)

HEAD_DIM = 64
SWA_Q_HEADS = 16
SWA_KV_HEADS = 4
SWA_GROUP = SWA_Q_HEADS // SWA_KV_HEADS
SWA_BLOCK = 128
SWA_WIDTH = SWA_Q_HEADS * HEAD_DIM
SWA_KV_WIDTH = SWA_KV_HEADS * HEAD_DIM
DIFF_HEADS = 8
DIFF_HEAD_WIDTH = 2 * HEAD_DIM
DIFF_WIDTH = DIFF_HEADS * DIFF_HEAD_WIDTH
X_HEADS = 4
X_HEAD_DIM = 128
X_WIDTH = X_HEADS * X_HEAD_DIM
N_EXPERTS = 32
TOP_K = 4
SWIGLU_LIMIT = 7.0
SWIGLU_ALPHA = 1.702
RMS_EPS = 1e-5
LAMBDA_INIT = 0.8 - 0.6 * math.exp(-0.3 * 0)

V7X_VMEM_BYTES = 64 * 1024 * 1024
VMEM_LIMIT = V7X_VMEM_BYTES - 8 * 1024 * 1024
V7X_LANES = 128
V7X_SUBLANES = 8

INPROJ_TM = 512
INPROJ_TN = 2176
DIFF_T = 512
DIFF_POS_SPLIT = 256
DIFF_DENOM_ROWS = 16
MERGE_TM = 256
CROSS_TM = 256
DISPATCH_TM = 256
EXPERT_CHUNK = 768
EXPERT_ROWS = 3 * EXPERT_CHUNK
EXPERT_TF = 512
COMBINE_TM = 256
DMA_ISSUE_UNROLL = 8


def _cparams(semantics):
    return pltpu.CompilerParams(dimension_semantics=semantics, vmem_limit_bytes=VMEM_LIMIT)


def _rms(x, g):
    ms = jnp.mean(x * x, axis=-1, keepdims=True)
    return x * lax.rsqrt(ms + RMS_EPS) * g


def _pack_pair(lo, hi):
    lo_bits = lax.bitcast_convert_type(lo.astype(BF16).astype(F32), U32)
    hi_bits = lax.bitcast_convert_type(hi.astype(BF16).astype(F32), U32)
    return hi_bits | (lo_bits >> 16)


def _unpack_lo(w):
    return lax.bitcast_convert_type(w << 16, F32)


def _unpack_hi(w):
    return lax.bitcast_convert_type(w & jnp.uint32(0xFFFF0000), F32)


def _inproj_kernel(x_ref, g_ref, w_ref, o_ref, xn_ref):
    @pl.when(pl.program_id(1) == 0)
    def _():
        xn_ref[...] = _rms(x_ref[...], g_ref[...]).astype(BF16)

    o_ref[...] = jnp.dot(xn_ref[...], w_ref[...], preferred_element_type=F32).astype(o_ref.dtype)


def _inproj(x2, g, w):
    t, d = x2.shape
    n = w.shape[1]
    tm, tn = min(INPROJ_TM, t), INPROJ_TN
    return pl.pallas_call(
        _inproj_kernel,
        grid=(t // tm, n // tn),
        in_specs=[
            pl.BlockSpec((tm, d), lambda i, j: (i, 0)),
            pl.BlockSpec((1, d), lambda i, j: (0, 0)),
            pl.BlockSpec((d, tn), lambda i, j: (0, j)),
        ],
        out_specs=pl.BlockSpec((tm, tn), lambda i, j: (i, j)),
        out_shape=jax.ShapeDtypeStruct((t, n), BF16),
        scratch_shapes=[pltpu.VMEM((tm, d), BF16)],
        compiler_params=_cparams(("parallel", "arbitrary")),
        name="inproj",
    )(x2, g, w)


def _swa_kernel(qT_ref, kp_ref, kc_ref, vp_ref, vc_ref, bias_ref, sink_ref, o_ref):
    n = pl.program_id(1)
    blk = SWA_BLOCK
    kband = jnp.concatenate([kp_ref[...], kc_ref[...]], axis=0)
    vband = jnp.concatenate([vp_ref[...], vc_ref[...]], axis=1)
    krow = lax.broadcasted_iota(jnp.int32, (2 * blk, SWA_GROUP * blk), 0)
    has_prev = n > 0
    zeros = jnp.zeros((HEAD_DIM, blk), BF16)
    for h in range(SWA_KV_HEADS):
        lane0 = 2 * HEAD_DIM * (h // 2)
        k128 = kband[:, lane0:lane0 + 2 * HEAD_DIM]
        pieces = []
        for g in range(SWA_GROUP):
            hq = SWA_GROUP * h + g
            qh = qT_ref[HEAD_DIM * hq:HEAD_DIM * (hq + 1), :]
            pieces.append(jnp.concatenate([qh, zeros] if h % 2 == 0 else [zeros, qh], axis=0))
        qz = jnp.concatenate(pieces, axis=1)
        s = jnp.dot(k128, qz, preferred_element_type=F32) * (HEAD_DIM ** -0.5)
        s = s + bias_ref[h]
        s = jnp.where((krow >= blk) | has_prev, s, NEG_INF)
        sink = sink_ref[h]
        m = jnp.maximum(jnp.max(s, axis=0, keepdims=True), sink)
        p = jnp.exp(s - m)
        denom = jnp.sum(p, axis=0, keepdims=True) + jnp.exp(sink - m)
        vh = vband[HEAD_DIM * h:HEAD_DIM * (h + 1), :]
        o = jnp.dot(vh, p.astype(BF16), preferred_element_type=F32) / denom
        for g in range(SWA_GROUP):
            hq = SWA_GROUP * h + g
            o_ref[HEAD_DIM * hq:HEAD_DIM * (hq + 1), :] = o[:, blk * g:blk * (g + 1)].astype(o_ref.dtype)


def _swa_tables(sinks):
    blk = SWA_BLOCK
    slopes = (2.0 ** (-8.0 * jnp.arange(1, SWA_Q_HEADS + 1, dtype=F32) / SWA_Q_HEADS)).reshape(SWA_KV_HEADS, SWA_GROUP)
    q_pos = jnp.arange(blk)[None, :] + blk
    k_pos = jnp.arange(2 * blk)[:, None]
    dist = q_pos - k_pos
    allowed = (dist >= 0) & (dist < SWA_BLOCK)
    bias = -slopes[:, :, None, None] * dist.astype(F32)[None, None]
    bias = jnp.where(allowed[None, None], bias, NEG_INF)
    bias = jnp.transpose(bias, (0, 2, 1, 3)).reshape(SWA_KV_HEADS, 2 * blk, SWA_GROUP * blk)
    sink = sinks.astype(F32).reshape(SWA_KV_HEADS, SWA_GROUP, 1)
    sink = jnp.broadcast_to(sink, (SWA_KV_HEADS, SWA_GROUP, blk)).reshape(SWA_KV_HEADS, 1, SWA_GROUP * blk)
    return bias, sink


def _swa(qT, k, vT, sinks):
    b, _, s = qT.shape
    blk = SWA_BLOCK
    bias, sink = _swa_tables(sinks)
    prev = lambda n: jnp.maximum(n - 1, 0)
    return pl.pallas_call(
        _swa_kernel,
        grid=(b, s // blk),
        in_specs=[
            pl.BlockSpec((None, SWA_WIDTH, blk), lambda bi, n: (bi, 0, n)),
            pl.BlockSpec((None, blk, SWA_KV_WIDTH), lambda bi, n: (bi, prev(n), 0)),
            pl.BlockSpec((None, blk, SWA_KV_WIDTH), lambda bi, n: (bi, n, 0)),
            pl.BlockSpec((None, SWA_KV_WIDTH, blk), lambda bi, n: (bi, 0, prev(n))),
            pl.BlockSpec((None, SWA_KV_WIDTH, blk), lambda bi, n: (bi, 0, n)),
            pl.BlockSpec(bias.shape, lambda bi, n: (0, 0, 0)),
            pl.BlockSpec(sink.shape, lambda bi, n: (0, 0, 0)),
        ],
        out_specs=pl.BlockSpec((None, SWA_WIDTH, blk), lambda bi, n: (bi, 0, n)),
        out_shape=jax.ShapeDtypeStruct((b, SWA_WIDTH, s), BF16),
        compiler_params=_cparams(("parallel", "arbitrary")),
        name="swa",
    )(qT, k, k, vT, vT, bias, sink)


def _diff_kernel(c_ref, qT_ref, qx_ref, k_ref, kx_ref, v_ref, ones_ref, lamv_ref, g_ref, o_ref,
                 q2_ref, z_ref, zm_ref, p_ref, a_ref, m_ref, acc_ref, *, t):
    h = pl.program_id(1)
    qi = pl.program_id(2)
    w = DIFF_HEAD_WIDTH
    c = c_ref[h]

    @pl.when(qi == 0)
    def _():
        q2_ref[w:, :] = qx_ref[...]

    qt = qT_ref[...]
    row = lax.broadcasted_iota(jnp.int32, qt.shape, 0)
    zero = jnp.zeros_like(qt)
    q2_ref[:w, :t] = jnp.where(row < HEAD_DIM, qt, zero)
    q2_ref[:w, t:] = jnp.where(row >= HEAD_DIM, qt, zero)
    m_ref[...] = jnp.full(m_ref.shape, NEG_INF, F32)
    acc_ref[...] = jnp.zeros(acc_ref.shape, F32)
    p_ref[1] = jnp.zeros(p_ref.shape[1:], BF16)
    a_ref[1] = jnp.ones(a_ref.shape[1:], F32)

    def scores(k, slot):
        keys = jnp.concatenate([k_ref[k], kx_ref[...]], axis=1)
        z = jnp.dot(keys, q2_ref[...], preferred_element_type=F32)
        z_ref[slot] = z
        zm_ref[slot] = jnp.max(z, axis=0, keepdims=True)

    def values(k, slot):
        vals = jnp.concatenate([v_ref[jnp.maximum(k, 0)], ones_ref[...]], axis=0)
        pv = jnp.dot(vals, p_ref[slot], preferred_element_type=F32)
        acc_ref[...] = a_ref[slot] * acc_ref[...] + pv

    def softmax(k, slot, masked):
        z = z_ref[slot]
        if masked:
            kr = lax.broadcasted_iota(jnp.int32, z.shape, 0)
            qc = lax.broadcasted_iota(jnp.int32, z.shape, 1)
            qc = jnp.where(qc >= t, qc - t, qc)
            z = jnp.where(kr <= qc, z, NEG_INF)
            zmax = jnp.max(z, axis=0, keepdims=True)
        else:
            zmax = zm_ref[slot]
        off = c * ((k - qi) * t).astype(F32)
        m_old = m_ref[...]
        m_new = jnp.maximum(m_old, zmax + off)
        p_ref[slot] = jnp.exp2(z - (m_new - off)).astype(BF16)
        a_ref[slot] = jnp.exp2(m_old - m_new)
        m_ref[...] = m_new

    scores(0, 0)

    def pair(j, carry):
        k = 2 * j
        scores(k + 1, 1)
        values(k - 1, 1)
        softmax(k, 0, False)
        scores(k + 2, 0)
        values(k, 0)
        softmax(k + 1, 1, False)
        return carry

    lax.fori_loop(0, qi >> 1, pair, 0)

    @pl.when((qi & 1) == 0)
    def _():
        values(qi - 1, 1)
        softmax(qi, 0, True)
        values(qi, 0)

    @pl.when((qi & 1) == 1)
    def _():
        scores(qi, 1)
        values(qi - 2, 1)
        softmax(qi - 1, 0, False)
        values(qi - 1, 0)
        softmax(qi, 1, True)
        values(qi, 1)

    lamv = lamv_ref[...]
    lam = (jnp.exp(jnp.sum(lamv[0:1] * lamv[1:2], axis=-1, keepdims=True))
           - jnp.exp(jnp.sum(lamv[2:3] * lamv[3:4], axis=-1, keepdims=True)) + LAMBDA_INIT)
    acc = acc_ref[...]
    l = acc[w:w + 1, :]
    o = acc[:w, :t] / l[:, :t] - lam * (acc[:w, t:] / l[:, t:])
    ms = jnp.mean(o * o, axis=0, keepdims=True)
    y = o * lax.rsqrt(ms + RMS_EPS) * g_ref[...] * (1.0 - LAMBDA_INIT)
    o_ref[...] = y.astype(o_ref.dtype)


def _diff_tables(t):
    w = DIFF_HEAD_WIDTH
    slopes = 2.0 ** (-8.0 * jnp.arange(1, DIFF_HEADS + 1, dtype=F32) / DIFF_HEADS)
    c = slopes * jnp.float32(LOG2E)
    c1 = c.astype(BF16)
    c2 = (c - c1.astype(F32)).astype(BF16)
    c3 = (c - c1.astype(F32) - c2.astype(F32)).astype(BF16)
    parts = jnp.stack([c1, c2, c3, c1, c2, c3], axis=1)
    qx = jnp.zeros((DIFF_HEADS, w, 2 * t), BF16).at[:, :6, :].set(parts[:, :, None])
    r = jnp.arange(t)
    r_lo = (r % DIFF_POS_SPLIT).astype(BF16)
    r_hi = ((r // DIFF_POS_SPLIT) * DIFF_POS_SPLIT).astype(BF16)
    kx = jnp.zeros((t, w), BF16).at[:, 0:3].set(r_lo[:, None]).at[:, 3:6].set(r_hi[:, None])
    return c, qx, kx


def _diff(qT, k5, v5, qx, kx, ones, cs, lamv, subln_g):
    b, _, s = qT.shape
    t = k5.shape[3]
    nk = s // t
    w = DIFF_HEAD_WIDTH
    wv = w + ones.shape[0]
    return pl.pallas_call(
        functools.partial(_diff_kernel, t=t),
        grid=(b, DIFF_HEADS, nk),
        in_specs=[
            pl.BlockSpec(memory_space=pltpu.SMEM),
            pl.BlockSpec((None, w, t), lambda bi, h, qi: (bi, h, qi)),
            pl.BlockSpec((None, w, 2 * t), lambda bi, h, qi: (h, 0, 0)),
            pl.BlockSpec((None, None, nk, t, w), lambda bi, h, qi: (bi, h, 0, 0, 0)),
            pl.BlockSpec((t, w), lambda bi, h, qi: (0, 0)),
            pl.BlockSpec((None, None, nk, w, t), lambda bi, h, qi: (bi, h, 0, 0, 0)),
            pl.BlockSpec(ones.shape, lambda bi, h, qi: (0, 0)),
            pl.BlockSpec((4, HEAD_DIM), lambda bi, h, qi: (0, 0)),
            pl.BlockSpec((w, 1), lambda bi, h, qi: (0, 0)),
        ],
        out_specs=pl.BlockSpec((None, w, t), lambda bi, h, qi: (bi, h, qi)),
        out_shape=jax.ShapeDtypeStruct((b, DIFF_WIDTH, s), BF16),
        scratch_shapes=[
            pltpu.VMEM((2 * w, 2 * t), BF16),
            pltpu.VMEM((2, t, 2 * t), F32),
            pltpu.VMEM((2, 1, 2 * t), F32),
            pltpu.VMEM((2, t, 2 * t), BF16),
            pltpu.VMEM((2, 1, 2 * t), F32),
            pltpu.VMEM((1, 2 * t), F32),
            pltpu.VMEM((wv, 2 * t), F32),
        ],
        compiler_params=_cparams(("arbitrary", "arbitrary", "arbitrary")),
        name="diffattn",
    )(cs, qT, qx, k5, kx, v5, ones, lamv, subln_g)


def _merge_kernel(ya_ref, yd_ref, g0_ref, g1_ref, x_ref, wb0_ref, wb1_ref, wo_ref, o_ref):
    ua = jnp.dot(ya_ref[...], wb0_ref[...], preferred_element_type=F32)
    ud = jnp.dot(yd_ref[...], wb1_ref[...], preferred_element_type=F32)
    merged = jax.nn.sigmoid(g0_ref[...].astype(F32)) * ua + jax.nn.sigmoid(g1_ref[...].astype(F32)) * ud
    o_ref[...] = x_ref[...] + jnp.dot(merged.astype(BF16), wo_ref[...], preferred_element_type=F32)


def _merge(ya, yd, proj, x2, wb, wo):
    t, d = x2.shape
    tm = min(MERGE_TM, t)
    bw = ya.shape[1]
    const = dict(pipeline_mode=pl.Buffered(1))
    return pl.pallas_call(
        _merge_kernel,
        grid=(t // tm,),
        in_specs=[
            pl.BlockSpec((tm, bw), lambda i: (i, 0)),
            pl.BlockSpec((tm, bw), lambda i: (i, 0)),
            pl.BlockSpec((tm, d), lambda i: (i, 0)),
            pl.BlockSpec((tm, d), lambda i: (i, 1)),
            pl.BlockSpec((tm, d), lambda i: (i, 0)),
            pl.BlockSpec((None, bw, d), lambda i: (0, 0, 0), **const),
            pl.BlockSpec((None, bw, d), lambda i: (1, 0, 0), **const),
            pl.BlockSpec((d, d), lambda i: (0, 0), **const),
        ],
        out_specs=pl.BlockSpec((tm, d), lambda i: (i, 0)),
        out_shape=jax.ShapeDtypeStruct((t, d), F32),
        compiler_params=_cparams(("parallel",)),
        name="merge",
    )(ya, yd, proj, proj, x2, wb, wb, wo)


def _memkv_kernel(mem_ref, g_ref, w_ref, k_ref, v_ref):
    mn = _rms(mem_ref[...], g_ref[...]).astype(BF16)
    kv = jnp.dot(mn, w_ref[...], preferred_element_type=F32)
    k_ref[...] = kv[:, :X_WIDTH].astype(k_ref.dtype)
    v_ref[...] = kv[:, X_WIDTH:].astype(v_ref.dtype)


def _memkv(mem, g, w):
    b, m, d = mem.shape
    out = jax.ShapeDtypeStruct((b, m, X_WIDTH), BF16)
    return pl.pallas_call(
        _memkv_kernel,
        grid=(b,),
        in_specs=[
            pl.BlockSpec((None, m, d), lambda i: (i, 0, 0)),
            pl.BlockSpec((1, d), lambda i: (0, 0)),
            pl.BlockSpec((d, 2 * X_WIDTH), lambda i: (0, 0)),
        ],
        out_specs=[pl.BlockSpec((None, m, X_WIDTH), lambda i: (i, 0, 0))] * 2,
        out_shape=[out, out],
        compiler_params=_cparams(("parallel",)),
        name="memkv",
    )(mem, g, w)


def _cross_router_kernel(h_ref, k_ref, v_ref, gc_ref, wq_ref, wo_ref, gf_ref, wr_ref, br_ref,
                         h2_ref, xn_ref, idx_ref, gate_ref, rank_ref, cnt_ref, run_ref):
    i = pl.program_id(0)
    tm, d = h_ref.shape

    @pl.when(i == 0)
    def _():
        run_ref[...] = jnp.zeros(run_ref.shape, F32)

    h = h_ref[...]
    hn = _rms(h, gc_ref[...]).astype(BF16)
    q = jnp.dot(hn, wq_ref[...], preferred_element_type=F32).astype(BF16)
    k = k_ref[...]
    v = v_ref[...]
    outs = []
    for hd in range(X_HEADS):
        sl = slice(X_HEAD_DIM * hd, X_HEAD_DIM * (hd + 1))
        s = lax.dot_general(q[:, sl], k[:, sl], (((1,), (1,)), ((), ())),
                            preferred_element_type=F32) * (X_HEAD_DIM ** -0.5)
        s = s - jnp.max(s, axis=-1, keepdims=True)
        p = jnp.exp(s)
        p = p / jnp.sum(p, axis=-1, keepdims=True)
        outs.append(jnp.dot(p.astype(BF16), v[:, sl], preferred_element_type=F32))
    o = jnp.concatenate(outs, axis=1).astype(BF16)
    h2 = h + jnp.dot(o, wo_ref[...], preferred_element_type=F32)
    h2_ref[...] = h2

    xn = _rms(h2, gf_ref[...])
    packed = _pack_pair(xn[:, :d // 2], xn[:, d // 2:])
    for j in range(V7X_SUBLANES):
        xn_ref[pl.ds(j, tm, stride=V7X_SUBLANES), :] = packed[:, j * V7X_LANES:(j + 1) * V7X_LANES]
    logits = jnp.dot(xn.astype(BF16), wr_ref[...], preferred_element_type=F32) + br_ref[...]

    lane = lax.broadcasted_iota(jnp.int32, logits.shape, 1).astype(F32)
    work = logits
    vals, idxs = [], []
    onehot = jnp.zeros(logits.shape, F32)
    for _ in range(TOP_K):
        mx = jnp.max(work, axis=-1, keepdims=True)
        ix = jnp.min(jnp.where(work == mx, lane, float(N_EXPERTS)), axis=-1, keepdims=True)
        sel = lane == ix
        vals.append(mx)
        idxs.append(ix)
        onehot = jnp.where(sel, 1.0, onehot)
        work = jnp.where(sel, NEG_INF, work)
    es = [jnp.exp(vv - vals[0]) for vv in vals]
    tot = es[0] + es[1] + es[2] + es[3]

    r = lax.broadcasted_iota(jnp.int32, (tm, tm), 0)
    c = lax.broadcasted_iota(jnp.int32, (tm, tm), 1)
    lower = jnp.where(c < r, 1.0, 0.0).astype(BF16)
    prefix = jnp.dot(lower, onehot.astype(BF16), preferred_element_type=F32) + run_ref[...]
    for kk in range(TOP_K):
        idx_ref[:, kk:kk + 1] = idxs[kk].astype(jnp.int32)
        gate_ref[:, kk:kk + 1] = es[kk] / tot
        rk = jnp.sum(jnp.where(lane == idxs[kk], prefix, 0.0), axis=-1, keepdims=True)
        rank_ref[:, kk:kk + 1] = rk.astype(jnp.int32)
    run_ref[...] = run_ref[...] + jnp.sum(onehot, axis=0, keepdims=True)
    cnt_ref[...] = run_ref[...].astype(jnp.int32)


def _cross_router(h1, kmem, vmem, gc, wq, wo, gf, wr, br, seq):
    t, d = h1.shape
    tm = min(CROSS_TM, seq)
    per_b = seq // tm
    m = kmem.shape[1]
    const = dict(pipeline_mode=pl.Buffered(1))
    outs = pl.pallas_call(
        _cross_router_kernel,
        grid=(t // tm,),
        in_specs=[
            pl.BlockSpec((tm, d), lambda i: (i, 0)),
            pl.BlockSpec((None, m, X_WIDTH), lambda i: (i // per_b, 0, 0)),
            pl.BlockSpec((None, m, X_WIDTH), lambda i: (i // per_b, 0, 0)),
            pl.BlockSpec((1, d), lambda i: (0, 0)),
            pl.BlockSpec((d, X_WIDTH), lambda i: (0, 0), **const),
            pl.BlockSpec((X_WIDTH, d), lambda i: (0, 0), **const),
            pl.BlockSpec((1, d), lambda i: (0, 0)),
            pl.BlockSpec((d, N_EXPERTS), lambda i: (0, 0), **const),
            pl.BlockSpec((1, N_EXPERTS), lambda i: (0, 0)),
        ],
        out_specs=[
            pl.BlockSpec((tm, d), lambda i: (i, 0)),
            pl.BlockSpec((tm * V7X_SUBLANES, V7X_LANES), lambda i: (i, 0)),
            pl.BlockSpec((tm, TOP_K), lambda i: (i, 0)),
            pl.BlockSpec((tm, TOP_K), lambda i: (i, 0)),
            pl.BlockSpec((tm, TOP_K), lambda i: (i, 0)),
            pl.BlockSpec((1, N_EXPERTS), lambda i: (0, 0)),
        ],
        out_shape=[
            jax.ShapeDtypeStruct((t, d), F32),
            jax.ShapeDtypeStruct((t * V7X_SUBLANES, V7X_LANES), U32),
            jax.ShapeDtypeStruct((t, TOP_K), jnp.int32),
            jax.ShapeDtypeStruct((t, TOP_K), F32),
            jax.ShapeDtypeStruct((t, TOP_K), jnp.int32),
            jax.ShapeDtypeStruct((1, N_EXPERTS), jnp.int32),
        ],
        scratch_shapes=[pltpu.VMEM((1, N_EXPERTS), F32)],
        compiler_params=_cparams(("arbitrary",)),
        name="cross_router",
    )(h1, kmem, vmem, gc, wq, wo, gf, wr, br)
    return outs


def _tile_rows(row):
    return pl.ds(pl.multiple_of(row * V7X_SUBLANES, V7X_SUBLANES), V7X_SUBLANES)


def _dispatch_kernel(pos_ref, tail_ref, x_ref, xs_ref, zero_ref, sem, zsem):
    i = pl.program_id(0)
    tm = x_ref.shape[0] // V7X_SUBLANES
    chunk_rows = zero_ref.shape[0]

    @pl.when(i == 0)
    def _():
        zero_ref[...] = jnp.zeros(zero_ref.shape, zero_ref.dtype)

        def tail_copy(e):
            row0 = pl.multiple_of(tail_ref[e] * V7X_SUBLANES, V7X_SUBLANES)
            return pltpu.make_async_copy(zero_ref, xs_ref.at[pl.ds(row0, chunk_rows)], zsem)

        for e in range(N_EXPERTS):
            @pl.when(tail_ref[e] >= 0)
            def _():
                tail_copy(e).start()
        for e in range(N_EXPERTS):
            @pl.when(tail_ref[e] >= 0)
            def _():
                tail_copy(e).wait()

    def row_copy(r, p):
        return pltpu.make_async_copy(x_ref.at[_tile_rows(r)], xs_ref.at[_tile_rows(p)], sem)

    def issue(r, carry):
        for kk in range(TOP_K):
            row_copy(r, pos_ref[0, 0, r * TOP_K + kk]).start(priority=kk % 2)
        return carry

    lax.fori_loop(0, tm, issue, 0, unroll=DMA_ISSUE_UNROLL)
    for _ in range(tm * TOP_K):
        row_copy(0, 0).wait()


def _dispatch(xn, pos, tails, n_rows):
    t = xn.shape[0] // V7X_SUBLANES
    tm = min(DISPATCH_TM, t)
    pos3 = pos.reshape(t // tm, 1, tm * TOP_K)
    return pl.pallas_call(
        _dispatch_kernel,
        grid=(t // tm,),
        in_specs=[
            pl.BlockSpec((1, 1, tm * TOP_K), lambda i: (i, 0, 0), memory_space=pltpu.SMEM),
            pl.BlockSpec(memory_space=pltpu.SMEM),
            pl.BlockSpec((tm * V7X_SUBLANES, V7X_LANES), lambda i: (i, 0)),
        ],
        out_specs=pl.BlockSpec(memory_space=pl.ANY),
        out_shape=jax.ShapeDtypeStruct((n_rows * V7X_SUBLANES, V7X_LANES), xn.dtype),
        scratch_shapes=[pltpu.VMEM((EXPERT_CHUNK * V7X_SUBLANES, V7X_LANES), xn.dtype),
                        pltpu.SemaphoreType.DMA(()), pltpu.SemaphoreType.DMA(())],
        compiler_params=_cparams(("arbitrary",)),
        name="dispatch",
    )(pos3, tails, xn)


def _expert_kernel(ve_ref, vr_ref, vb_ref, x_ref, wg_ref, wl_ref, bg_ref, bl_ref, wd_ref, bd_ref, o_ref,
                   act_ref, wa_ref, wb_ref):
    del ve_ref, vb_ref
    v = pl.program_id(0)
    s = pl.program_id(1)
    rows = vr_ref[v]
    nj = act_ref.shape[0]
    tf = act_ref.shape[2]
    ch = EXPERT_CHUNK
    n_chunks = x_ref.shape[0] // (ch * V7X_SUBLANES)
    half = o_ref.shape[1]

    @pl.when((rows > 0) & (s < nj))
    def _():
        wa_ref[...] = wg_ref[...].astype(BF16)
        wb_ref[...] = wl_ref[...].astype(BF16)
        for c in range(n_chunks):
            @pl.when(c * ch < rows)
            def _():
                xw = [x_ref[pl.ds(c * ch * V7X_SUBLANES + j, ch, stride=V7X_SUBLANES), :]
                      for j in range(V7X_SUBLANES)]
                xb = jnp.concatenate([_unpack_lo(u).astype(BF16) for u in xw]
                                     + [_unpack_hi(u).astype(BF16) for u in xw], axis=1)
                glu = jnp.dot(xb, wa_ref[...], preferred_element_type=F32) + bg_ref[...]
                lin = jnp.dot(xb, wb_ref[...], preferred_element_type=F32) + bl_ref[...]
                glu = jnp.minimum(glu, SWIGLU_LIMIT)
                lin = jnp.clip(lin, -SWIGLU_LIMIT, SWIGLU_LIMIT)
                act = glu * jax.nn.sigmoid(SWIGLU_ALPHA * glu) * (lin + 1.0)
                act_ref[s, pl.ds(c * ch, ch), :] = act.astype(BF16)

    @pl.when((rows > 0) & (s >= nj))
    def _():
        wa_ref[...] = wd_ref[...].astype(BF16)
        for c in range(n_chunks):
            sl = pl.ds(c * ch, ch)

            @pl.when(c * ch < rows)
            def _():
                acc = jnp.dot(act_ref[0, sl, :], wa_ref[0:tf, :], preferred_element_type=F32)
                for j in range(1, nj):
                    acc = acc + jnp.dot(act_ref[j, sl, :], wa_ref[j * tf:(j + 1) * tf, :],
                                        preferred_element_type=F32)
                acc = acc + bd_ref[...]
                o_ref[sl, :] = _pack_pair(acc[:, :half], acc[:, half:])

            @pl.when(c * ch >= rows)
            def _():
                o_ref[sl, :] = jnp.zeros((ch, half), o_ref.dtype)


def _experts(xs, visit_expert, visit_rows, visit_blk, n_used, w_gu, b_gu, w_down, b_down):
    n_rows = xs.shape[0] // V7X_SUBLANES
    e, d, ff2 = w_gu.shape
    ff = ff2 // 2
    r, tf = EXPERT_ROWS, EXPERT_TF
    nj = ff // tf
    nn = d // tf
    b_gu3 = b_gu.reshape(e, 1, ff2)
    b_down3 = b_down.reshape(e, 1, d)

    def up(v, s, vr):
        return jnp.where(vr[v] > 0, jnp.minimum(s, nj - 1), nj - 1)

    def down(v, s, vr):
        return jnp.where(vr[v] > 0, jnp.maximum(s - nj, 0), nn - 1)

    grid_spec = pltpu.PrefetchScalarGridSpec(
        num_scalar_prefetch=3,
        grid=(n_used, nj + nn),
        in_specs=[
            pl.BlockSpec((r * V7X_SUBLANES, V7X_LANES), lambda v, s, ve, vr, vb: (vb[v], 0),
                         pipeline_mode=pl.Buffered(1)),
            pl.BlockSpec((None, d, tf), lambda v, s, ve, vr, vb: (ve[v], 0, up(v, s, vr))),
            pl.BlockSpec((None, d, tf), lambda v, s, ve, vr, vb: (ve[v], 0, nj + up(v, s, vr))),
            pl.BlockSpec((None, 1, tf), lambda v, s, ve, vr, vb: (ve[v], 0, up(v, s, vr))),
            pl.BlockSpec((None, 1, tf), lambda v, s, ve, vr, vb: (ve[v], 0, nj + up(v, s, vr))),
            pl.BlockSpec((None, ff, tf), lambda v, s, ve, vr, vb: (ve[v], 0, down(v, s, vr))),
            pl.BlockSpec((None, 1, tf), lambda v, s, ve, vr, vb: (ve[v], 0, down(v, s, vr))),
        ],
        out_specs=pl.BlockSpec((r, tf // 2), lambda v, s, ve, vr, vb: (vb[v], down(v, s, vr))),
        scratch_shapes=[
            pltpu.VMEM((nj, r, tf), BF16),
            pltpu.VMEM((d, tf), BF16),
            pltpu.VMEM((d, tf), BF16),
        ],
    )
    return pl.pallas_call(
        _expert_kernel,
        grid_spec=grid_spec,
        out_shape=jax.ShapeDtypeStruct((n_rows, d // 2), U32),
        compiler_params=_cparams(("arbitrary", "arbitrary")),
        name="experts",
    )(visit_expert, visit_rows, visit_blk, xs, w_gu, w_gu, b_gu3, b_gu3, w_down, b_down3)


def _combine_kernel(pos_ref, h_ref, gate_ref, g_ref, ys_ref, o_ref, buf_ref, sem):
    tm = h_ref.shape[0]
    half = EXPERT_TF // 2

    def row_copy(r, kk, p):
        return pltpu.make_async_copy(ys_ref.at[pl.ds(p, 1)], buf_ref.at[kk, pl.ds(r, 1)], sem)

    def issue(r, carry):
        for kk in range(TOP_K):
            row_copy(r, kk, pos_ref[0, 0, r * TOP_K + kk]).start(priority=kk % 2)
        return carry

    lax.fori_loop(0, tm, issue, 0, unroll=DMA_ISSUE_UNROLL)
    for _ in range(tm * TOP_K):
        row_copy(0, 0, 0).wait()

    gates = gate_ref[...]
    y = h_ref[...]
    for kk in range(TOP_K):
        w = buf_ref[kk]
        lo = _unpack_lo(w)
        hi = _unpack_hi(w)
        pieces = []
        for n in range(w.shape[1] // half):
            pieces += [lo[:, n * half:(n + 1) * half], hi[:, n * half:(n + 1) * half]]
        y = y + gates[:, kk:kk + 1] * jnp.concatenate(pieces, axis=1)
    o_ref[...] = _rms(y, g_ref[...])


def _combine(h2, gates, pos, ys, g_final):
    t, d = h2.shape
    tm = min(COMBINE_TM, t)
    pos3 = pos.reshape(t // tm, 1, tm * TOP_K)
    return pl.pallas_call(
        _combine_kernel,
        grid=(t // tm,),
        in_specs=[
            pl.BlockSpec((1, 1, tm * TOP_K), lambda i: (i, 0, 0), memory_space=pltpu.SMEM),
            pl.BlockSpec((tm, d), lambda i: (i, 0)),
            pl.BlockSpec((tm, TOP_K), lambda i: (i, 0)),
            pl.BlockSpec((1, d), lambda i: (0, 0)),
            pl.BlockSpec(memory_space=pl.ANY),
        ],
        out_specs=pl.BlockSpec((tm, d), lambda i: (i, 0)),
        out_shape=jax.ShapeDtypeStruct((t, d), F32),
        scratch_shapes=[pltpu.VMEM((TOP_K, tm, ys.shape[1]), ys.dtype), pltpu.SemaphoreType.DMA(())],
        compiler_params=_cparams(("arbitrary",)),
        name="combine",
    )(pos3, h2, gates, g_final, ys)


def _routing_tables(counts, top_idx, rank, n_visits):
    r, ch = EXPERT_ROWS, EXPERT_CHUNK
    counts = counts.reshape(N_EXPERTS)
    padded = (counts + r - 1) // r * r
    ends = jnp.cumsum(padded)
    starts = ends - padded
    experts = jnp.arange(N_EXPERTS, dtype=jnp.int32)
    start_of = jnp.sum(jnp.where(top_idx[..., None] == experts, starts, 0), axis=-1)
    pos = start_of + rank
    visit_row0 = jnp.arange(n_visits, dtype=jnp.int32) * r
    owner = jnp.sum((ends[None, :] <= visit_row0[:, None]).astype(jnp.int32), axis=1)
    visit_expert = jnp.minimum(owner, N_EXPERTS - 1).astype(jnp.int32)
    visit_rows = jnp.clip(counts[visit_expert] - (visit_row0 - starts[visit_expert]), 0, r).astype(jnp.int32)
    n_used = (ends[-1] // r).astype(jnp.int32)
    last = jnp.maximum(n_used - 1, 0)
    ids = jnp.arange(n_visits, dtype=jnp.int32)
    active = ids < n_used
    visit_rows = jnp.where(active, visit_rows, 0)
    visit_blk = jnp.where(active, ids, last)
    visit_expert = jnp.where(active, visit_expert, visit_expert[last])
    tail = starts + jnp.minimum(counts // ch * ch, padded - ch)
    tail = jnp.where(counts > 0, tail, -1)
    return pos.astype(jnp.int32), visit_expert, visit_rows, visit_blk, tail.astype(jnp.int32), n_used


def kernel(x, mem, norm_mix_g, w_in, sinks, lambda_q1, lambda_k1, lambda_q2, lambda_k2, diff_subln_g, w_branch, w_o, norm_cross_g, norm_mem_g, w_cq, w_ckv, w_co, norm_ffn_g, w_router, b_router, w_gate_up, b_gate_up, w_down, b_down, norm_final_g):
    b, s, d = x.shape
    t = b * s
    x2 = x.reshape(t, d)
    l = 0

    w = w_in[l]
    o_qa, o_ka, o_va = 0, SWA_WIDTH, SWA_WIDTH + SWA_KV_WIDTH
    o_qd = o_va + SWA_KV_WIDTH
    o_kd, o_vd, o_g = o_qd + DIFF_WIDTH, o_qd + 2 * DIFF_WIDTH, o_qd + 3 * DIFF_WIDTH
    w_perm = jnp.concatenate([
        w[:, o_g:], w[:, o_qd:o_kd] * jnp.float32(HEAD_DIM ** -0.5 * LOG2E), w[:, o_kd:o_vd], w[:, o_vd:o_g],
        w[:, o_qa:o_ka], w[:, o_ka:o_va], w[:, o_va:o_qd]], axis=1).astype(BF16)
    proj = _inproj(x2, norm_mix_g[l].reshape(1, d), w_perm)
    c_qd = 2 * d
    c_kd, c_vd, c_qa = c_qd + DIFF_WIDTH, c_qd + 2 * DIFF_WIDTH, c_qd + 3 * DIFF_WIDTH
    c_ka, c_va = c_qa + SWA_WIDTH, c_qa + SWA_WIDTH + SWA_KV_WIDTH

    def seg(c0, width):
        return proj[:, c0:c0 + width].reshape(b, s, width)

    qaT = jnp.swapaxes(seg(c_qa, SWA_WIDTH), 1, 2)
    vaT = jnp.swapaxes(seg(c_va, SWA_KV_WIDTH), 1, 2)
    yaT = _swa(qaT, seg(c_ka, SWA_KV_WIDTH), vaT, sinks[l])
    ya = jnp.swapaxes(yaT, 1, 2).reshape(t, SWA_WIDTH)

    td = min(DIFF_T, s)
    nk = s // td
    hw = DIFF_HEAD_WIDTH
    cs, qx, kx = _diff_tables(td)
    qdT = jnp.swapaxes(seg(c_qd, DIFF_WIDTH), 1, 2)
    k5 = seg(c_kd, DIFF_WIDTH).reshape(b, nk, td, DIFF_HEADS, hw).transpose(0, 3, 1, 2, 4)
    v5 = seg(c_vd, DIFF_WIDTH).reshape(b, nk, td, DIFF_HEADS, hw).transpose(0, 3, 1, 4, 2)
    ones = jnp.zeros((DIFF_DENOM_ROWS, td), BF16).at[0].set(1.0)
    lamv = jnp.stack([lambda_q1[l], lambda_k1[l], lambda_q2[l], lambda_k2[l]]).astype(F32)
    ydT = _diff(qdT, k5, v5, qx, kx, ones, cs, lamv, diff_subln_g[l].astype(F32).reshape(hw, 1))
    yd = jnp.swapaxes(ydT, 1, 2).reshape(t, DIFF_WIDTH)

    h1 = _merge(ya, yd, proj, x2, w_branch[l].astype(BF16), w_o[l].astype(BF16))

    kmem, vmem = _memkv(mem, norm_mem_g[l].reshape(1, d), w_ckv[l].astype(BF16))
    h2, xn, top_idx, gates, rank, counts = _cross_router(
        h1, kmem, vmem, norm_cross_g[l].reshape(1, d), w_cq[l].astype(BF16), w_co[l].astype(BF16),
        norm_ffn_g[l].reshape(1, d), w_router[l].astype(BF16), b_router[l].reshape(1, N_EXPERTS), s)

    assert d // 2 == V7X_SUBLANES * V7X_LANES
    n_visits = (t * TOP_K) // EXPERT_ROWS + N_EXPERTS
    n_rows = n_visits * EXPERT_ROWS
    pos, visit_expert, visit_rows, visit_blk, tails, n_used = _routing_tables(counts, top_idx, rank, n_visits)
    xs = _dispatch(xn, pos, tails, n_rows)
    ys = _experts(xs, visit_expert, visit_rows, visit_blk, n_used, w_gate_up[l], b_gate_up[l], w_down[l], b_down[l])
    out = _combine(h2, gates, pos, ys, norm_final_g.reshape(1, d))
    return out.reshape(b, s, d)
```

```python
import functools
import math

import jax
import jax.numpy as jnp
from jax import lax
from jax.experimental import pallas as pl
from jax.experimental.pallas import tpu as pltpu

BF16 = jnp.bfloat16
F32 = jnp.float32
U32 = jnp.uint32
NEG_INF = float("-inf")
LOG2E = math.log2(math.e)

HEAD_DIM = 64
SWA_Q_HEADS = 16
SWA_KV_HEADS = 4
SWA_GROUP = SWA_Q_HEADS // SWA_KV_HEADS
SWA_BLOCK = 128
SWA_WIDTH = SWA_Q_HEADS * HEAD_DIM
SWA_KV_WIDTH = SWA_KV_HEADS * HEAD_DIM
DIFF_HEADS = 8
DIFF_HEAD_WIDTH = 2 * HEAD_DIM
DIFF_WIDTH = DIFF_HEADS * DIFF_HEAD_WIDTH
X_HEADS = 4
X_HEAD_DIM = 128
X_WIDTH = X_HEADS * X_HEAD_DIM
N_EXPERTS = 32
TOP_K = 4
SWIGLU_LIMIT = 7.0
SWIGLU_ALPHA = 1.702
RMS_EPS = 1e-5
LAMBDA_INIT = 0.8 - 0.6 * math.exp(-0.3 * 0)

V7X_VMEM_BYTES = 64 * 1024 * 1024
VMEM_LIMIT = V7X_VMEM_BYTES - 8 * 1024 * 1024
V7X_LANES = 128
V7X_SUBLANES = 8

INPROJ_TM = 512
INPROJ_TN = 2176
DIFF_T = 512
DIFF_POS_SPLIT = 256
DIFF_DENOM_ROWS = 16
MERGE_TM = 256
CROSS_TM = 256
DISPATCH_TM = 256
EXPERT_CHUNK = 768
EXPERT_ROWS = 3 * EXPERT_CHUNK
EXPERT_TF = 512
COMBINE_TM = 256
DMA_ISSUE_UNROLL = 8


def _cparams(semantics):
    return pltpu.CompilerParams(dimension_semantics=semantics, vmem_limit_bytes=VMEM_LIMIT)


def _rms(x, g):
    ms = jnp.mean(x * x, axis=-1, keepdims=True)
    return x * lax.rsqrt(ms + RMS_EPS) * g


def _pack_pair(lo, hi):
    lo_bits = lax.bitcast_convert_type(lo.astype(BF16).astype(F32), U32)
    hi_bits = lax.bitcast_convert_type(hi.astype(BF16).astype(F32), U32)
    return hi_bits | (lo_bits >> 16)


def _unpack_lo(w):
    return lax.bitcast_convert_type(w << 16, F32)


def _unpack_hi(w):
    return lax.bitcast_convert_type(w & jnp.uint32(0xFFFF0000), F32)


def _inproj_kernel(x_ref, g_ref, w_ref, o_ref, xn_ref):
    @pl.when(pl.program_id(1) == 0)
    def _():
        xn_ref[...] = _rms(x_ref[...], g_ref[...]).astype(BF16)

    o_ref[...] = jnp.dot(xn_ref[...], w_ref[...], preferred_element_type=F32).astype(o_ref.dtype)


def _inproj(x2, g, w):
    t, d = x2.shape
    n = w.shape[1]
    tm, tn = min(INPROJ_TM, t), INPROJ_TN
    return pl.pallas_call(
        _inproj_kernel,
        grid=(t // tm, n // tn),
        in_specs=[
            pl.BlockSpec((tm, d), lambda i, j: (i, 0)),
            pl.BlockSpec((1, d), lambda i, j: (0, 0)),
            pl.BlockSpec((d, tn), lambda i, j: (0, j)),
        ],
        out_specs=pl.BlockSpec((tm, tn), lambda i, j: (i, j)),
        out_shape=jax.ShapeDtypeStruct((t, n), BF16),
        scratch_shapes=[pltpu.VMEM((tm, d), BF16)],
        compiler_params=_cparams(("parallel", "arbitrary")),
        name="inproj",
    )(x2, g, w)


def _swa_kernel(qT_ref, kp_ref, kc_ref, vp_ref, vc_ref, bias_ref, sink_ref, o_ref):
    n = pl.program_id(1)
    blk = SWA_BLOCK
    kband = jnp.concatenate([kp_ref[...], kc_ref[...]], axis=0)
    vband = jnp.concatenate([vp_ref[...], vc_ref[...]], axis=1)
    krow = lax.broadcasted_iota(jnp.int32, (2 * blk, SWA_GROUP * blk), 0)
    has_prev = n > 0
    zeros = jnp.zeros((HEAD_DIM, blk), BF16)
    for h in range(SWA_KV_HEADS):
        lane0 = 2 * HEAD_DIM * (h // 2)
        k128 = kband[:, lane0:lane0 + 2 * HEAD_DIM]
        pieces = []
        for g in range(SWA_GROUP):
            hq = SWA_GROUP * h + g
            qh = qT_ref[HEAD_DIM * hq:HEAD_DIM * (hq + 1), :]
            pieces.append(jnp.concatenate([qh, zeros] if h % 2 == 0 else [zeros, qh], axis=0))
        qz = jnp.concatenate(pieces, axis=1)
        s = jnp.dot(k128, qz, preferred_element_type=F32) * (HEAD_DIM ** -0.5)
        s = s + bias_ref[h]
        s = jnp.where((krow >= blk) | has_prev, s, NEG_INF)
        sink = sink_ref[h]
        m = jnp.maximum(jnp.max(s, axis=0, keepdims=True), sink)
        p = jnp.exp(s - m)
        denom = jnp.sum(p, axis=0, keepdims=True) + jnp.exp(sink - m)
        vh = vband[HEAD_DIM * h:HEAD_DIM * (h + 1), :]
        o = jnp.dot(vh, p.astype(BF16), preferred_element_type=F32) / denom
        for g in range(SWA_GROUP):
            hq = SWA_GROUP * h + g
            o_ref[HEAD_DIM * hq:HEAD_DIM * (hq + 1), :] = o[:, blk * g:blk * (g + 1)].astype(o_ref.dtype)


def _swa_tables(sinks):
    blk = SWA_BLOCK
    slopes = (2.0 ** (-8.0 * jnp.arange(1, SWA_Q_HEADS + 1, dtype=F32) / SWA_Q_HEADS)).reshape(SWA_KV_HEADS, SWA_GROUP)
    q_pos = jnp.arange(blk)[None, :] + blk
    k_pos = jnp.arange(2 * blk)[:, None]
    dist = q_pos - k_pos
    allowed = (dist >= 0) & (dist < SWA_BLOCK)
    bias = -slopes[:, :, None, None] * dist.astype(F32)[None, None]
    bias = jnp.where(allowed[None, None], bias, NEG_INF)
    bias = jnp.transpose(bias, (0, 2, 1, 3)).reshape(SWA_KV_HEADS, 2 * blk, SWA_GROUP * blk)
    sink = sinks.astype(F32).reshape(SWA_KV_HEADS, SWA_GROUP, 1)
    sink = jnp.broadcast_to(sink, (SWA_KV_HEADS, SWA_GROUP, blk)).reshape(SWA_KV_HEADS, 1, SWA_GROUP * blk)
    return bias, sink


def _swa(qT, k, k_col0, vT, sinks):
    b, _, s = qT.shape
    blk = SWA_BLOCK
    bias, sink = _swa_tables(sinks)
    prev = lambda n: jnp.maximum(n - 1, 0)
    kb = k_col0 // SWA_KV_WIDTH
    return pl.pallas_call(
        _swa_kernel,
        grid=(b, s // blk),
        in_specs=[
            pl.BlockSpec((None, SWA_WIDTH, blk), lambda bi, n: (bi, 0, n)),
            pl.BlockSpec((None, blk, SWA_KV_WIDTH), lambda bi, n: (bi, prev(n), kb)),
            pl.BlockSpec((None, blk, SWA_KV_WIDTH), lambda bi, n: (bi, n, kb)),
            pl.BlockSpec((None, SWA_KV_WIDTH, blk), lambda bi, n: (bi, 0, prev(n))),
            pl.BlockSpec((None, SWA_KV_WIDTH, blk), lambda bi, n: (bi, 0, n)),
            pl.BlockSpec(bias.shape, lambda bi, n: (0, 0, 0)),
            pl.BlockSpec(sink.shape, lambda bi, n: (0, 0, 0)),
        ],
        out_specs=pl.BlockSpec((None, SWA_WIDTH, blk), lambda bi, n: (bi, 0, n)),
        out_shape=jax.ShapeDtypeStruct((b, SWA_WIDTH, s), BF16),
        compiler_params=_cparams(("parallel", "arbitrary")),
        name="swa",
    )(qT, k, k, vT, vT, bias, sink)


def _diff_kernel(c_ref, qT_ref, qx_ref, k_ref, kx_ref, v_ref, ones_ref, lamv_ref, g_ref, o_ref,
                 q2_ref, z_ref, zm_ref, p_ref, a_ref, m_ref, acc_ref, *, t):
    h = pl.program_id(1)
    qi = pl.program_id(2)
    w = DIFF_HEAD_WIDTH
    c = c_ref[h]

    @pl.when(qi == 0)
    def _():
        q2_ref[w:, :] = qx_ref[...]

    qt = qT_ref[...]
    row = lax.broadcasted_iota(jnp.int32, qt.shape, 0)
    zero = jnp.zeros_like(qt)
    q2_ref[:w, :t] = jnp.where(row < HEAD_DIM, qt, zero)
    q2_ref[:w, t:] = jnp.where(row >= HEAD_DIM, qt, zero)
    m_ref[...] = jnp.full(m_ref.shape, NEG_INF, F32)
    acc_ref[...] = jnp.zeros(acc_ref.shape, F32)
    p_ref[1] = jnp.zeros(p_ref.shape[1:], BF16)
    a_ref[1] = jnp.ones(a_ref.shape[1:], F32)

    def scores(k, slot):
        keys = jnp.concatenate([k_ref[k], kx_ref[...]], axis=1)
        z = jnp.dot(keys, q2_ref[...], preferred_element_type=F32)
        z_ref[slot] = z
        zm_ref[slot] = jnp.max(z, axis=0, keepdims=True)

    def values(k, slot):
        vals = jnp.concatenate([v_ref[jnp.maximum(k, 0)], ones_ref[...]], axis=0)
        pv = jnp.dot(vals, p_ref[slot], preferred_element_type=F32)
        acc_ref[...] = a_ref[slot] * acc_ref[...] + pv

    def softmax(k, slot, masked):
        z = z_ref[slot]
        if masked:
            kr = lax.broadcasted_iota(jnp.int32, z.shape, 0)
            qc = lax.broadcasted_iota(jnp.int32, z.shape, 1)
            qc = jnp.where(qc >= t, qc - t, qc)
            z = jnp.where(kr <= qc, z, NEG_INF)
            zmax = jnp.max(z, axis=0, keepdims=True)
        else:
            zmax = zm_ref[slot]
        off = c * ((k - qi) * t).astype(F32)
        m_old = m_ref[...]
        m_new = jnp.maximum(m_old, zmax + off)
        p_ref[slot] = jnp.exp2(z - (m_new - off)).astype(BF16)
        a_ref[slot] = jnp.exp2(m_old - m_new)
        m_ref[...] = m_new

    scores(0, 0)

    def pair(j, carry):
        k = 2 * j
        scores(k + 1, 1)
        values(k - 1, 1)
        softmax(k, 0, False)
        scores(k + 2, 0)
        values(k, 0)
        softmax(k + 1, 1, False)
        return carry

    lax.fori_loop(0, qi >> 1, pair, 0)

    @pl.when((qi & 1) == 0)
    def _():
        values(qi - 1, 1)
        softmax(qi, 0, True)
        values(qi, 0)

    @pl.when((qi & 1) == 1)
    def _():
        scores(qi, 1)
        values(qi - 2, 1)
        softmax(qi - 1, 0, False)
        values(qi - 1, 0)
        softmax(qi, 1, True)
        values(qi, 1)

    lamv = lamv_ref[...]
    lam = (jnp.exp(jnp.sum(lamv[0:1] * lamv[1:2], axis=-1, keepdims=True))
           - jnp.exp(jnp.sum(lamv[2:3] * lamv[3:4], axis=-1, keepdims=True)) + LAMBDA_INIT)
    acc = acc_ref[...]
    l = acc[w:w + 1, :]
    o = acc[:w, :t] / l[:, :t] - lam * (acc[:w, t:] / l[:, t:])
    ms = jnp.mean(o * o, axis=0, keepdims=True)
    y = o * lax.rsqrt(ms + RMS_EPS) * g_ref[...] * (1.0 - LAMBDA_INIT)
    o_ref[...] = y.astype(o_ref.dtype)


def _diff_tables(t):
    w = DIFF_HEAD_WIDTH
    slopes = 2.0 ** (-8.0 * jnp.arange(1, DIFF_HEADS + 1, dtype=F32) / DIFF_HEADS)
    c = slopes * jnp.float32(LOG2E)
    c1 = c.astype(BF16)
    c2 = (c - c1.astype(F32)).astype(BF16)
    c3 = (c - c1.astype(F32) - c2.astype(F32)).astype(BF16)
    parts = jnp.stack([c1, c2, c3, c1, c2, c3], axis=1)
    qx = jnp.zeros((DIFF_HEADS, w, 2 * t), BF16).at[:, :6, :].set(parts[:, :, None])
    r = jnp.arange(t)
    r_lo = (r % DIFF_POS_SPLIT).astype(BF16)
    r_hi = ((r // DIFF_POS_SPLIT) * DIFF_POS_SPLIT).astype(BF16)
    kx = jnp.zeros((t, w), BF16).at[:, 0:3].set(r_lo[:, None]).at[:, 3:6].set(r_hi[:, None])
    return c, qx, kx


def _diff(qT, k4, k_col0, v5, qx, kx, ones, cs, lamv, subln_g):
    b, _, s = qT.shape
    t = k4.shape[2]
    nk = s // t
    w = DIFF_HEAD_WIDTH
    wv = w + ones.shape[0]
    k_blk0 = k_col0 // w
    return pl.pallas_call(
        functools.partial(_diff_kernel, t=t),
        grid=(b, DIFF_HEADS, nk),
        in_specs=[
            pl.BlockSpec(memory_space=pltpu.SMEM),
            pl.BlockSpec((None, w, t), lambda bi, h, qi: (bi, h, qi)),
            pl.BlockSpec((None, w, 2 * t), lambda bi, h, qi: (h, 0, 0)),
            pl.BlockSpec((None, nk, t, w), lambda bi, h, qi: (bi, 0, 0, k_blk0 + h)),
            pl.BlockSpec((t, w), lambda bi, h, qi: (0, 0)),
            pl.BlockSpec((None, None, nk, w, t), lambda bi, h, qi: (bi, h, 0, 0, 0)),
            pl.BlockSpec(ones.shape, lambda bi, h, qi: (0, 0)),
            pl.BlockSpec((4, HEAD_DIM), lambda bi, h, qi: (0, 0)),
            pl.BlockSpec((w, 1), lambda bi, h, qi: (0, 0)),
        ],
        out_specs=pl.BlockSpec((None, w, t), lambda bi, h, qi: (bi, h, qi)),
        out_shape=jax.ShapeDtypeStruct((b, DIFF_WIDTH, s), BF16),
        scratch_shapes=[
            pltpu.VMEM((2 * w, 2 * t), BF16),
            pltpu.VMEM((2, t, 2 * t), F32),
            pltpu.VMEM((2, 1, 2 * t), F32),
            pltpu.VMEM((2, t, 2 * t), BF16),
            pltpu.VMEM((2, 1, 2 * t), F32),
            pltpu.VMEM((1, 2 * t), F32),
            pltpu.VMEM((wv, 2 * t), F32),
        ],
        compiler_params=_cparams(("arbitrary", "arbitrary", "arbitrary")),
        name="diffattn",
    )(cs, qT, qx, k4, kx, v5, ones, lamv, subln_g)


def _merge_kernel(ya_ref, yd_ref, g0_ref, g1_ref, x_ref, wb0_ref, wb1_ref, wo_ref, o_ref):
    ua = jnp.dot(ya_ref[...], wb0_ref[...], preferred_element_type=F32)
    ud = jnp.dot(yd_ref[...], wb1_ref[...], preferred_element_type=F32)
    merged = jax.nn.sigmoid(g0_ref[...].astype(F32)) * ua + jax.nn.sigmoid(g1_ref[...].astype(F32)) * ud
    o_ref[...] = x_ref[...] + jnp.dot(merged.astype(BF16), wo_ref[...], preferred_element_type=F32)


def _merge(ya, yd, proj, x2, wb, wo):
    t, d = x2.shape
    tm = min(MERGE_TM, t)
    bw = ya.shape[1]
    const = dict(pipeline_mode=pl.Buffered(1))
    return pl.pallas_call(
        _merge_kernel,
        grid=(t // tm,),
        in_specs=[
            pl.BlockSpec((tm, bw), lambda i: (i, 0)),
            pl.BlockSpec((tm, bw), lambda i: (i, 0)),
            pl.BlockSpec((tm, d), lambda i: (i, 0)),
            pl.BlockSpec((tm, d), lambda i: (i, 1)),
            pl.BlockSpec((tm, d), lambda i: (i, 0)),
            pl.BlockSpec((None, bw, d), lambda i: (0, 0, 0), **const),
            pl.BlockSpec((None, bw, d), lambda i: (1, 0, 0), **const),
            pl.BlockSpec((d, d), lambda i: (0, 0), **const),
        ],
        out_specs=pl.BlockSpec((tm, d), lambda i: (i, 0)),
        out_shape=jax.ShapeDtypeStruct((t, d), F32),
        compiler_params=_cparams(("parallel",)),
        name="merge",
    )(ya, yd, proj, proj, x2, wb, wb, wo)


def _memkv_kernel(mem_ref, g_ref, w_ref, k_ref, v_ref):
    mn = _rms(mem_ref[...], g_ref[...]).astype(BF16)
    kv = jnp.dot(mn, w_ref[...], preferred_element_type=F32)
    k_ref[...] = kv[:, :X_WIDTH].astype(k_ref.dtype)
    v_ref[...] = kv[:, X_WIDTH:].astype(v_ref.dtype)


def _memkv(mem, g, w):
    b, m, d = mem.shape
    out = jax.ShapeDtypeStruct((b, m, X_WIDTH), BF16)
    return pl.pallas_call(
        _memkv_kernel,
        grid=(b,),
        in_specs=[
            pl.BlockSpec((None, m, d), lambda i: (i, 0, 0)),
            pl.BlockSpec((1, d), lambda i: (0, 0)),
            pl.BlockSpec((d, 2 * X_WIDTH), lambda i: (0, 0)),
        ],
        out_specs=[pl.BlockSpec((None, m, X_WIDTH), lambda i: (i, 0, 0))] * 2,
        out_shape=[out, out],
        compiler_params=_cparams(("parallel",)),
        name="memkv",
    )(mem, g, w)


def _cross_router_kernel(h_ref, k_ref, v_ref, gc_ref, wq_ref, wo_ref, gf_ref, wr_ref, br_ref,
                         h2_ref, xn_ref, idx_ref, gate_ref, rank_ref, cnt_ref, run_ref):
    i = pl.program_id(0)
    tm, d = h_ref.shape

    @pl.when(i == 0)
    def _():
        run_ref[...] = jnp.zeros(run_ref.shape, F32)

    h = h_ref[...]
    hn = _rms(h, gc_ref[...]).astype(BF16)
    q = jnp.dot(hn, wq_ref[...], preferred_element_type=F32).astype(BF16)
    k = k_ref[...]
    v = v_ref[...]
    outs = []
    for hd in range(X_HEADS):
        sl = slice(X_HEAD_DIM * hd, X_HEAD_DIM * (hd + 1))
        s = lax.dot_general(q[:, sl], k[:, sl], (((1,), (1,)), ((), ())),
                            preferred_element_type=F32) * (X_HEAD_DIM ** -0.5)
        s = s - jnp.max(s, axis=-1, keepdims=True)
        p = jnp.exp(s)
        p = p / jnp.sum(p, axis=-1, keepdims=True)
        outs.append(jnp.dot(p.astype(BF16), v[:, sl], preferred_element_type=F32))
    o = jnp.concatenate(outs, axis=1).astype(BF16)
    h2 = h + jnp.dot(o, wo_ref[...], preferred_element_type=F32)
    h2_ref[...] = h2

    xn = _rms(h2, gf_ref[...])
    packed = _pack_pair(xn[:, :d // 2], xn[:, d // 2:])
    for j in range(V7X_SUBLANES):
        xn_ref[pl.ds(j, tm, stride=V7X_SUBLANES), :] = packed[:, j * V7X_LANES:(j + 1) * V7X_LANES]
    logits = jnp.dot(xn.astype(BF16), wr_ref[...], preferred_element_type=F32) + br_ref[...]

    lane = lax.broadcasted_iota(jnp.int32, logits.shape, 1).astype(F32)
    work = logits
    vals, idxs = [], []
    onehot = jnp.zeros(logits.shape, F32)
    for _ in range(TOP_K):
        mx = jnp.max(work, axis=-1, keepdims=True)
        ix = jnp.min(jnp.where(work == mx, lane, float(N_EXPERTS)), axis=-1, keepdims=True)
        sel = lane == ix
        vals.append(mx)
        idxs.append(ix)
        onehot = jnp.where(sel, 1.0, onehot)
        work = jnp.where(sel, NEG_INF, work)
    es = [jnp.exp(vv - vals[0]) for vv in vals]
    tot = es[0] + es[1] + es[2] + es[3]

    r = lax.broadcasted_iota(jnp.int32, (tm, tm), 0)
    c = lax.broadcasted_iota(jnp.int32, (tm, tm), 1)
    lower = jnp.where(c < r, 1.0, 0.0).astype(BF16)
    prefix = jnp.dot(lower, onehot.astype(BF16), preferred_element_type=F32) + run_ref[...]
    for kk in range(TOP_K):
        idx_ref[:, kk:kk + 1] = idxs[kk].astype(jnp.int32)
        gate_ref[:, kk:kk + 1] = es[kk] / tot
        rk = jnp.sum(jnp.where(lane == idxs[kk], prefix, 0.0), axis=-1, keepdims=True)
        rank_ref[:, kk:kk + 1] = rk.astype(jnp.int32)
    run_ref[...] = run_ref[...] + jnp.sum(onehot, axis=0, keepdims=True)
    cnt_ref[...] = run_ref[...].astype(jnp.int32)


def _cross_router(h1, kmem, vmem, gc, wq, wo, gf, wr, br, seq):
    t, d = h1.shape
    tm = min(CROSS_TM, seq)
    per_b = seq // tm
    m = kmem.shape[1]
    const = dict(pipeline_mode=pl.Buffered(1))
    outs = pl.pallas_call(
        _cross_router_kernel,
        grid=(t // tm,),
        in_specs=[
            pl.BlockSpec((tm, d), lambda i: (i, 0)),
            pl.BlockSpec((None, m, X_WIDTH), lambda i: (i // per_b, 0, 0)),
            pl.BlockSpec((None, m, X_WIDTH), lambda i: (i // per_b, 0, 0)),
            pl.BlockSpec((1, d), lambda i: (0, 0)),
            pl.BlockSpec((d, X_WIDTH), lambda i: (0, 0), **const),
            pl.BlockSpec((X_WIDTH, d), lambda i: (0, 0), **const),
            pl.BlockSpec((1, d), lambda i: (0, 0)),
            pl.BlockSpec((d, N_EXPERTS), lambda i: (0, 0), **const),
            pl.BlockSpec((1, N_EXPERTS), lambda i: (0, 0)),
        ],
        out_specs=[
            pl.BlockSpec((tm, d), lambda i: (i, 0)),
            pl.BlockSpec((tm * V7X_SUBLANES, V7X_LANES), lambda i: (i, 0)),
            pl.BlockSpec((tm, TOP_K), lambda i: (i, 0)),
            pl.BlockSpec((tm, TOP_K), lambda i: (i, 0)),
            pl.BlockSpec((tm, TOP_K), lambda i: (i, 0)),
            pl.BlockSpec((1, N_EXPERTS), lambda i: (0, 0)),
        ],
        out_shape=[
            jax.ShapeDtypeStruct((t, d), F32),
            jax.ShapeDtypeStruct((t * V7X_SUBLANES, V7X_LANES), U32),
            jax.ShapeDtypeStruct((t, TOP_K), jnp.int32),
            jax.ShapeDtypeStruct((t, TOP_K), F32),
            jax.ShapeDtypeStruct((t, TOP_K), jnp.int32),
            jax.ShapeDtypeStruct((1, N_EXPERTS), jnp.int32),
        ],
        scratch_shapes=[pltpu.VMEM((1, N_EXPERTS), F32)],
        compiler_params=_cparams(("arbitrary",)),
        name="cross_router",
    )(h1, kmem, vmem, gc, wq, wo, gf, wr, br)
    return outs


def _tile_rows(row):
    return pl.ds(pl.multiple_of(row * V7X_SUBLANES, V7X_SUBLANES), V7X_SUBLANES)


def _dispatch_kernel(pos_ref, tail_ref, x_ref, xs_ref, zero_ref, sem, zsem):
    i = pl.program_id(0)
    tm = x_ref.shape[0] // V7X_SUBLANES
    chunk_rows = zero_ref.shape[0]

    @pl.when(i == 0)
    def _():
        zero_ref[...] = jnp.zeros(zero_ref.shape, zero_ref.dtype)

        def tail_copy(e):
            row0 = pl.multiple_of(tail_ref[e] * V7X_SUBLANES, V7X_SUBLANES)
            return pltpu.make_async_copy(zero_ref, xs_ref.at[pl.ds(row0, chunk_rows)], zsem)

        for e in range(N_EXPERTS):
            @pl.when(tail_ref[e] >= 0)
            def _():
                tail_copy(e).start()
        for e in range(N_EXPERTS):
            @pl.when(tail_ref[e] >= 0)
            def _():
                tail_copy(e).wait()

    def row_copy(r, p):
        return pltpu.make_async_copy(x_ref.at[_tile_rows(r)], xs_ref.at[_tile_rows(p)], sem)

    def issue(r, carry):
        for kk in range(TOP_K):
            row_copy(r, pos_ref[0, 0, r * TOP_K + kk]).start(priority=kk % 2)
        return carry

    lax.fori_loop(0, tm, issue, 0, unroll=DMA_ISSUE_UNROLL)
    for _ in range(tm * TOP_K):
        row_copy(0, 0).wait()


def _dispatch(xn, pos, tails, n_rows):
    t = xn.shape[0] // V7X_SUBLANES
    tm = min(DISPATCH_TM, t)
    pos3 = pos.reshape(t // tm, 1, tm * TOP_K)
    return pl.pallas_call(
        _dispatch_kernel,
        grid=(t // tm,),
        in_specs=[
            pl.BlockSpec((1, 1, tm * TOP_K), lambda i: (i, 0, 0), memory_space=pltpu.SMEM),
            pl.BlockSpec(memory_space=pltpu.SMEM),
            pl.BlockSpec((tm * V7X_SUBLANES, V7X_LANES), lambda i: (i, 0)),
        ],
        out_specs=pl.BlockSpec(memory_space=pl.ANY),
        out_shape=jax.ShapeDtypeStruct((n_rows * V7X_SUBLANES, V7X_LANES), xn.dtype),
        scratch_shapes=[pltpu.VMEM((EXPERT_CHUNK * V7X_SUBLANES, V7X_LANES), xn.dtype),
                        pltpu.SemaphoreType.DMA(()), pltpu.SemaphoreType.DMA(())],
        compiler_params=_cparams(("arbitrary",)),
        name="dispatch",
    )(pos3, tails, xn)


def _expert_kernel(ve_ref, vr_ref, vb_ref, xs_ref, wg_ref, wl_ref, bg_ref, bl_ref, wd_ref, bd_ref, o_ref,
                   x_ref, act_ref, wa_ref, wb_ref, xsem):
    del ve_ref
    v = pl.program_id(0)
    s = pl.program_id(1)
    rows = vr_ref[v]
    nj = act_ref.shape[0]
    tf = act_ref.shape[2]
    ch = EXPERT_CHUNK
    n_chunks = x_ref.shape[0] // (ch * V7X_SUBLANES)
    half = o_ref.shape[1]

    def x_copy(visit):
        row0 = pl.multiple_of(vb_ref[visit] * x_ref.shape[0], V7X_SUBLANES)
        return pltpu.make_async_copy(xs_ref.at[pl.ds(row0, x_ref.shape[0])], x_ref, xsem)

    @pl.when((v == 0) & (s == 0))
    def _():
        x_copy(0).start()

    @pl.when(s == 0)
    def _():
        x_copy(v).wait()

    @pl.when((s == nj) & (v + 1 < pl.num_programs(0)))
    def _():
        x_copy(v + 1).start()

    @pl.when((rows > 0) & (s < nj))
    def _():
        wa_ref[...] = wg_ref[...].astype(BF16)
        wb_ref[...] = wl_ref[...].astype(BF16)
        for c in range(n_chunks):
            @pl.when(c * ch < rows)
            def _():
                xw = [x_ref[pl.ds(c * ch * V7X_SUBLANES + j, ch, stride=V7X_SUBLANES), :]
                      for j in range(V7X_SUBLANES)]
                xb = jnp.concatenate([_unpack_lo(u).astype(BF16) for u in xw]
                                     + [_unpack_hi(u).astype(BF16) for u in xw], axis=1)
                glu = jnp.dot(xb, wa_ref[...], preferred_element_type=F32) + bg_ref[...]
                lin = jnp.dot(xb, wb_ref[...], preferred_element_type=F32) + bl_ref[...]
                glu = jnp.minimum(glu, SWIGLU_LIMIT)
                lin = jnp.clip(lin, -SWIGLU_LIMIT, SWIGLU_LIMIT)
                act = glu * jax.nn.sigmoid(SWIGLU_ALPHA * glu) * (lin + 1.0)
                act_ref[s, pl.ds(c * ch, ch), :] = act.astype(BF16)

    @pl.when((rows > 0) & (s >= nj))
    def _():
        wa_ref[...] = wd_ref[...].astype(BF16)
        for c in range(n_chunks):
            sl = pl.ds(c * ch, ch)

            @pl.when(c * ch < rows)
            def _():
                acc = jnp.dot(act_ref[0, sl, :], wa_ref[0:tf, :], preferred_element_type=F32)
                for j in range(1, nj):
                    acc = acc + jnp.dot(act_ref[j, sl, :], wa_ref[j * tf:(j + 1) * tf, :],
                                        preferred_element_type=F32)
                acc = acc + bd_ref[...]
                o_ref[sl, :] = _pack_pair(acc[:, :half], acc[:, half:])

            @pl.when(c * ch >= rows)
            def _():
                o_ref[sl, :] = jnp.zeros((ch, half), o_ref.dtype)


def _experts(xs, visit_expert, visit_rows, visit_blk, n_used, w_gu, b_gu, w_down, b_down):
    n_rows = xs.shape[0] // V7X_SUBLANES
    e, d, ff2 = w_gu.shape
    ff = ff2 // 2
    r, tf = EXPERT_ROWS, EXPERT_TF
    nj = ff // tf
    nn = d // tf
    b_gu3 = b_gu.reshape(e, 1, ff2)
    b_down3 = b_down.reshape(e, 1, d)

    def up(v, s, vr):
        return jnp.where(vr[v] > 0, jnp.minimum(s, nj - 1), nj - 1)

    def down(v, s, vr):
        return jnp.where(vr[v] > 0, jnp.maximum(s - nj, 0), nn - 1)

    grid_spec = pltpu.PrefetchScalarGridSpec(
        num_scalar_prefetch=3,
        grid=(n_used, nj + nn),
        in_specs=[
            pl.BlockSpec(memory_space=pl.ANY),
            pl.BlockSpec((None, d, tf), lambda v, s, ve, vr, vb: (ve[v], 0, up(v, s, vr))),
            pl.BlockSpec((None, d, tf), lambda v, s, ve, vr, vb: (ve[v], 0, nj + up(v, s, vr))),
            pl.BlockSpec((None, 1, tf), lambda v, s, ve, vr, vb: (ve[v], 0, up(v, s, vr))),
            pl.BlockSpec((None, 1, tf), lambda v, s, ve, vr, vb: (ve[v], 0, nj + up(v, s, vr))),
            pl.BlockSpec((None, ff, tf), lambda v, s, ve, vr, vb: (ve[v], 0, down(v, s, vr))),
            pl.BlockSpec((None, 1, tf), lambda v, s, ve, vr, vb: (ve[v], 0, down(v, s, vr))),
        ],
        out_specs=pl.BlockSpec((r, tf // 2), lambda v, s, ve, vr, vb: (vb[v], down(v, s, vr))),
        scratch_shapes=[
            pltpu.VMEM((r * V7X_SUBLANES, V7X_LANES), xs.dtype),
            pltpu.VMEM((nj, r, tf), BF16),
            pltpu.VMEM((d, tf), BF16),
            pltpu.VMEM((d, tf), BF16),
            pltpu.SemaphoreType.DMA(()),
        ],
    )
    return pl.pallas_call(
        _expert_kernel,
        grid_spec=grid_spec,
        out_shape=jax.ShapeDtypeStruct((n_rows, d // 2), U32),
        compiler_params=_cparams(("arbitrary", "arbitrary")),
        name="experts",
    )(visit_expert, visit_rows, visit_blk, xs, w_gu, w_gu, b_gu3, b_gu3, w_down, b_down3)


def _combine_kernel(pos_ref, h_ref, gate_ref, g_ref, ys_ref, o_ref, buf_ref, sem):
    tm = h_ref.shape[0]
    half = EXPERT_TF // 2

    def row_copy(r, kk, p):
        return pltpu.make_async_copy(ys_ref.at[pl.ds(p, 1)], buf_ref.at[kk, pl.ds(r, 1)], sem)

    def issue(r, carry):
        for kk in range(TOP_K):
            row_copy(r, kk, pos_ref[0, 0, r * TOP_K + kk]).start(priority=kk % 2)
        return carry

    lax.fori_loop(0, tm, issue, 0, unroll=DMA_ISSUE_UNROLL)
    for _ in range(tm * TOP_K):
        row_copy(0, 0, 0).wait()

    gates = gate_ref[...]
    y = h_ref[...]
    for kk in range(TOP_K):
        w = buf_ref[kk]
        lo = _unpack_lo(w)
        hi = _unpack_hi(w)
        pieces = []
        for n in range(w.shape[1] // half):
            pieces += [lo[:, n * half:(n + 1) * half], hi[:, n * half:(n + 1) * half]]
        y = y + gates[:, kk:kk + 1] * jnp.concatenate(pieces, axis=1)
    o_ref[...] = _rms(y, g_ref[...])


def _combine(h2, gates, pos, ys, g_final):
    t, d = h2.shape
    tm = min(COMBINE_TM, t)
    pos3 = pos.reshape(t // tm, 1, tm * TOP_K)
    return pl.pallas_call(
        _combine_kernel,
        grid=(t // tm,),
        in_specs=[
            pl.BlockSpec((1, 1, tm * TOP_K), lambda i: (i, 0, 0), memory_space=pltpu.SMEM),
            pl.BlockSpec((tm, d), lambda i: (i, 0)),
            pl.BlockSpec((tm, TOP_K), lambda i: (i, 0)),
            pl.BlockSpec((1, d), lambda i: (0, 0)),
            pl.BlockSpec(memory_space=pl.ANY),
        ],
        out_specs=pl.BlockSpec((tm, d), lambda i: (i, 0)),
        out_shape=jax.ShapeDtypeStruct((t, d), F32),
        scratch_shapes=[pltpu.VMEM((TOP_K, tm, ys.shape[1]), ys.dtype), pltpu.SemaphoreType.DMA(())],
        compiler_params=_cparams(("arbitrary",)),
        name="combine",
    )(pos3, h2, gates, g_final, ys)


def _routing_tables(counts, top_idx, rank, n_visits):
    r, ch = EXPERT_ROWS, EXPERT_CHUNK
    counts = counts.reshape(N_EXPERTS)
    padded = (counts + r - 1) // r * r
    ends = jnp.cumsum(padded)
    starts = ends - padded
    experts = jnp.arange(N_EXPERTS, dtype=jnp.int32)
    start_of = jnp.sum(jnp.where(top_idx[..., None] == experts, starts, 0), axis=-1)
    pos = start_of + rank
    visit_row0 = jnp.arange(n_visits, dtype=jnp.int32) * r
    owner = jnp.sum((ends[None, :] <= visit_row0[:, None]).astype(jnp.int32), axis=1)
    visit_expert = jnp.minimum(owner, N_EXPERTS - 1).astype(jnp.int32)
    visit_rows = jnp.clip(counts[visit_expert] - (visit_row0 - starts[visit_expert]), 0, r).astype(jnp.int32)
    n_used = (ends[-1] // r).astype(jnp.int32)
    last = jnp.maximum(n_used - 1, 0)
    ids = jnp.arange(n_visits, dtype=jnp.int32)
    active = ids < n_used
    visit_rows = jnp.where(active, visit_rows, 0)
    visit_blk = jnp.where(active, ids, last)
    visit_expert = jnp.where(active, visit_expert, visit_expert[last])
    tail = starts + jnp.minimum(counts // ch * ch, padded - ch)
    tail = jnp.where(counts > 0, tail, -1)
    return pos.astype(jnp.int32), visit_expert, visit_rows, visit_blk, tail.astype(jnp.int32), n_used


def kernel(x, mem, norm_mix_g, w_in, sinks, lambda_q1, lambda_k1, lambda_q2, lambda_k2, diff_subln_g, w_branch, w_o, norm_cross_g, norm_mem_g, w_cq, w_ckv, w_co, norm_ffn_g, w_router, b_router, w_gate_up, b_gate_up, w_down, b_down, norm_final_g):
    b, s, d = x.shape
    t = b * s
    x2 = x.reshape(t, d)
    l = 0

    w = w_in[l]
    o_qa, o_ka, o_va = 0, SWA_WIDTH, SWA_WIDTH + SWA_KV_WIDTH
    o_qd = o_va + SWA_KV_WIDTH
    o_kd, o_vd, o_g = o_qd + DIFF_WIDTH, o_qd + 2 * DIFF_WIDTH, o_qd + 3 * DIFF_WIDTH
    w_perm = jnp.concatenate([
        w[:, o_g:], w[:, o_qd:o_kd] * jnp.float32(HEAD_DIM ** -0.5 * LOG2E), w[:, o_kd:o_vd], w[:, o_vd:o_g],
        w[:, o_qa:o_ka], w[:, o_ka:o_va], w[:, o_va:o_qd]], axis=1).astype(BF16)
    proj = _inproj(x2, norm_mix_g[l].reshape(1, d), w_perm)
    c_qd = 2 * d
    c_kd, c_vd, c_qa = c_qd + DIFF_WIDTH, c_qd + 2 * DIFF_WIDTH, c_qd + 3 * DIFF_WIDTH
    c_ka, c_va = c_qa + SWA_WIDTH, c_qa + SWA_WIDTH + SWA_KV_WIDTH

    def seg(c0, width):
        return proj[:, c0:c0 + width].reshape(b, s, width)

    qaT = jnp.swapaxes(seg(c_qa, SWA_WIDTH), 1, 2)
    vaT = jnp.swapaxes(seg(c_va, SWA_KV_WIDTH), 1, 2)
    yaT = _swa(qaT, proj.reshape(b, s, proj.shape[1]), c_ka, vaT, sinks[l])
    ya = jnp.swapaxes(yaT, 1, 2).reshape(t, SWA_WIDTH)

    td = min(DIFF_T, s)
    nk = s // td
    hw = DIFF_HEAD_WIDTH
    cs, qx, kx = _diff_tables(td)
    qdT = jnp.swapaxes(seg(c_qd, DIFF_WIDTH), 1, 2)
    k4 = proj.reshape(b, nk, td, proj.shape[1])
    v5 = seg(c_vd, DIFF_WIDTH).reshape(b, nk, td, DIFF_HEADS, hw).transpose(0, 3, 1, 4, 2)
    ones = jnp.zeros((DIFF_DENOM_ROWS, td), BF16).at[0].set(1.0)
    lamv = jnp.stack([lambda_q1[l], lambda_k1[l], lambda_q2[l], lambda_k2[l]]).astype(F32)
    ydT = _diff(qdT, k4, c_kd, v5, qx, kx, ones, cs, lamv, diff_subln_g[l].astype(F32).reshape(hw, 1))
    yd = jnp.swapaxes(ydT, 1, 2).reshape(t, DIFF_WIDTH)

    h1 = _merge(ya, yd, proj, x2, w_branch[l].astype(BF16), w_o[l].astype(BF16))

    kmem, vmem = _memkv(mem, norm_mem_g[l].reshape(1, d), w_ckv[l].astype(BF16))
    h2, xn, top_idx, gates, rank, counts = _cross_router(
        h1, kmem, vmem, norm_cross_g[l].reshape(1, d), w_cq[l].astype(BF16), w_co[l].astype(BF16),
        norm_ffn_g[l].reshape(1, d), w_router[l].astype(BF16), b_router[l].reshape(1, N_EXPERTS), s)

    assert d // 2 == V7X_SUBLANES * V7X_LANES
    n_visits = (t * TOP_K) // EXPERT_ROWS + N_EXPERTS
    n_rows = n_visits * EXPERT_ROWS
    pos, visit_expert, visit_rows, visit_blk, tails, n_used = _routing_tables(counts, top_idx, rank, n_visits)
    xs = _dispatch(xn, pos, tails, n_rows)
    ys = _experts(xs, visit_expert, visit_rows, visit_blk, n_used, w_gate_up[l], b_gate_up[l], w_down[l], b_down[l])
    out = _combine(h2, gates, pos, ys, norm_final_g.reshape(1, d))
    return out.reshape(b, s, d)
```

```python
import functools
import math

import jax
import jax.numpy as jnp
from jax import lax
from jax.experimental import pallas as pl
from jax.experimental.pallas import tpu as pltpu

BF16 = jnp.bfloat16
F32 = jnp.float32
U32 = jnp.uint32
NEG_INF = float("-inf")
LOG2E = math.log2(math.e)

HEAD_DIM = 64
SWA_Q_HEADS = 16
SWA_KV_HEADS = 4
SWA_GROUP = SWA_Q_HEADS // SWA_KV_HEADS
SWA_BLOCK = 128
SWA_WIDTH = SWA_Q_HEADS * HEAD_DIM
SWA_KV_WIDTH = SWA_KV_HEADS * HEAD_DIM
DIFF_HEADS = 8
DIFF_HEAD_WIDTH = 2 * HEAD_DIM
DIFF_WIDTH = DIFF_HEADS * DIFF_HEAD_WIDTH
X_HEADS = 4
X_HEAD_DIM = 128
X_WIDTH = X_HEADS * X_HEAD_DIM
N_EXPERTS = 32
TOP_K = 4
SWIGLU_LIMIT = 7.0
SWIGLU_ALPHA = 1.702
RMS_EPS = 1e-5
LAMBDA_INIT = 0.8 - 0.6 * math.exp(-0.3 * 0)

V7X_VMEM_BYTES = 64 * 1024 * 1024
VMEM_LIMIT = V7X_VMEM_BYTES - 8 * 1024 * 1024
EXPERT_VMEM_LIMIT = V7X_VMEM_BYTES - 3 * 1024 * 1024
V7X_LANES = 128
V7X_SUBLANES = 8

INPROJ_TM = 512
INPROJ_TN = 2176
DIFF_T = 512
DIFF_POS_SPLIT = 256
DIFF_DENOM_ROWS = 16
MERGE_TM = 256
CROSS_TM = 256
DISPATCH_TM = 256
EXPERT_CHUNK = 768
EXPERT_ROWS = 3 * EXPERT_CHUNK
EXPERT_TF = 512
COMBINE_TM = 256
DMA_ISSUE_UNROLL = 8


def _cparams(semantics):
    return pltpu.CompilerParams(dimension_semantics=semantics, vmem_limit_bytes=VMEM_LIMIT)


def _rms(x, g):
    ms = jnp.mean(x * x, axis=-1, keepdims=True)
    return x * lax.rsqrt(ms + RMS_EPS) * g


def _pack_pair(lo, hi):
    lo_bits = lax.bitcast_convert_type(lo.astype(BF16).astype(F32), U32)
    hi_bits = lax.bitcast_convert_type(hi.astype(BF16).astype(F32), U32)
    return hi_bits | (lo_bits >> 16)


def _unpack_lo(w):
    return lax.bitcast_convert_type(w << 16, F32)


def _unpack_hi(w):
    return lax.bitcast_convert_type(w & jnp.uint32(0xFFFF0000), F32)


def _inproj_kernel(x_ref, g_ref, w_ref, o_ref, xn_ref):
    @pl.when(pl.program_id(1) == 0)
    def _():
        xn_ref[...] = _rms(x_ref[...], g_ref[...]).astype(BF16)

    o_ref[...] = jnp.dot(xn_ref[...], w_ref[...], preferred_element_type=F32).astype(o_ref.dtype)


def _inproj(x2, g, w):
    t, d = x2.shape
    n = w.shape[1]
    tm, tn = min(INPROJ_TM, t), INPROJ_TN
    return pl.pallas_call(
        _inproj_kernel,
        grid=(t // tm, n // tn),
        in_specs=[
            pl.BlockSpec((tm, d), lambda i, j: (i, 0)),
            pl.BlockSpec((1, d), lambda i, j: (0, 0)),
            pl.BlockSpec((d, tn), lambda i, j: (0, j)),
        ],
        out_specs=pl.BlockSpec((tm, tn), lambda i, j: (i, j)),
        out_shape=jax.ShapeDtypeStruct((t, n), BF16),
        scratch_shapes=[pltpu.VMEM((tm, d), BF16)],
        compiler_params=_cparams(("parallel", "arbitrary")),
        name="inproj",
    )(x2, g, w)


def _swa_kernel(qT_ref, kp_ref, kc_ref, vp_ref, vc_ref, bias_ref, sink_ref, o_ref):
    n = pl.program_id(1)
    blk = SWA_BLOCK
    kband = jnp.concatenate([kp_ref[...], kc_ref[...]], axis=0)
    vband = jnp.concatenate([vp_ref[...], vc_ref[...]], axis=1)
    krow = lax.broadcasted_iota(jnp.int32, (2 * blk, SWA_GROUP * blk), 0)
    has_prev = n > 0
    zeros = jnp.zeros((HEAD_DIM, blk), BF16)
    for h in range(SWA_KV_HEADS):
        lane0 = 2 * HEAD_DIM * (h // 2)
        k128 = kband[:, lane0:lane0 + 2 * HEAD_DIM]
        pieces = []
        for g in range(SWA_GROUP):
            hq = SWA_GROUP * h + g
            qh = qT_ref[HEAD_DIM * hq:HEAD_DIM * (hq + 1), :]
            pieces.append(jnp.concatenate([qh, zeros] if h % 2 == 0 else [zeros, qh], axis=0))
        qz = jnp.concatenate(pieces, axis=1)
        s = jnp.dot(k128, qz, preferred_element_type=F32) * (HEAD_DIM ** -0.5)
        s = s + bias_ref[h]
        s = jnp.where((krow >= blk) | has_prev, s, NEG_INF)
        sink = sink_ref[h]
        m = jnp.maximum(jnp.max(s, axis=0, keepdims=True), sink)
        p = jnp.exp(s - m)
        denom = jnp.sum(p, axis=0, keepdims=True) + jnp.exp(sink - m)
        vh = vband[HEAD_DIM * h:HEAD_DIM * (h + 1), :]
        o = jnp.dot(vh, p.astype(BF16), preferred_element_type=F32) / denom
        for g in range(SWA_GROUP):
            hq = SWA_GROUP * h + g
            o_ref[HEAD_DIM * hq:HEAD_DIM * (hq + 1), :] = o[:, blk * g:blk * (g + 1)].astype(o_ref.dtype)


def _swa_tables(sinks):
    blk = SWA_BLOCK
    slopes = (2.0 ** (-8.0 * jnp.arange(1, SWA_Q_HEADS + 1, dtype=F32) / SWA_Q_HEADS)).reshape(SWA_KV_HEADS, SWA_GROUP)
    q_pos = jnp.arange(blk)[None, :] + blk
    k_pos = jnp.arange(2 * blk)[:, None]
    dist = q_pos - k_pos
    allowed = (dist >= 0) & (dist < SWA_BLOCK)
    bias = -slopes[:, :, None, None] * dist.astype(F32)[None, None]
    bias = jnp.where(allowed[None, None], bias, NEG_INF)
    bias = jnp.transpose(bias, (0, 2, 1, 3)).reshape(SWA_KV_HEADS, 2 * blk, SWA_GROUP * blk)
    sink = sinks.astype(F32).reshape(SWA_KV_HEADS, SWA_GROUP, 1)
    sink = jnp.broadcast_to(sink, (SWA_KV_HEADS, SWA_GROUP, blk)).reshape(SWA_KV_HEADS, 1, SWA_GROUP * blk)
    return bias, sink


def _swa(qT, k, k_col0, vT, sinks):
    b, _, s = qT.shape
    blk = SWA_BLOCK
    bias, sink = _swa_tables(sinks)
    prev = lambda n: jnp.maximum(n - 1, 0)
    kb = k_col0 // SWA_KV_WIDTH
    return pl.pallas_call(
        _swa_kernel,
        grid=(b, s // blk),
        in_specs=[
            pl.BlockSpec((None, SWA_WIDTH, blk), lambda bi, n: (bi, 0, n)),
            pl.BlockSpec((None, blk, SWA_KV_WIDTH), lambda bi, n: (bi, prev(n), kb)),
            pl.BlockSpec((None, blk, SWA_KV_WIDTH), lambda bi, n: (bi, n, kb)),
            pl.BlockSpec((None, SWA_KV_WIDTH, blk), lambda bi, n: (bi, 0, prev(n))),
            pl.BlockSpec((None, SWA_KV_WIDTH, blk), lambda bi, n: (bi, 0, n)),
            pl.BlockSpec(bias.shape, lambda bi, n: (0, 0, 0)),
            pl.BlockSpec(sink.shape, lambda bi, n: (0, 0, 0)),
        ],
        out_specs=pl.BlockSpec((None, SWA_WIDTH, blk), lambda bi, n: (bi, 0, n)),
        out_shape=jax.ShapeDtypeStruct((b, SWA_WIDTH, s), BF16),
        compiler_params=_cparams(("parallel", "arbitrary")),
        name="swa",
    )(qT, k, k, vT, vT, bias, sink)


def _diff_kernel(c_ref, qT_ref, qx_ref, k_ref, kx_ref, v_ref, ones_ref, lamv_ref, g_ref, o_ref,
                 q2_ref, z_ref, zm_ref, p_ref, a_ref, m_ref, acc_ref, *, t):
    h = pl.program_id(1)
    qi = pl.program_id(2)
    w = DIFF_HEAD_WIDTH
    c = c_ref[h]

    @pl.when(qi == 0)
    def _():
        q2_ref[w:, :] = qx_ref[...]

    qt = qT_ref[...]
    row = lax.broadcasted_iota(jnp.int32, qt.shape, 0)
    zero = jnp.zeros_like(qt)
    q2_ref[:w, :t] = jnp.where(row < HEAD_DIM, qt, zero)
    q2_ref[:w, t:] = jnp.where(row >= HEAD_DIM, qt, zero)
    m_ref[...] = jnp.full(m_ref.shape, NEG_INF, F32)
    acc_ref[...] = jnp.zeros(acc_ref.shape, F32)
    p_ref[1] = jnp.zeros(p_ref.shape[1:], BF16)
    a_ref[1] = jnp.ones(a_ref.shape[1:], F32)

    def scores(k, slot):
        keys = jnp.concatenate([k_ref[k], kx_ref[...]], axis=1)
        z = jnp.dot(keys, q2_ref[...], preferred_element_type=F32)
        z_ref[slot] = z
        zm_ref[slot] = jnp.max(z, axis=0, keepdims=True)

    def values(k, slot):
        vals = jnp.concatenate([v_ref[jnp.maximum(k, 0)], ones_ref[...]], axis=0)
        pv = jnp.dot(vals, p_ref[slot], preferred_element_type=F32)
        acc_ref[...] = a_ref[slot] * acc_ref[...] + pv

    def softmax(k, slot, masked):
        z = z_ref[slot]
        if masked:
            kr = lax.broadcasted_iota(jnp.int32, z.shape, 0)
            qc = lax.broadcasted_iota(jnp.int32, z.shape, 1)
            qc = jnp.where(qc >= t, qc - t, qc)
            z = jnp.where(kr <= qc, z, NEG_INF)
            zmax = jnp.max(z, axis=0, keepdims=True)
        else:
            zmax = zm_ref[slot]
        off = c * ((k - qi) * t).astype(F32)
        m_old = m_ref[...]
        m_new = jnp.maximum(m_old, zmax + off)
        p_ref[slot] = jnp.exp2(z - (m_new - off)).astype(BF16)
        a_ref[slot] = jnp.exp2(m_old - m_new)
        m_ref[...] = m_new

    scores(0, 0)

    def pair(j, carry):
        k = 2 * j
        scores(k + 1, 1)
        values(k - 1, 1)
        softmax(k, 0, False)
        scores(k + 2, 0)
        values(k, 0)
        softmax(k + 1, 1, False)
        return carry

    lax.fori_loop(0, qi >> 1, pair, 0)

    @pl.when((qi & 1) == 0)
    def _():
        values(qi - 1, 1)
        softmax(qi, 0, True)
        values(qi, 0)

    @pl.when((qi & 1) == 1)
    def _():
        scores(qi, 1)
        values(qi - 2, 1)
        softmax(qi - 1, 0, False)
        values(qi - 1, 0)
        softmax(qi, 1, True)
        values(qi, 1)

    lamv = lamv_ref[...]
    lam = (jnp.exp(jnp.sum(lamv[0:1] * lamv[1:2], axis=-1, keepdims=True))
           - jnp.exp(jnp.sum(lamv[2:3] * lamv[3:4], axis=-1, keepdims=True)) + LAMBDA_INIT)
    acc = acc_ref[...]
    l = acc[w:w + 1, :]
    o = acc[:w, :t] / l[:, :t] - lam * (acc[:w, t:] / l[:, t:])
    ms = jnp.mean(o * o, axis=0, keepdims=True)
    y = o * lax.rsqrt(ms + RMS_EPS) * g_ref[...] * (1.0 - LAMBDA_INIT)
    o_ref[...] = y.astype(o_ref.dtype)


def _diff_tables(t):
    w = DIFF_HEAD_WIDTH
    slopes = 2.0 ** (-8.0 * jnp.arange(1, DIFF_HEADS + 1, dtype=F32) / DIFF_HEADS)
    c = slopes * jnp.float32(LOG2E)
    c1 = c.astype(BF16)
    c2 = (c - c1.astype(F32)).astype(BF16)
    c3 = (c - c1.astype(F32) - c2.astype(F32)).astype(BF16)
    parts = jnp.stack([c1, c2, c3, c1, c2, c3], axis=1)
    qx = jnp.zeros((DIFF_HEADS, w, 2 * t), BF16).at[:, :6, :].set(parts[:, :, None])
    r = jnp.arange(t)
    r_lo = (r % DIFF_POS_SPLIT).astype(BF16)
    r_hi = ((r // DIFF_POS_SPLIT) * DIFF_POS_SPLIT).astype(BF16)
    kx = jnp.zeros((t, w), BF16).at[:, 0:3].set(r_lo[:, None]).at[:, 3:6].set(r_hi[:, None])
    return c, qx, kx


def _diff(qT, k4, k_col0, v5, qx, kx, ones, cs, lamv, subln_g):
    b, _, s = qT.shape
    t = k4.shape[2]
    nk = s // t
    w = DIFF_HEAD_WIDTH
    wv = w + ones.shape[0]
    k_blk0 = k_col0 // w
    return pl.pallas_call(
        functools.partial(_diff_kernel, t=t),
        grid=(b, DIFF_HEADS, nk),
        in_specs=[
            pl.BlockSpec(memory_space=pltpu.SMEM),
            pl.BlockSpec((None, w, t), lambda bi, h, qi: (bi, h, qi)),
            pl.BlockSpec((None, w, 2 * t), lambda bi, h, qi: (h, 0, 0)),
            pl.BlockSpec((None, nk, t, w), lambda bi, h, qi: (bi, 0, 0, k_blk0 + h)),
            pl.BlockSpec((t, w), lambda bi, h, qi: (0, 0)),
            pl.BlockSpec((None, None, nk, w, t), lambda bi, h, qi: (bi, h, 0, 0, 0)),
            pl.BlockSpec(ones.shape, lambda bi, h, qi: (0, 0)),
            pl.BlockSpec((4, HEAD_DIM), lambda bi, h, qi: (0, 0)),
            pl.BlockSpec((w, 1), lambda bi, h, qi: (0, 0)),
        ],
        out_specs=pl.BlockSpec((None, w, t), lambda bi, h, qi: (bi, h, qi)),
        out_shape=jax.ShapeDtypeStruct((b, DIFF_WIDTH, s), BF16),
        scratch_shapes=[
            pltpu.VMEM((2 * w, 2 * t), BF16),
            pltpu.VMEM((2, t, 2 * t), F32),
            pltpu.VMEM((2, 1, 2 * t), F32),
            pltpu.VMEM((2, t, 2 * t), BF16),
            pltpu.VMEM((2, 1, 2 * t), F32),
            pltpu.VMEM((1, 2 * t), F32),
            pltpu.VMEM((wv, 2 * t), F32),
        ],
        compiler_params=_cparams(("arbitrary", "arbitrary", "arbitrary")),
        name="diffattn",
    )(cs, qT, qx, k4, kx, v5, ones, lamv, subln_g)


def _merge_kernel(ya_ref, yd_ref, g0_ref, g1_ref, x_ref, wb0_ref, wb1_ref, wo_ref, o_ref):
    ua = jnp.dot(ya_ref[...], wb0_ref[...], preferred_element_type=F32)
    ud = jnp.dot(yd_ref[...], wb1_ref[...], preferred_element_type=F32)
    merged = jax.nn.sigmoid(g0_ref[...].astype(F32)) * ua + jax.nn.sigmoid(g1_ref[...].astype(F32)) * ud
    o_ref[...] = x_ref[...] + jnp.dot(merged.astype(BF16), wo_ref[...], preferred_element_type=F32)


def _merge(ya, yd, proj, x2, wb, wo):
    t, d = x2.shape
    tm = min(MERGE_TM, t)
    bw = ya.shape[1]
    const = dict(pipeline_mode=pl.Buffered(1))
    return pl.pallas_call(
        _merge_kernel,
        grid=(t // tm,),
        in_specs=[
            pl.BlockSpec((tm, bw), lambda i: (i, 0)),
            pl.BlockSpec((tm, bw), lambda i: (i, 0)),
            pl.BlockSpec((tm, d), lambda i: (i, 0)),
            pl.BlockSpec((tm, d), lambda i: (i, 1)),
            pl.BlockSpec((tm, d), lambda i: (i, 0)),
            pl.BlockSpec((None, bw, d), lambda i: (0, 0, 0), **const),
            pl.BlockSpec((None, bw, d), lambda i: (1, 0, 0), **const),
            pl.BlockSpec((d, d), lambda i: (0, 0), **const),
        ],
        out_specs=pl.BlockSpec((tm, d), lambda i: (i, 0)),
        out_shape=jax.ShapeDtypeStruct((t, d), F32),
        compiler_params=_cparams(("parallel",)),
        name="merge",
    )(ya, yd, proj, proj, x2, wb, wb, wo)


def _memkv_kernel(mem_ref, g_ref, w_ref, k_ref, v_ref):
    mn = _rms(mem_ref[...], g_ref[...]).astype(BF16)
    kv = jnp.dot(mn, w_ref[...], preferred_element_type=F32)
    k_ref[...] = kv[:, :X_WIDTH].astype(k_ref.dtype)
    v_ref[...] = kv[:, X_WIDTH:].astype(v_ref.dtype)


def _memkv(mem, g, w):
    b, m, d = mem.shape
    out = jax.ShapeDtypeStruct((b, m, X_WIDTH), BF16)
    return pl.pallas_call(
        _memkv_kernel,
        grid=(b,),
        in_specs=[
            pl.BlockSpec((None, m, d), lambda i: (i, 0, 0)),
            pl.BlockSpec((1, d), lambda i: (0, 0)),
            pl.BlockSpec((d, 2 * X_WIDTH), lambda i: (0, 0)),
        ],
        out_specs=[pl.BlockSpec((None, m, X_WIDTH), lambda i: (i, 0, 0))] * 2,
        out_shape=[out, out],
        compiler_params=_cparams(("parallel",)),
        name="memkv",
    )(mem, g, w)


def _cross_router_kernel(h_ref, k_ref, v_ref, gc_ref, wq_ref, wo_ref, gf_ref, wr_ref, br_ref,
                         h2_ref, xn_ref, idx_ref, gate_ref, rank_ref, cnt_ref, run_ref):
    i = pl.program_id(0)
    tm, d = h_ref.shape

    @pl.when(i == 0)
    def _():
        run_ref[...] = jnp.zeros(run_ref.shape, F32)

    h = h_ref[...]
    hn = _rms(h, gc_ref[...]).astype(BF16)
    q = jnp.dot(hn, wq_ref[...], preferred_element_type=F32).astype(BF16)
    k = k_ref[...]
    v = v_ref[...]
    outs = []
    for hd in range(X_HEADS):
        sl = slice(X_HEAD_DIM * hd, X_HEAD_DIM * (hd + 1))
        s = lax.dot_general(q[:, sl], k[:, sl], (((1,), (1,)), ((), ())),
                            preferred_element_type=F32) * (X_HEAD_DIM ** -0.5)
        s = s - jnp.max(s, axis=-1, keepdims=True)
        p = jnp.exp(s)
        p = p / jnp.sum(p, axis=-1, keepdims=True)
        outs.append(jnp.dot(p.astype(BF16), v[:, sl], preferred_element_type=F32))
    o = jnp.concatenate(outs, axis=1).astype(BF16)
    h2 = h + jnp.dot(o, wo_ref[...], preferred_element_type=F32)
    h2_ref[...] = h2

    xn = _rms(h2, gf_ref[...])
    packed = _pack_pair(xn[:, :d // 2], xn[:, d // 2:])
    for j in range(V7X_SUBLANES):
        xn_ref[pl.ds(j, tm, stride=V7X_SUBLANES), :] = packed[:, j * V7X_LANES:(j + 1) * V7X_LANES]
    logits = jnp.dot(xn.astype(BF16), wr_ref[...], preferred_element_type=F32) + br_ref[...]

    lane = lax.broadcasted_iota(jnp.int32, logits.shape, 1).astype(F32)
    work = logits
    vals, idxs = [], []
    onehot = jnp.zeros(logits.shape, F32)
    for _ in range(TOP_K):
        mx = jnp.max(work, axis=-1, keepdims=True)
        ix = jnp.min(jnp.where(work == mx, lane, float(N_EXPERTS)), axis=-1, keepdims=True)
        sel = lane == ix
        vals.append(mx)
        idxs.append(ix)
        onehot = jnp.where(sel, 1.0, onehot)
        work = jnp.where(sel, NEG_INF, work)
    es = [jnp.exp(vv - vals[0]) for vv in vals]
    tot = es[0] + es[1] + es[2] + es[3]

    r = lax.broadcasted_iota(jnp.int32, (tm, tm), 0)
    c = lax.broadcasted_iota(jnp.int32, (tm, tm), 1)
    lower = jnp.where(c < r, 1.0, 0.0).astype(BF16)
    prefix = jnp.dot(lower, onehot.astype(BF16), preferred_element_type=F32) + run_ref[...]
    for kk in range(TOP_K):
        idx_ref[:, kk:kk + 1] = idxs[kk].astype(jnp.int32)
        gate_ref[:, kk:kk + 1] = es[kk] / tot
        rk = jnp.sum(jnp.where(lane == idxs[kk], prefix, 0.0), axis=-1, keepdims=True)
        rank_ref[:, kk:kk + 1] = rk.astype(jnp.int32)
    run_ref[...] = run_ref[...] + jnp.sum(onehot, axis=0, keepdims=True)
    cnt_ref[...] = run_ref[...].astype(jnp.int32)


def _cross_router(h1, kmem, vmem, gc, wq, wo, gf, wr, br, seq):
    t, d = h1.shape
    tm = min(CROSS_TM, seq)
    per_b = seq // tm
    m = kmem.shape[1]
    const = dict(pipeline_mode=pl.Buffered(1))
    outs = pl.pallas_call(
        _cross_router_kernel,
        grid=(t // tm,),
        in_specs=[
            pl.BlockSpec((tm, d), lambda i: (i, 0)),
            pl.BlockSpec((None, m, X_WIDTH), lambda i: (i // per_b, 0, 0)),
            pl.BlockSpec((None, m, X_WIDTH), lambda i: (i // per_b, 0, 0)),
            pl.BlockSpec((1, d), lambda i: (0, 0)),
            pl.BlockSpec((d, X_WIDTH), lambda i: (0, 0), **const),
            pl.BlockSpec((X_WIDTH, d), lambda i: (0, 0), **const),
            pl.BlockSpec((1, d), lambda i: (0, 0)),
            pl.BlockSpec((d, N_EXPERTS), lambda i: (0, 0), **const),
            pl.BlockSpec((1, N_EXPERTS), lambda i: (0, 0)),
        ],
        out_specs=[
            pl.BlockSpec((tm, d), lambda i: (i, 0)),
            pl.BlockSpec((tm * V7X_SUBLANES, V7X_LANES), lambda i: (i, 0)),
            pl.BlockSpec((tm, TOP_K), lambda i: (i, 0)),
            pl.BlockSpec((tm, TOP_K), lambda i: (i, 0)),
            pl.BlockSpec((tm, TOP_K), lambda i: (i, 0)),
            pl.BlockSpec((1, N_EXPERTS), lambda i: (0, 0)),
        ],
        out_shape=[
            jax.ShapeDtypeStruct((t, d), F32),
            jax.ShapeDtypeStruct((t * V7X_SUBLANES, V7X_LANES), U32),
            jax.ShapeDtypeStruct((t, TOP_K), jnp.int32),
            jax.ShapeDtypeStruct((t, TOP_K), F32),
            jax.ShapeDtypeStruct((t, TOP_K), jnp.int32),
            jax.ShapeDtypeStruct((1, N_EXPERTS), jnp.int32),
        ],
        scratch_shapes=[pltpu.VMEM((1, N_EXPERTS), F32)],
        compiler_params=_cparams(("arbitrary",)),
        name="cross_router",
    )(h1, kmem, vmem, gc, wq, wo, gf, wr, br)
    return outs


def _tile_rows(row):
    return pl.ds(pl.multiple_of(row * V7X_SUBLANES, V7X_SUBLANES), V7X_SUBLANES)


def _dispatch_kernel(pos_ref, tail_ref, x_ref, xs_ref, zero_ref, sem, zsem):
    i = pl.program_id(0)
    tm = x_ref.shape[0] // V7X_SUBLANES
    chunk_rows = zero_ref.shape[0]

    @pl.when(i == 0)
    def _():
        zero_ref[...] = jnp.zeros(zero_ref.shape, zero_ref.dtype)

        def tail_copy(e):
            row0 = pl.multiple_of(tail_ref[e] * V7X_SUBLANES, V7X_SUBLANES)
            return pltpu.make_async_copy(zero_ref, xs_ref.at[pl.ds(row0, chunk_rows)], zsem)

        for e in range(N_EXPERTS):
            @pl.when(tail_ref[e] >= 0)
            def _():
                tail_copy(e).start()
        for e in range(N_EXPERTS):
            @pl.when(tail_ref[e] >= 0)
            def _():
                tail_copy(e).wait()

    def row_copy(r, p):
        return pltpu.make_async_copy(x_ref.at[_tile_rows(r)], xs_ref.at[_tile_rows(p)], sem)

    def issue(r, carry):
        for kk in range(TOP_K):
            row_copy(r, pos_ref[0, 0, r * TOP_K + kk]).start(priority=kk % 2)
        return carry

    lax.fori_loop(0, tm, issue, 0, unroll=DMA_ISSUE_UNROLL)
    for _ in range(tm * TOP_K):
        row_copy(0, 0).wait()


def _dispatch(xn, pos, tails, n_rows):
    t = xn.shape[0] // V7X_SUBLANES
    tm = min(DISPATCH_TM, t)
    pos3 = pos.reshape(t // tm, 1, tm * TOP_K)
    return pl.pallas_call(
        _dispatch_kernel,
        grid=(t // tm,),
        in_specs=[
            pl.BlockSpec((1, 1, tm * TOP_K), lambda i: (i, 0, 0), memory_space=pltpu.SMEM),
            pl.BlockSpec(memory_space=pltpu.SMEM),
            pl.BlockSpec((tm * V7X_SUBLANES, V7X_LANES), lambda i: (i, 0)),
        ],
        out_specs=pl.BlockSpec(memory_space=pl.ANY),
        out_shape=jax.ShapeDtypeStruct((n_rows * V7X_SUBLANES, V7X_LANES), xn.dtype),
        scratch_shapes=[pltpu.VMEM((EXPERT_CHUNK * V7X_SUBLANES, V7X_LANES), xn.dtype),
                        pltpu.SemaphoreType.DMA(()), pltpu.SemaphoreType.DMA(())],
        compiler_params=_cparams(("arbitrary",)),
        name="dispatch",
    )(pos3, tails, xn)


def _expert_kernel(ve_ref, vr_ref, vb_ref, xs_ref, wg_ref, wl_ref, bg_ref, bl_ref, wd_ref, bd_ref, ys_ref,
                   x_ref, y_ref, act_ref, wa_ref, wb_ref, xsem, ysem):
    del ve_ref
    v = pl.program_id(0)
    s = pl.program_id(1)
    rows = vr_ref[v]
    nj = act_ref.shape[0]
    tf = act_ref.shape[2]
    ch = EXPERT_CHUNK
    n_chunks = x_ref.shape[0] // (ch * V7X_SUBLANES)
    half = tf // 2
    last_step = pl.num_programs(1) - 1
    sub = V7X_SUBLANES

    def x_copy(visit):
        row0 = pl.multiple_of(vb_ref[visit] * x_ref.shape[0], V7X_SUBLANES)
        return pltpu.make_async_copy(xs_ref.at[pl.ds(row0, x_ref.shape[0])], x_ref, xsem)

    def y_copy(visit):
        row0 = pl.multiple_of(vb_ref[visit] * y_ref.shape[0], V7X_SUBLANES)
        return pltpu.make_async_copy(y_ref, ys_ref.at[pl.ds(row0, y_ref.shape[0])], ysem)

    @pl.when((s == nj) & (v > 0))
    def _():
        y_copy(v - 1).wait()

    @pl.when((v == 0) & (s == 0))
    def _():
        x_copy(0).start()

    @pl.when(s == 0)
    def _():
        x_copy(v).wait()

    @pl.when((s == nj) & (v + 1 < pl.num_programs(0)))
    def _():
        x_copy(v + 1).start()

    @pl.when((rows > 0) & (s < nj))
    def _():
        wa_ref[...] = wg_ref[...].astype(BF16)
        wb_ref[...] = wl_ref[...].astype(BF16)
        for c in range(n_chunks):
            @pl.when(c * ch < rows)
            def _():
                xw = [x_ref[pl.ds(c * ch * V7X_SUBLANES + j, ch, stride=V7X_SUBLANES), :]
                      for j in range(V7X_SUBLANES)]
                xb = jnp.concatenate([_unpack_lo(u).astype(BF16) for u in xw]
                                     + [_unpack_hi(u).astype(BF16) for u in xw], axis=1)
                glu = jnp.dot(xb, wa_ref[...], preferred_element_type=F32) + bg_ref[...]
                lin = jnp.dot(xb, wb_ref[...], preferred_element_type=F32) + bl_ref[...]
                glu = jnp.minimum(glu, SWIGLU_LIMIT)
                lin = jnp.clip(lin, -SWIGLU_LIMIT, SWIGLU_LIMIT)
                act = glu * jax.nn.sigmoid(SWIGLU_ALPHA * glu) * (lin + 1.0)
                act_ref[s, pl.ds(c * ch, ch), :] = act.astype(BF16)

    @pl.when((rows > 0) & (s >= nj))
    def _():
        wa_ref[...] = wd_ref[...].astype(BF16)
        for c in range(n_chunks):
            sl = pl.ds(c * ch, ch)

            @pl.when(c * ch < rows)
            def _():
                acc = jnp.dot(act_ref[0, sl, :], wa_ref[0:tf, :], preferred_element_type=F32)
                for j in range(1, nj):
                    acc = acc + jnp.dot(act_ref[j, sl, :], wa_ref[j * tf:(j + 1) * tf, :],
                                        preferred_element_type=F32)
                acc = acc + bd_ref[...]
                packed = _pack_pair(acc[:, :half], acc[:, half:])
                for jj in range(half // V7X_LANES):
                    tile_sub = (s - nj) * (half // V7X_LANES) + jj
                    y_ref[pl.ds(c * ch * sub + tile_sub, ch, stride=sub), :] = (
                        packed[:, jj * V7X_LANES:(jj + 1) * V7X_LANES])

            @pl.when((c * ch >= rows) & (s == nj))
            def _():
                y_ref[pl.ds(c * ch * sub, ch * sub), :] = jnp.zeros((ch * sub, V7X_LANES), y_ref.dtype)

    @pl.when(s == last_step)
    def _():
        y_copy(v).start()

    @pl.when((s == last_step) & (v == pl.num_programs(0) - 1))
    def _():
        y_copy(v).wait()


def _experts(xs, visit_expert, visit_rows, visit_blk, n_used, w_gu, b_gu, w_down, b_down):
    n_rows = xs.shape[0] // V7X_SUBLANES
    e, d, ff2 = w_gu.shape
    ff = ff2 // 2
    r, tf = EXPERT_ROWS, EXPERT_TF
    nj = ff // tf
    nn = d // tf
    b_gu3 = b_gu.reshape(e, 1, ff2)
    b_down3 = b_down.reshape(e, 1, d)

    def up(v, s, vr):
        return jnp.where(vr[v] > 0, jnp.minimum(s, nj - 1), nj - 1)

    def down(v, s, vr):
        return jnp.where(vr[v] > 0, jnp.maximum(s - nj, 0), nn - 1)

    grid_spec = pltpu.PrefetchScalarGridSpec(
        num_scalar_prefetch=3,
        grid=(n_used, nj + nn),
        in_specs=[
            pl.BlockSpec(memory_space=pl.ANY),
            pl.BlockSpec((None, d, tf), lambda v, s, ve, vr, vb: (ve[v], 0, up(v, s, vr))),
            pl.BlockSpec((None, d, tf), lambda v, s, ve, vr, vb: (ve[v], 0, nj + up(v, s, vr))),
            pl.BlockSpec((None, 1, tf), lambda v, s, ve, vr, vb: (ve[v], 0, up(v, s, vr))),
            pl.BlockSpec((None, 1, tf), lambda v, s, ve, vr, vb: (ve[v], 0, nj + up(v, s, vr))),
            pl.BlockSpec((None, ff, tf), lambda v, s, ve, vr, vb: (ve[v], 0, down(v, s, vr))),
            pl.BlockSpec((None, 1, tf), lambda v, s, ve, vr, vb: (ve[v], 0, down(v, s, vr))),
        ],
        out_specs=pl.BlockSpec(memory_space=pl.ANY),
        scratch_shapes=[
            pltpu.VMEM((r * V7X_SUBLANES, V7X_LANES), xs.dtype),
            pltpu.VMEM((r * V7X_SUBLANES, V7X_LANES), U32),
            pltpu.VMEM((nj, r, tf), BF16),
            pltpu.VMEM((d, tf), BF16),
            pltpu.VMEM((d, tf), BF16),
            pltpu.SemaphoreType.DMA(()),
            pltpu.SemaphoreType.DMA(()),
        ],
    )
    return pl.pallas_call(
        _expert_kernel,
        grid_spec=grid_spec,
        out_shape=jax.ShapeDtypeStruct((n_rows * V7X_SUBLANES, V7X_LANES), U32),
        compiler_params=pltpu.CompilerParams(dimension_semantics=("arbitrary", "arbitrary"),
                                             vmem_limit_bytes=EXPERT_VMEM_LIMIT),
        name="experts",
    )(visit_expert, visit_rows, visit_blk, xs, w_gu, w_gu, b_gu3, b_gu3, w_down, b_down3)


def _combine_kernel(pos_ref, h_ref, gate_ref, g_ref, ys_ref, o_ref, buf_ref, sem):
    tm = h_ref.shape[0]
    subs_per_half = EXPERT_TF // 2 // V7X_LANES

    def row_copy(r, kk, p):
        return pltpu.make_async_copy(ys_ref.at[_tile_rows(p)], buf_ref.at[kk, _tile_rows(r)], sem)

    def issue(r, carry):
        for kk in range(TOP_K):
            row_copy(r, kk, pos_ref[0, 0, r * TOP_K + kk]).start(priority=kk % 2)
        return carry

    lax.fori_loop(0, tm, issue, 0, unroll=DMA_ISSUE_UNROLL)
    for _ in range(tm * TOP_K):
        row_copy(0, 0, 0).wait()

    gates = gate_ref[...]
    y = h_ref[...]
    for kk in range(TOP_K):
        words = [buf_ref[kk, pl.ds(j, tm, stride=V7X_SUBLANES), :] for j in range(V7X_SUBLANES)]
        pieces = []
        for n in range(V7X_SUBLANES // subs_per_half):
            chunk = words[n * subs_per_half:(n + 1) * subs_per_half]
            pieces += [_unpack_lo(u) for u in chunk] + [_unpack_hi(u) for u in chunk]
        y = y + gates[:, kk:kk + 1] * jnp.concatenate(pieces, axis=1)
    o_ref[...] = _rms(y, g_ref[...])


def _combine(h2, gates, pos, ys, g_final):
    t, d = h2.shape
    tm = min(COMBINE_TM, t)
    pos3 = pos.reshape(t // tm, 1, tm * TOP_K)
    return pl.pallas_call(
        _combine_kernel,
        grid=(t // tm,),
        in_specs=[
            pl.BlockSpec((1, 1, tm * TOP_K), lambda i: (i, 0, 0), memory_space=pltpu.SMEM),
            pl.BlockSpec((tm, d), lambda i: (i, 0)),
            pl.BlockSpec((tm, TOP_K), lambda i: (i, 0)),
            pl.BlockSpec((1, d), lambda i: (0, 0)),
            pl.BlockSpec(memory_space=pl.ANY),
        ],
        out_specs=pl.BlockSpec((tm, d), lambda i: (i, 0)),
        out_shape=jax.ShapeDtypeStruct((t, d), F32),
        scratch_shapes=[pltpu.VMEM((TOP_K, tm * V7X_SUBLANES, V7X_LANES), ys.dtype), pltpu.SemaphoreType.DMA(())],
        compiler_params=_cparams(("arbitrary",)),
        name="combine",
    )(pos3, h2, gates, g_final, ys)


def _routing_tables(counts, top_idx, rank, n_visits):
    r, ch = EXPERT_ROWS, EXPERT_CHUNK
    counts = counts.reshape(N_EXPERTS)
    padded = (counts + r - 1) // r * r
    ends = jnp.cumsum(padded)
    starts = ends - padded
    experts = jnp.arange(N_EXPERTS, dtype=jnp.int32)
    start_of = jnp.sum(jnp.where(top_idx[..., None] == experts, starts, 0), axis=-1)
    pos = start_of + rank
    visit_row0 = jnp.arange(n_visits, dtype=jnp.int32) * r
    owner = jnp.sum((ends[None, :] <= visit_row0[:, None]).astype(jnp.int32), axis=1)
    visit_expert = jnp.minimum(owner, N_EXPERTS - 1).astype(jnp.int32)
    visit_rows = jnp.clip(counts[visit_expert] - (visit_row0 - starts[visit_expert]), 0, r).astype(jnp.int32)
    n_used = (ends[-1] // r).astype(jnp.int32)
    last = jnp.maximum(n_used - 1, 0)
    ids = jnp.arange(n_visits, dtype=jnp.int32)
    active = ids < n_used
    visit_rows = jnp.where(active, visit_rows, 0)
    visit_blk = jnp.where(active, ids, last)
    visit_expert = jnp.where(active, visit_expert, visit_expert[last])
    tail = starts + jnp.minimum(counts // ch * ch, padded - ch)
    tail = jnp.where(counts > 0, tail, -1)
    return pos.astype(jnp.int32), visit_expert, visit_rows, visit_blk, tail.astype(jnp.int32), n_used


def kernel(x, mem, norm_mix_g, w_in, sinks, lambda_q1, lambda_k1, lambda_q2, lambda_k2, diff_subln_g, w_branch, w_o, norm_cross_g, norm_mem_g, w_cq, w_ckv, w_co, norm_ffn_g, w_router, b_router, w_gate_up, b_gate_up, w_down, b_down, norm_final_g):
    b, s, d = x.shape
    t = b * s
    x2 = x.reshape(t, d)
    l = 0

    w = w_in[l]
    o_qa, o_ka, o_va = 0, SWA_WIDTH, SWA_WIDTH + SWA_KV_WIDTH
    o_qd = o_va + SWA_KV_WIDTH
    o_kd, o_vd, o_g = o_qd + DIFF_WIDTH, o_qd + 2 * DIFF_WIDTH, o_qd + 3 * DIFF_WIDTH
    w_perm = jnp.concatenate([
        w[:, o_g:], w[:, o_qd:o_kd] * jnp.float32(HEAD_DIM ** -0.5 * LOG2E), w[:, o_kd:o_vd], w[:, o_vd:o_g],
        w[:, o_qa:o_ka], w[:, o_ka:o_va], w[:, o_va:o_qd]], axis=1).astype(BF16)
    proj = _inproj(x2, norm_mix_g[l].reshape(1, d), w_perm)
    c_qd = 2 * d
    c_kd, c_vd, c_qa = c_qd + DIFF_WIDTH, c_qd + 2 * DIFF_WIDTH, c_qd + 3 * DIFF_WIDTH
    c_ka, c_va = c_qa + SWA_WIDTH, c_qa + SWA_WIDTH + SWA_KV_WIDTH

    def seg(c0, width):
        return proj[:, c0:c0 + width].reshape(b, s, width)

    qaT = jnp.swapaxes(seg(c_qa, SWA_WIDTH), 1, 2)
    vaT = jnp.swapaxes(seg(c_va, SWA_KV_WIDTH), 1, 2)
    yaT = _swa(qaT, proj.reshape(b, s, proj.shape[1]), c_ka, vaT, sinks[l])
    ya = jnp.swapaxes(yaT, 1, 2).reshape(t, SWA_WIDTH)

    td = min(DIFF_T, s)
    nk = s // td
    hw = DIFF_HEAD_WIDTH
    cs, qx, kx = _diff_tables(td)
    qdT = jnp.swapaxes(seg(c_qd, DIFF_WIDTH), 1, 2)
    k4 = proj.reshape(b, nk, td, proj.shape[1])
    v5 = seg(c_vd, DIFF_WIDTH).reshape(b, nk, td, DIFF_HEADS, hw).transpose(0, 3, 1, 4, 2)
    ones = jnp.zeros((DIFF_DENOM_ROWS, td), BF16).at[0].set(1.0)
    lamv = jnp.stack([lambda_q1[l], lambda_k1[l], lambda_q2[l], lambda_k2[l]]).astype(F32)
    ydT = _diff(qdT, k4, c_kd, v5, qx, kx, ones, cs, lamv, diff_subln_g[l].astype(F32).reshape(hw, 1))
    yd = jnp.swapaxes(ydT, 1, 2).reshape(t, DIFF_WIDTH)

    h1 = _merge(ya, yd, proj, x2, w_branch[l].astype(BF16), w_o[l].astype(BF16))

    kmem, vmem = _memkv(mem, norm_mem_g[l].reshape(1, d), w_ckv[l].astype(BF16))
    h2, xn, top_idx, gates, rank, counts = _cross_router(
        h1, kmem, vmem, norm_cross_g[l].reshape(1, d), w_cq[l].astype(BF16), w_co[l].astype(BF16),
        norm_ffn_g[l].reshape(1, d), w_router[l].astype(BF16), b_router[l].reshape(1, N_EXPERTS), s)

    assert d // 2 == V7X_SUBLANES * V7X_LANES
    n_visits = (t * TOP_K) // EXPERT_ROWS + N_EXPERTS
    n_rows = n_visits * EXPERT_ROWS
    pos, visit_expert, visit_rows, visit_blk, tails, n_used = _routing_tables(counts, top_idx, rank, n_visits)
    xs = _dispatch(xn, pos, tails, n_rows)
    ys = _experts(xs, visit_expert, visit_rows, visit_blk, n_used, w_gate_up[l], b_gate_up[l], w_down[l], b_down[l])
    out = _combine(h2, gates, pos, ys, norm_final_g.reshape(1, d))
    return out.reshape(b, s, d)
```

```python
import functools
import math

import jax
import jax.numpy as jnp
from jax import lax
from jax.experimental import pallas as pl
from jax.experimental.pallas import tpu as pltpu

BF16 = jnp.bfloat16
F32 = jnp.float32
U32 = jnp.uint32
NEG_INF = float("-inf")
LOG2E = math.log2(math.e)

HEAD_DIM = 64
SWA_Q_HEADS = 16
SWA_KV_HEADS = 4
SWA_GROUP = SWA_Q_HEADS // SWA_KV_HEADS
SWA_BLOCK = 128
SWA_WIDTH = SWA_Q_HEADS * HEAD_DIM
SWA_KV_WIDTH = SWA_KV_HEADS * HEAD_DIM
DIFF_HEADS = 8
DIFF_HEAD_WIDTH = 2 * HEAD_DIM
DIFF_WIDTH = DIFF_HEADS * DIFF_HEAD_WIDTH
X_HEADS = 4
X_HEAD_DIM = 128
X_WIDTH = X_HEADS * X_HEAD_DIM
N_EXPERTS = 32
TOP_K = 4
SWIGLU_LIMIT = 7.0
SWIGLU_ALPHA = 1.702
RMS_EPS = 1e-5
LAMBDA_INIT = 0.8 - 0.6 * math.exp(-0.3 * 0)

V7X_VMEM_BYTES = 64 * 1024 * 1024
VMEM_LIMIT = V7X_VMEM_BYTES - 8 * 1024 * 1024
EXPERT_VMEM_LIMIT = V7X_VMEM_BYTES - 3 * 1024 * 1024
V7X_LANES = 128
V7X_SUBLANES = 8

INPROJ_TM = 1024
INPROJ_TN = 2176
DIFF_T = 512
DIFF_POS_SPLIT = 256
DIFF_DENOM_ROWS = 16
MERGE_TM = 256
CROSS_TM = 256
DISPATCH_TM = 256
EXPERT_CHUNK = 720
EXPERT_ROWS = 3 * EXPERT_CHUNK
EXPERT_TF = 512
COMBINE_TM = 256
DMA_ISSUE_UNROLL = 8


def _cparams(semantics):
    return pltpu.CompilerParams(dimension_semantics=semantics, vmem_limit_bytes=VMEM_LIMIT)


def _rms(x, g):
    ms = jnp.mean(x * x, axis=-1, keepdims=True)
    return x * lax.rsqrt(ms + RMS_EPS) * g


def _pack_pair(lo, hi):
    lo_bits = lax.bitcast_convert_type(lo.astype(BF16).astype(F32), U32)
    hi_bits = lax.bitcast_convert_type(hi.astype(BF16).astype(F32), U32)
    return hi_bits | (lo_bits >> 16)


def _unpack_lo(w):
    return lax.bitcast_convert_type(w << 16, F32)


def _unpack_hi(w):
    return lax.bitcast_convert_type(w & jnp.uint32(0xFFFF0000), F32)


def _inproj_kernel(x_ref, g_ref, w_ref, o_ref, xn_ref):
    @pl.when(pl.program_id(1) == 0)
    def _():
        xn_ref[...] = _rms(x_ref[...], g_ref[...]).astype(BF16)

    o_ref[...] = jnp.dot(xn_ref[...], w_ref[...], preferred_element_type=F32).astype(o_ref.dtype)


def _inproj(x2, g, w):
    t, d = x2.shape
    n = w.shape[1]
    tm, tn = min(INPROJ_TM, t), INPROJ_TN
    return pl.pallas_call(
        _inproj_kernel,
        grid=(t // tm, n // tn),
        in_specs=[
            pl.BlockSpec((tm, d), lambda i, j: (i, 0)),
            pl.BlockSpec((1, d), lambda i, j: (0, 0)),
            pl.BlockSpec((d, tn), lambda i, j: (0, j)),
        ],
        out_specs=pl.BlockSpec((tm, tn), lambda i, j: (i, j)),
        out_shape=jax.ShapeDtypeStruct((t, n), BF16),
        scratch_shapes=[pltpu.VMEM((tm, d), BF16)],
        compiler_params=_cparams(("parallel", "arbitrary")),
        name="inproj",
    )(x2, g, w)


def _swa_kernel(qT_ref, kp_ref, kc_ref, vp_ref, vc_ref, bias_ref, sink_ref, o_ref):
    n = pl.program_id(1)
    blk = SWA_BLOCK
    kband = jnp.concatenate([kp_ref[...], kc_ref[...]], axis=0)
    vband = jnp.concatenate([vp_ref[...], vc_ref[...]], axis=1)
    krow = lax.broadcasted_iota(jnp.int32, (2 * blk, SWA_GROUP * blk), 0)
    has_prev = n > 0
    zeros = jnp.zeros((HEAD_DIM, blk), BF16)
    for h in range(SWA_KV_HEADS):
        lane0 = 2 * HEAD_DIM * (h // 2)
        k128 = kband[:, lane0:lane0 + 2 * HEAD_DIM]
        pieces = []
        for g in range(SWA_GROUP):
            hq = SWA_GROUP * h + g
            qh = qT_ref[HEAD_DIM * hq:HEAD_DIM * (hq + 1), :]
            pieces.append(jnp.concatenate([qh, zeros] if h % 2 == 0 else [zeros, qh], axis=0))
        qz = jnp.concatenate(pieces, axis=1)
        s = jnp.dot(k128, qz, preferred_element_type=F32) * (HEAD_DIM ** -0.5)
        s = s + bias_ref[h]
        s = jnp.where((krow >= blk) | has_prev, s, NEG_INF)
        sink = sink_ref[h]
        m = jnp.maximum(jnp.max(s, axis=0, keepdims=True), sink)
        p = jnp.exp(s - m)
        denom = jnp.sum(p, axis=0, keepdims=True) + jnp.exp(sink - m)
        vh = vband[HEAD_DIM * h:HEAD_DIM * (h + 1), :]
        o = jnp.dot(vh, p.astype(BF16), preferred_element_type=F32) / denom
        for g in range(SWA_GROUP):
            hq = SWA_GROUP * h + g
            o_ref[HEAD_DIM * hq:HEAD_DIM * (hq + 1), :] = o[:, blk * g:blk * (g + 1)].astype(o_ref.dtype)


def _swa_tables(sinks):
    blk = SWA_BLOCK
    slopes = (2.0 ** (-8.0 * jnp.arange(1, SWA_Q_HEADS + 1, dtype=F32) / SWA_Q_HEADS)).reshape(SWA_KV_HEADS, SWA_GROUP)
    q_pos = jnp.arange(blk)[None, :] + blk
    k_pos = jnp.arange(2 * blk)[:, None]
    dist = q_pos - k_pos
    allowed = (dist >= 0) & (dist < SWA_BLOCK)
    bias = -slopes[:, :, None, None] * dist.astype(F32)[None, None]
    bias = jnp.where(allowed[None, None], bias, NEG_INF)
    bias = jnp.transpose(bias, (0, 2, 1, 3)).reshape(SWA_KV_HEADS, 2 * blk, SWA_GROUP * blk)
    sink = sinks.astype(F32).reshape(SWA_KV_HEADS, SWA_GROUP, 1)
    sink = jnp.broadcast_to(sink, (SWA_KV_HEADS, SWA_GROUP, blk)).reshape(SWA_KV_HEADS, 1, SWA_GROUP * blk)
    return bias, sink


def _swa(qT, k, k_col0, vT, sinks):
    b, _, s = qT.shape
    blk = SWA_BLOCK
    bias, sink = _swa_tables(sinks)
    prev = lambda n: jnp.maximum(n - 1, 0)
    kb = k_col0 // SWA_KV_WIDTH
    return pl.pallas_call(
        _swa_kernel,
        grid=(b, s // blk),
        in_specs=[
            pl.BlockSpec((None, SWA_WIDTH, blk), lambda bi, n: (bi, 0, n)),
            pl.BlockSpec((None, blk, SWA_KV_WIDTH), lambda bi, n: (bi, prev(n), kb)),
            pl.BlockSpec((None, blk, SWA_KV_WIDTH), lambda bi, n: (bi, n, kb)),
            pl.BlockSpec((None, SWA_KV_WIDTH, blk), lambda bi, n: (bi, 0, prev(n))),
            pl.BlockSpec((None, SWA_KV_WIDTH, blk), lambda bi, n: (bi, 0, n)),
            pl.BlockSpec(bias.shape, lambda bi, n: (0, 0, 0)),
            pl.BlockSpec(sink.shape, lambda bi, n: (0, 0, 0)),
        ],
        out_specs=pl.BlockSpec((None, SWA_WIDTH, blk), lambda bi, n: (bi, 0, n)),
        out_shape=jax.ShapeDtypeStruct((b, SWA_WIDTH, s), BF16),
        compiler_params=_cparams(("parallel", "arbitrary")),
        name="swa",
    )(qT, k, k, vT, vT, bias, sink)


def _diff_kernel(c_ref, qT_ref, qx_ref, k_ref, kx_ref, v_ref, ones_ref, lamv_ref, g_ref, o_ref,
                 q2_ref, z_ref, zm_ref, p_ref, a_ref, m_ref, acc_ref, *, t):
    h = pl.program_id(1)
    qi = pl.program_id(2)
    w = DIFF_HEAD_WIDTH
    c = c_ref[h]

    q2_ref[w:, :] = qx_ref[...]
    qt = qT_ref[...]
    row = lax.broadcasted_iota(jnp.int32, qt.shape, 0)
    zero = jnp.zeros_like(qt)
    q2_ref[:w, :t] = jnp.where(row < HEAD_DIM, qt, zero)
    q2_ref[:w, t:] = jnp.where(row >= HEAD_DIM, qt, zero)
    m_ref[...] = jnp.full(m_ref.shape, NEG_INF, F32)
    acc_ref[...] = jnp.zeros(acc_ref.shape, F32)
    p_ref[1] = jnp.zeros(p_ref.shape[1:], BF16)
    a_ref[1] = jnp.ones(a_ref.shape[1:], F32)

    def scores(k, slot):
        keys = jnp.concatenate([k_ref[k], kx_ref[...]], axis=1)
        z = jnp.dot(keys, q2_ref[...], preferred_element_type=F32)
        z_ref[slot] = z
        zm_ref[slot] = jnp.max(z, axis=0, keepdims=True)

    def values(k, slot):
        vals = jnp.concatenate([v_ref[jnp.maximum(k, 0)], ones_ref[...]], axis=0)
        pv = jnp.dot(vals, p_ref[slot], preferred_element_type=F32)
        acc_ref[...] = a_ref[slot] * acc_ref[...] + pv

    def softmax(k, slot, masked):
        z = z_ref[slot]
        if masked:
            kr = lax.broadcasted_iota(jnp.int32, z.shape, 0)
            qc = lax.broadcasted_iota(jnp.int32, z.shape, 1)
            qc = jnp.where(qc >= t, qc - t, qc)
            z = jnp.where(kr <= qc, z, NEG_INF)
            zmax = jnp.max(z, axis=0, keepdims=True)
        else:
            zmax = zm_ref[slot]
        off = c * ((k - qi) * t).astype(F32)
        m_old = m_ref[...]
        m_new = jnp.maximum(m_old, zmax + off)
        p_ref[slot] = jnp.exp2(z - (m_new - off)).astype(BF16)
        a_ref[slot] = jnp.exp2(m_old - m_new)
        m_ref[...] = m_new

    scores(0, 0)

    def pair(j, carry):
        k = 2 * j
        scores(k + 1, 1)
        values(k - 1, 1)
        softmax(k, 0, False)
        scores(k + 2, 0)
        values(k, 0)
        softmax(k + 1, 1, False)
        return carry

    lax.fori_loop(0, qi >> 1, pair, 0)

    @pl.when((qi & 1) == 0)
    def _():
        values(qi - 1, 1)
        softmax(qi, 0, True)
        values(qi, 0)

    @pl.when((qi & 1) == 1)
    def _():
        scores(qi, 1)
        values(qi - 2, 1)
        softmax(qi - 1, 0, False)
        values(qi - 1, 0)
        softmax(qi, 1, True)
        values(qi, 1)

    lamv = lamv_ref[...]
    lam = (jnp.exp(jnp.sum(lamv[0:1] * lamv[1:2], axis=-1, keepdims=True))
           - jnp.exp(jnp.sum(lamv[2:3] * lamv[3:4], axis=-1, keepdims=True)) + LAMBDA_INIT)
    acc = acc_ref[...]
    l = acc[w:w + 1, :]
    o = acc[:w, :t] / l[:, :t] - lam * (acc[:w, t:] / l[:, t:])
    ms = jnp.mean(o * o, axis=0, keepdims=True)
    y = o * lax.rsqrt(ms + RMS_EPS) * g_ref[...] * (1.0 - LAMBDA_INIT)
    o_ref[...] = y.astype(o_ref.dtype)


def _diff_tables(t):
    w = DIFF_HEAD_WIDTH
    slopes = 2.0 ** (-8.0 * jnp.arange(1, DIFF_HEADS + 1, dtype=F32) / DIFF_HEADS)
    c = slopes * jnp.float32(LOG2E)
    c1 = c.astype(BF16)
    c2 = (c - c1.astype(F32)).astype(BF16)
    c3 = (c - c1.astype(F32) - c2.astype(F32)).astype(BF16)
    parts = jnp.stack([c1, c2, c3, c1, c2, c3], axis=1)
    qx = jnp.zeros((DIFF_HEADS, w, 2 * t), BF16).at[:, :6, :].set(parts[:, :, None])
    r = jnp.arange(t)
    r_lo = (r % DIFF_POS_SPLIT).astype(BF16)
    r_hi = ((r // DIFF_POS_SPLIT) * DIFF_POS_SPLIT).astype(BF16)
    kx = jnp.zeros((t, w), BF16).at[:, 0:3].set(r_lo[:, None]).at[:, 3:6].set(r_hi[:, None])
    return c, qx, kx


def _diff(qT, k4, k_col0, v5, qx, kx, ones, cs, lamv, subln_g):
    b, _, s = qT.shape
    t = k4.shape[2]
    nk = s // t
    w = DIFF_HEAD_WIDTH
    wv = w + ones.shape[0]
    k_blk0 = k_col0 // w
    return pl.pallas_call(
        functools.partial(_diff_kernel, t=t),
        grid=(b, DIFF_HEADS, nk),
        in_specs=[
            pl.BlockSpec(memory_space=pltpu.SMEM),
            pl.BlockSpec((None, w, t), lambda bi, h, qi: (bi, h, qi)),
            pl.BlockSpec((None, w, 2 * t), lambda bi, h, qi: (h, 0, 0)),
            pl.BlockSpec((None, nk, t, w), lambda bi, h, qi: (bi, 0, 0, k_blk0 + h)),
            pl.BlockSpec((t, w), lambda bi, h, qi: (0, 0)),
            pl.BlockSpec((None, None, nk, w, t), lambda bi, h, qi: (bi, h, 0, 0, 0)),
            pl.BlockSpec(ones.shape, lambda bi, h, qi: (0, 0)),
            pl.BlockSpec((4, HEAD_DIM), lambda bi, h, qi: (0, 0)),
            pl.BlockSpec((w, 1), lambda bi, h, qi: (0, 0)),
        ],
        out_specs=pl.BlockSpec((None, w, t), lambda bi, h, qi: (bi, h, qi)),
        out_shape=jax.ShapeDtypeStruct((b, DIFF_WIDTH, s), BF16),
        scratch_shapes=[
            pltpu.VMEM((2 * w, 2 * t), BF16),
            pltpu.VMEM((2, t, 2 * t), F32),
            pltpu.VMEM((2, 1, 2 * t), F32),
            pltpu.VMEM((2, t, 2 * t), BF16),
            pltpu.VMEM((2, 1, 2 * t), F32),
            pltpu.VMEM((1, 2 * t), F32),
            pltpu.VMEM((wv, 2 * t), F32),
        ],
        compiler_params=_cparams(("arbitrary", "arbitrary", "arbitrary")),
        name="diffattn",
    )(cs, qT, qx, k4, kx, v5, ones, lamv, subln_g)


def _merge_kernel(ya_ref, yd_ref, g0_ref, g1_ref, x_ref, wb0_ref, wb1_ref, wo_ref, o_ref):
    ua = jnp.dot(ya_ref[...], wb0_ref[...], preferred_element_type=F32)
    ud = jnp.dot(yd_ref[...], wb1_ref[...], preferred_element_type=F32)
    merged = jax.nn.sigmoid(g0_ref[...].astype(F32)) * ua + jax.nn.sigmoid(g1_ref[...].astype(F32)) * ud
    o_ref[...] = x_ref[...] + jnp.dot(merged.astype(BF16), wo_ref[...], preferred_element_type=F32)


def _merge(ya, yd, proj, x2, wb, wo):
    t, d = x2.shape
    tm = min(MERGE_TM, t)
    bw = ya.shape[1]
    const = dict(pipeline_mode=pl.Buffered(1))
    return pl.pallas_call(
        _merge_kernel,
        grid=(t // tm,),
        in_specs=[
            pl.BlockSpec((tm, bw), lambda i: (i, 0)),
            pl.BlockSpec((tm, bw), lambda i: (i, 0)),
            pl.BlockSpec((tm, d), lambda i: (i, 0)),
            pl.BlockSpec((tm, d), lambda i: (i, 1)),
            pl.BlockSpec((tm, d), lambda i: (i, 0)),
            pl.BlockSpec((None, bw, d), lambda i: (0, 0, 0), **const),
            pl.BlockSpec((None, bw, d), lambda i: (1, 0, 0), **const),
            pl.BlockSpec((d, d), lambda i: (0, 0), **const),
        ],
        out_specs=pl.BlockSpec((tm, d), lambda i: (i, 0)),
        out_shape=jax.ShapeDtypeStruct((t, d), F32),
        compiler_params=_cparams(("parallel",)),
        name="merge",
    )(ya, yd, proj, proj, x2, wb, wb, wo)


def _memkv_kernel(mem_ref, g_ref, w_ref, k_ref, v_ref):
    mn = _rms(mem_ref[...], g_ref[...]).astype(BF16)
    kv = jnp.dot(mn, w_ref[...], preferred_element_type=F32)
    k_ref[...] = kv[:, :X_WIDTH].astype(k_ref.dtype)
    v_ref[...] = kv[:, X_WIDTH:].astype(v_ref.dtype)


def _memkv(mem, g, w):
    b, m, d = mem.shape
    out = jax.ShapeDtypeStruct((b, m, X_WIDTH), BF16)
    return pl.pallas_call(
        _memkv_kernel,
        grid=(b,),
        in_specs=[
            pl.BlockSpec((None, m, d), lambda i: (i, 0, 0)),
            pl.BlockSpec((1, d), lambda i: (0, 0)),
            pl.BlockSpec((d, 2 * X_WIDTH), lambda i: (0, 0)),
        ],
        out_specs=[pl.BlockSpec((None, m, X_WIDTH), lambda i: (i, 0, 0))] * 2,
        out_shape=[out, out],
        compiler_params=_cparams(("parallel",)),
        name="memkv",
    )(mem, g, w)


def _cross_router_kernel(h_ref, k_ref, v_ref, gc_ref, wq_ref, wo_ref, gf_ref, wr_ref, br_ref,
                         h2_ref, xn_ref, idx_ref, gate_ref, rank_ref, cnt_ref, run_ref):
    i = pl.program_id(0)
    tm, d = h_ref.shape

    @pl.when(i == 0)
    def _():
        run_ref[...] = jnp.zeros(run_ref.shape, F32)

    h = h_ref[...]
    hn = _rms(h, gc_ref[...]).astype(BF16)
    q = jnp.dot(hn, wq_ref[...], preferred_element_type=F32).astype(BF16)
    k = k_ref[...]
    v = v_ref[...]
    outs = []
    for hd in range(X_HEADS):
        sl = slice(X_HEAD_DIM * hd, X_HEAD_DIM * (hd + 1))
        s = lax.dot_general(q[:, sl], k[:, sl], (((1,), (1,)), ((), ())),
                            preferred_element_type=F32) * (X_HEAD_DIM ** -0.5)
        s = s - jnp.max(s, axis=-1, keepdims=True)
        p = jnp.exp(s)
        p = p / jnp.sum(p, axis=-1, keepdims=True)
        outs.append(jnp.dot(p.astype(BF16), v[:, sl], preferred_element_type=F32))
    o = jnp.concatenate(outs, axis=1).astype(BF16)
    h2 = h + jnp.dot(o, wo_ref[...], preferred_element_type=F32)
    h2_ref[...] = h2

    xn = _rms(h2, gf_ref[...])
    packed = _pack_pair(xn[:, :d // 2], xn[:, d // 2:])
    for j in range(V7X_SUBLANES):
        xn_ref[pl.ds(j, tm, stride=V7X_SUBLANES), :] = packed[:, j * V7X_LANES:(j + 1) * V7X_LANES]
    logits = jnp.dot(xn.astype(BF16), wr_ref[...], preferred_element_type=F32) + br_ref[...]

    lane = lax.broadcasted_iota(jnp.int32, logits.shape, 1).astype(F32)
    work = logits
    vals, idxs = [], []
    onehot = jnp.zeros(logits.shape, F32)
    for _ in range(TOP_K):
        mx = jnp.max(work, axis=-1, keepdims=True)
        ix = jnp.min(jnp.where(work == mx, lane, float(N_EXPERTS)), axis=-1, keepdims=True)
        sel = lane == ix
        vals.append(mx)
        idxs.append(ix)
        onehot = jnp.where(sel, 1.0, onehot)
        work = jnp.where(sel, NEG_INF, work)
    es = [jnp.exp(vv - vals[0]) for vv in vals]
    tot = es[0] + es[1] + es[2] + es[3]

    r = lax.broadcasted_iota(jnp.int32, (tm, tm), 0)
    c = lax.broadcasted_iota(jnp.int32, (tm, tm), 1)
    lower = jnp.where(c < r, 1.0, 0.0).astype(BF16)
    prefix = jnp.dot(lower, onehot.astype(BF16), preferred_element_type=F32) + run_ref[...]
    for kk in range(TOP_K):
        idx_ref[:, kk:kk + 1] = idxs[kk].astype(jnp.int32)
        gate_ref[:, kk:kk + 1] = es[kk] / tot
        rk = jnp.sum(jnp.where(lane == idxs[kk], prefix, 0.0), axis=-1, keepdims=True)
        rank_ref[:, kk:kk + 1] = rk.astype(jnp.int32)
    run_ref[...] = run_ref[...] + jnp.sum(onehot, axis=0, keepdims=True)
    cnt_ref[...] = run_ref[...].astype(jnp.int32)


def _cross_router(h1, kmem, vmem, gc, wq, wo, gf, wr, br, seq):
    t, d = h1.shape
    tm = min(CROSS_TM, seq)
    per_b = seq // tm
    m = kmem.shape[1]
    const = dict(pipeline_mode=pl.Buffered(1))
    outs = pl.pallas_call(
        _cross_router_kernel,
        grid=(t // tm,),
        in_specs=[
            pl.BlockSpec((tm, d), lambda i: (i, 0)),
            pl.BlockSpec((None, m, X_WIDTH), lambda i: (i // per_b, 0, 0)),
            pl.BlockSpec((None, m, X_WIDTH), lambda i: (i // per_b, 0, 0)),
            pl.BlockSpec((1, d), lambda i: (0, 0)),
            pl.BlockSpec((d, X_WIDTH), lambda i: (0, 0), **const),
            pl.BlockSpec((X_WIDTH, d), lambda i: (0, 0), **const),
            pl.BlockSpec((1, d), lambda i: (0, 0)),
            pl.BlockSpec((d, N_EXPERTS), lambda i: (0, 0), **const),
            pl.BlockSpec((1, N_EXPERTS), lambda i: (0, 0)),
        ],
        out_specs=[
            pl.BlockSpec((tm, d), lambda i: (i, 0)),
            pl.BlockSpec((tm * V7X_SUBLANES, V7X_LANES), lambda i: (i, 0)),
            pl.BlockSpec((tm, TOP_K), lambda i: (i, 0)),
            pl.BlockSpec((tm, TOP_K), lambda i: (i, 0)),
            pl.BlockSpec((tm, TOP_K), lambda i: (i, 0)),
            pl.BlockSpec((1, N_EXPERTS), lambda i: (0, 0)),
        ],
        out_shape=[
            jax.ShapeDtypeStruct((t, d), F32),
            jax.ShapeDtypeStruct((t * V7X_SUBLANES, V7X_LANES), U32),
            jax.ShapeDtypeStruct((t, TOP_K), jnp.int32),
            jax.ShapeDtypeStruct((t, TOP_K), F32),
            jax.ShapeDtypeStruct((t, TOP_K), jnp.int32),
            jax.ShapeDtypeStruct((1, N_EXPERTS), jnp.int32),
        ],
        scratch_shapes=[pltpu.VMEM((1, N_EXPERTS), F32)],
        compiler_params=_cparams(("arbitrary",)),
        name="cross_router",
    )(h1, kmem, vmem, gc, wq, wo, gf, wr, br)
    return outs


def _tile_rows(row):
    return pl.ds(pl.multiple_of(row * V7X_SUBLANES, V7X_SUBLANES), V7X_SUBLANES)


def _dispatch_kernel(pos_ref, tail_ref, x_ref, xs_ref, zero_ref, sem, zsem):
    i = pl.program_id(0)
    tm = x_ref.shape[0] // V7X_SUBLANES
    chunk_rows = zero_ref.shape[0]

    @pl.when(i == 0)
    def _():
        zero_ref[...] = jnp.zeros(zero_ref.shape, zero_ref.dtype)

        def tail_copy(e):
            row0 = pl.multiple_of(tail_ref[e] * V7X_SUBLANES, V7X_SUBLANES)
            return pltpu.make_async_copy(zero_ref, xs_ref.at[pl.ds(row0, chunk_rows)], zsem)

        for e in range(N_EXPERTS):
            @pl.when(tail_ref[e] >= 0)
            def _():
                tail_copy(e).start()
        for e in range(N_EXPERTS):
            @pl.when(tail_ref[e] >= 0)
            def _():
                tail_copy(e).wait()

    def row_copy(r, p):
        return pltpu.make_async_copy(x_ref.at[_tile_rows(r)], xs_ref.at[_tile_rows(p)], sem)

    def issue(r, carry):
        for kk in range(TOP_K):
            row_copy(r, pos_ref[0, 0, r * TOP_K + kk]).start(priority=kk % 2)
        return carry

    lax.fori_loop(0, tm, issue, 0, unroll=DMA_ISSUE_UNROLL)
    for _ in range(tm * TOP_K):
        row_copy(0, 0).wait()


def _dispatch(xn, pos, tails, n_rows):
    t = xn.shape[0] // V7X_SUBLANES
    tm = min(DISPATCH_TM, t)
    pos3 = pos.reshape(t // tm, 1, tm * TOP_K)
    return pl.pallas_call(
        _dispatch_kernel,
        grid=(t // tm,),
        in_specs=[
            pl.BlockSpec((1, 1, tm * TOP_K), lambda i: (i, 0, 0), memory_space=pltpu.SMEM),
            pl.BlockSpec(memory_space=pltpu.SMEM),
            pl.BlockSpec((tm * V7X_SUBLANES, V7X_LANES), lambda i: (i, 0)),
        ],
        out_specs=pl.BlockSpec(memory_space=pl.ANY),
        out_shape=jax.ShapeDtypeStruct((n_rows * V7X_SUBLANES, V7X_LANES), xn.dtype),
        scratch_shapes=[pltpu.VMEM((EXPERT_CHUNK * V7X_SUBLANES, V7X_LANES), xn.dtype),
                        pltpu.SemaphoreType.DMA(()), pltpu.SemaphoreType.DMA(())],
        compiler_params=_cparams(("arbitrary",)),
        name="dispatch",
    )(pos3, tails, xn)


def _expert_kernel(ve_ref, vr_ref, vb_ref, xs_ref, wg_ref, wl_ref, bg_ref, bl_ref, wd_ref, bd_ref, ys_ref,
                   x_ref, y_ref, act_ref, wa_ref, wb_ref, xsem, ysem):
    del ve_ref
    v = pl.program_id(0)
    s = pl.program_id(1)
    rows = vr_ref[v]
    nj = act_ref.shape[0]
    tf = act_ref.shape[2]
    ch = EXPERT_CHUNK
    n_chunks = x_ref.shape[0] // (ch * V7X_SUBLANES)
    half = tf // 2
    last_step = pl.num_programs(1) - 1
    sub = V7X_SUBLANES

    def x_copy(visit):
        row0 = pl.multiple_of(vb_ref[visit] * x_ref.shape[0], V7X_SUBLANES)
        return pltpu.make_async_copy(xs_ref.at[pl.ds(row0, x_ref.shape[0])], x_ref, xsem)

    def y_copy(visit):
        row0 = pl.multiple_of(vb_ref[visit] * y_ref.shape[0], V7X_SUBLANES)
        return pltpu.make_async_copy(y_ref, ys_ref.at[pl.ds(row0, y_ref.shape[0])], ysem)

    @pl.when((s == nj) & (v > 0))
    def _():
        y_copy(v - 1).wait()

    @pl.when((v == 0) & (s == 0))
    def _():
        x_copy(0).start()

    @pl.when(s == 0)
    def _():
        x_copy(v).wait()

    @pl.when((s == nj) & (v + 1 < pl.num_programs(0)))
    def _():
        x_copy(v + 1).start()

    @pl.when((rows > 0) & (s < nj))
    def _():
        wa_ref[...] = wg_ref[...].astype(BF16)
        wb_ref[...] = wl_ref[...].astype(BF16)
        for c in range(n_chunks):
            @pl.when(c * ch < rows)
            def _():
                xw = [x_ref[pl.ds(c * ch * V7X_SUBLANES + j, ch, stride=V7X_SUBLANES), :]
                      for j in range(V7X_SUBLANES)]
                xb = jnp.concatenate([_unpack_lo(u).astype(BF16) for u in xw]
                                     + [_unpack_hi(u).astype(BF16) for u in xw], axis=1)
                glu = jnp.dot(xb, wa_ref[...], preferred_element_type=F32) + bg_ref[...]
                lin = jnp.dot(xb, wb_ref[...], preferred_element_type=F32) + bl_ref[...]
                glu = jnp.minimum(glu, SWIGLU_LIMIT)
                lin = jnp.clip(lin, -SWIGLU_LIMIT, SWIGLU_LIMIT)
                act = glu * jax.nn.sigmoid(SWIGLU_ALPHA * glu) * (lin + 1.0)
                act_ref[s, pl.ds(c * ch, ch), :] = act.astype(BF16)

    @pl.when((rows > 0) & (s >= nj))
    def _():
        wa_ref[...] = wd_ref[...].astype(BF16)
        for c in range(n_chunks):
            sl = pl.ds(c * ch, ch)

            @pl.when(c * ch < rows)
            def _():
                acc = jnp.dot(act_ref[0, sl, :], wa_ref[0:tf, :], preferred_element_type=F32)
                for j in range(1, nj):
                    acc = acc + jnp.dot(act_ref[j, sl, :], wa_ref[j * tf:(j + 1) * tf, :],
                                        preferred_element_type=F32)
                acc = acc + bd_ref[...]
                packed = _pack_pair(acc[:, :half], acc[:, half:])
                for jj in range(half // V7X_LANES):
                    tile_sub = (s - nj) * (half // V7X_LANES) + jj
                    y_ref[pl.ds(c * ch * sub + tile_sub, ch, stride=sub), :] = (
                        packed[:, jj * V7X_LANES:(jj + 1) * V7X_LANES])

            @pl.when((c * ch >= rows) & (s == nj))
            def _():
                y_ref[pl.ds(c * ch * sub, ch * sub), :] = jnp.zeros((ch * sub, V7X_LANES), y_ref.dtype)

    @pl.when(s == last_step)
    def _():
        y_copy(v).start()

    @pl.when((s == last_step) & (v == pl.num_programs(0) - 1))
    def _():
        y_copy(v).wait()


def _experts(xs, visit_expert, visit_rows, visit_blk, n_used, w_gu, b_gu, w_down, b_down):
    n_rows = xs.shape[0] // V7X_SUBLANES
    e, d, ff2 = w_gu.shape
    ff = ff2 // 2
    r, tf = EXPERT_ROWS, EXPERT_TF
    nj = ff // tf
    nn = d // tf
    b_gu3 = b_gu.reshape(e, 1, ff2)
    b_down3 = b_down.reshape(e, 1, d)

    def up(v, s, vr):
        return jnp.where(vr[v] > 0, jnp.minimum(s, nj - 1), nj - 1)

    def down(v, s, vr):
        return jnp.where(vr[v] > 0, jnp.maximum(s - nj, 0), nn - 1)

    grid_spec = pltpu.PrefetchScalarGridSpec(
        num_scalar_prefetch=3,
        grid=(n_used, nj + nn),
        in_specs=[
            pl.BlockSpec(memory_space=pl.ANY),
            pl.BlockSpec((None, d, tf), lambda v, s, ve, vr, vb: (ve[v], 0, up(v, s, vr))),
            pl.BlockSpec((None, d, tf), lambda v, s, ve, vr, vb: (ve[v], 0, nj + up(v, s, vr))),
            pl.BlockSpec((None, 1, tf), lambda v, s, ve, vr, vb: (ve[v], 0, up(v, s, vr))),
            pl.BlockSpec((None, 1, tf), lambda v, s, ve, vr, vb: (ve[v], 0, nj + up(v, s, vr))),
            pl.BlockSpec((None, ff, tf), lambda v, s, ve, vr, vb: (ve[v], 0, down(v, s, vr))),
            pl.BlockSpec((None, 1, tf), lambda v, s, ve, vr, vb: (ve[v], 0, down(v, s, vr))),
        ],
        out_specs=pl.BlockSpec(memory_space=pl.ANY),
        scratch_shapes=[
            pltpu.VMEM((r * V7X_SUBLANES, V7X_LANES), xs.dtype),
            pltpu.VMEM((r * V7X_SUBLANES, V7X_LANES), U32),
            pltpu.VMEM((nj, r, tf), BF16),
            pltpu.VMEM((d, tf), BF16),
            pltpu.VMEM((d, tf), BF16),
            pltpu.SemaphoreType.DMA(()),
            pltpu.SemaphoreType.DMA(()),
        ],
    )
    return pl.pallas_call(
        _expert_kernel,
        grid_spec=grid_spec,
        out_shape=jax.ShapeDtypeStruct((n_rows * V7X_SUBLANES, V7X_LANES), U32),
        compiler_params=pltpu.CompilerParams(dimension_semantics=("arbitrary", "arbitrary"),
                                             vmem_limit_bytes=EXPERT_VMEM_LIMIT),
        name="experts",
    )(visit_expert, visit_rows, visit_blk, xs, w_gu, w_gu, b_gu3, b_gu3, w_down, b_down3)


def _combine_kernel(pos_ref, h_ref, gate_ref, g_ref, ys_ref, o_ref, buf_ref, sem):
    tm = h_ref.shape[0]
    subs_per_half = EXPERT_TF // 2 // V7X_LANES

    def row_copy(r, kk, p):
        return pltpu.make_async_copy(ys_ref.at[_tile_rows(p)], buf_ref.at[kk, _tile_rows(r)], sem)

    def issue(r, carry):
        for kk in range(TOP_K):
            row_copy(r, kk, pos_ref[0, 0, r * TOP_K + kk]).start(priority=kk % 2)
        return carry

    lax.fori_loop(0, tm, issue, 0, unroll=DMA_ISSUE_UNROLL)
    for _ in range(tm * TOP_K):
        row_copy(0, 0, 0).wait()

    gates = gate_ref[...]
    y = h_ref[...]
    for kk in range(TOP_K):
        words = [buf_ref[kk, pl.ds(j, tm, stride=V7X_SUBLANES), :] for j in range(V7X_SUBLANES)]
        pieces = []
        for n in range(V7X_SUBLANES // subs_per_half):
            chunk = words[n * subs_per_half:(n + 1) * subs_per_half]
            pieces += [_unpack_lo(u) for u in chunk] + [_unpack_hi(u) for u in chunk]
        y = y + gates[:, kk:kk + 1] * jnp.concatenate(pieces, axis=1)
    o_ref[...] = _rms(y, g_ref[...])


def _combine(h2, gates, pos, ys, g_final):
    t, d = h2.shape
    tm = min(COMBINE_TM, t)
    pos3 = pos.reshape(t // tm, 1, tm * TOP_K)
    return pl.pallas_call(
        _combine_kernel,
        grid=(t // tm,),
        in_specs=[
            pl.BlockSpec((1, 1, tm * TOP_K), lambda i: (i, 0, 0), memory_space=pltpu.SMEM),
            pl.BlockSpec((tm, d), lambda i: (i, 0)),
            pl.BlockSpec((tm, TOP_K), lambda i: (i, 0)),
            pl.BlockSpec((1, d), lambda i: (0, 0)),
            pl.BlockSpec(memory_space=pl.ANY),
        ],
        out_specs=pl.BlockSpec((tm, d), lambda i: (i, 0)),
        out_shape=jax.ShapeDtypeStruct((t, d), F32),
        scratch_shapes=[pltpu.VMEM((TOP_K, tm * V7X_SUBLANES, V7X_LANES), ys.dtype), pltpu.SemaphoreType.DMA(())],
        compiler_params=_cparams(("arbitrary",)),
        name="combine",
    )(pos3, h2, gates, g_final, ys)


def _routing_tables(counts, top_idx, rank, n_visits):
    r, ch = EXPERT_ROWS, EXPERT_CHUNK
    counts = counts.reshape(N_EXPERTS)
    padded = (counts + r - 1) // r * r
    ends = jnp.cumsum(padded)
    starts = ends - padded
    experts = jnp.arange(N_EXPERTS, dtype=jnp.int32)
    start_of = jnp.sum(jnp.where(top_idx[..., None] == experts, starts, 0), axis=-1)
    pos = start_of + rank
    visit_row0 = jnp.arange(n_visits, dtype=jnp.int32) * r
    owner = jnp.sum((ends[None, :] <= visit_row0[:, None]).astype(jnp.int32), axis=1)
    visit_expert = jnp.minimum(owner, N_EXPERTS - 1).astype(jnp.int32)
    visit_rows = jnp.clip(counts[visit_expert] - (visit_row0 - starts[visit_expert]), 0, r).astype(jnp.int32)
    n_used = (ends[-1] // r).astype(jnp.int32)
    last = jnp.maximum(n_used - 1, 0)
    ids = jnp.arange(n_visits, dtype=jnp.int32)
    active = ids < n_used
    visit_rows = jnp.where(active, visit_rows, 0)
    visit_blk = jnp.where(active, ids, last)
    visit_expert = jnp.where(active, visit_expert, visit_expert[last])
    tail = starts + jnp.minimum(counts // ch * ch, padded - ch)
    tail = jnp.where(counts > 0, tail, -1)
    return pos.astype(jnp.int32), visit_expert, visit_rows, visit_blk, tail.astype(jnp.int32), n_used


def kernel(x, mem, norm_mix_g, w_in, sinks, lambda_q1, lambda_k1, lambda_q2, lambda_k2, diff_subln_g, w_branch, w_o, norm_cross_g, norm_mem_g, w_cq, w_ckv, w_co, norm_ffn_g, w_router, b_router, w_gate_up, b_gate_up, w_down, b_down, norm_final_g):
    b, s, d = x.shape
    t = b * s
    x2 = x.reshape(t, d)
    l = 0

    w = w_in[l]
    o_qa, o_ka, o_va = 0, SWA_WIDTH, SWA_WIDTH + SWA_KV_WIDTH
    o_qd = o_va + SWA_KV_WIDTH
    o_kd, o_vd, o_g = o_qd + DIFF_WIDTH, o_qd + 2 * DIFF_WIDTH, o_qd + 3 * DIFF_WIDTH
    w_perm = jnp.concatenate([
        w[:, o_g:], w[:, o_qd:o_kd] * jnp.float32(HEAD_DIM ** -0.5 * LOG2E), w[:, o_kd:o_vd], w[:, o_vd:o_g],
        w[:, o_qa:o_ka], w[:, o_ka:o_va], w[:, o_va:o_qd]], axis=1).astype(BF16)
    proj = _inproj(x2, norm_mix_g[l].reshape(1, d), w_perm)
    c_qd = 2 * d
    c_kd, c_vd, c_qa = c_qd + DIFF_WIDTH, c_qd + 2 * DIFF_WIDTH, c_qd + 3 * DIFF_WIDTH
    c_ka, c_va = c_qa + SWA_WIDTH, c_qa + SWA_WIDTH + SWA_KV_WIDTH

    def seg(c0, width):
        return proj[:, c0:c0 + width].reshape(b, s, width)

    qaT = jnp.swapaxes(seg(c_qa, SWA_WIDTH), 1, 2)
    vaT = jnp.swapaxes(seg(c_va, SWA_KV_WIDTH), 1, 2)
    yaT = _swa(qaT, proj.reshape(b, s, proj.shape[1]), c_ka, vaT, sinks[l])
    ya = jnp.swapaxes(yaT, 1, 2).reshape(t, SWA_WIDTH)

    td = min(DIFF_T, s)
    nk = s // td
    hw = DIFF_HEAD_WIDTH
    cs, qx, kx = _diff_tables(td)
    qdT = jnp.swapaxes(seg(c_qd, DIFF_WIDTH), 1, 2)
    k4 = proj.reshape(b, nk, td, proj.shape[1])
    v5 = seg(c_vd, DIFF_WIDTH).reshape(b, nk, td, DIFF_HEADS, hw).transpose(0, 3, 1, 4, 2)
    ones = jnp.zeros((DIFF_DENOM_ROWS, td), BF16).at[0].set(1.0)
    lamv = jnp.stack([lambda_q1[l], lambda_k1[l], lambda_q2[l], lambda_k2[l]]).astype(F32)
    ydT = _diff(qdT, k4, c_kd, v5, qx, kx, ones, cs, lamv, diff_subln_g[l].astype(F32).reshape(hw, 1))
    yd = jnp.swapaxes(ydT, 1, 2).reshape(t, DIFF_WIDTH)

    h1 = _merge(ya, yd, proj, x2, w_branch[l].astype(BF16), w_o[l].astype(BF16))

    kmem, vmem = _memkv(mem, norm_mem_g[l].reshape(1, d), w_ckv[l].astype(BF16))
    h2, xn, top_idx, gates, rank, counts = _cross_router(
        h1, kmem, vmem, norm_cross_g[l].reshape(1, d), w_cq[l].astype(BF16), w_co[l].astype(BF16),
        norm_ffn_g[l].reshape(1, d), w_router[l].astype(BF16), b_router[l].reshape(1, N_EXPERTS), s)

    assert d // 2 == V7X_SUBLANES * V7X_LANES
    n_visits = (t * TOP_K) // EXPERT_ROWS + N_EXPERTS
    n_rows = n_visits * EXPERT_ROWS
    pos, visit_expert, visit_rows, visit_blk, tails, n_used = _routing_tables(counts, top_idx, rank, n_visits)
    xs = _dispatch(xn, pos, tails, n_rows)
    ys = _experts(xs, visit_expert, visit_rows, visit_blk, n_used, w_gate_up[l], b_gate_up[l], w_down[l], b_down[l])
    out = _combine(h2, gates, pos, ys, norm_final_g.reshape(1, d))
    return out.reshape(b, s, d)
```

```python
import functools
import math

import jax
import jax.numpy as jnp
from jax import lax
from jax.experimental import pallas as pl
from jax.experimental.pallas import tpu as pltpu

BF16 = jnp.bfloat16
F32 = jnp.float32
U32 = jnp.uint32
NEG_INF = float("-inf")
LOG2E = math.log2(math.e)

HEAD_DIM = 64
SWA_Q_HEADS = 16
SWA_KV_HEADS = 4
SWA_GROUP = SWA_Q_HEADS // SWA_KV_HEADS
SWA_BLOCK = 128
SWA_WIDTH = SWA_Q_HEADS * HEAD_DIM
SWA_KV_WIDTH = SWA_KV_HEADS * HEAD_DIM
DIFF_HEADS = 8
DIFF_HEAD_WIDTH = 2 * HEAD_DIM
DIFF_WIDTH = DIFF_HEADS * DIFF_HEAD_WIDTH
X_HEADS = 4
X_HEAD_DIM = 128
X_WIDTH = X_HEADS * X_HEAD_DIM
N_EXPERTS = 32
TOP_K = 4
SWIGLU_LIMIT = 7.0
SWIGLU_ALPHA = 1.702
RMS_EPS = 1e-5
LAMBDA_INIT = 0.8 - 0.6 * math.exp(-0.3 * 0)

V7X_VMEM_BYTES = 64 * 1024 * 1024
VMEM_LIMIT = V7X_VMEM_BYTES - 8 * 1024 * 1024
EXPERT_VMEM_LIMIT = V7X_VMEM_BYTES - 3 * 1024 * 1024
V7X_LANES = 128
V7X_SUBLANES = 8

INPROJ_TM = 1024
INPROJ_TN = 2176
DIFF_T = 512
DIFF_POS_SPLIT = 256
DIFF_DENOM_ROWS = 16
MERGE_TM = 256
CROSS_TM = 256
DISPATCH_TM = 512
EXPERT_CHUNK = 768
EXPERT_ROWS = 3 * EXPERT_CHUNK
EXPERT_TF = 512
COMBINE_TM = 512
DMA_ISSUE_UNROLL = 8


def _cparams(semantics):
    return pltpu.CompilerParams(dimension_semantics=semantics, vmem_limit_bytes=VMEM_LIMIT)


def _rms(x, g):
    ms = jnp.mean(x * x, axis=-1, keepdims=True)
    return x * lax.rsqrt(ms + RMS_EPS) * g


def _pack_pair(lo, hi):
    lo_bits = lax.bitcast_convert_type(lo.astype(BF16).astype(F32), U32)
    hi_bits = lax.bitcast_convert_type(hi.astype(BF16).astype(F32), U32)
    return hi_bits | (lo_bits >> 16)


def _unpack_lo(w):
    return lax.bitcast_convert_type(w << 16, F32)


def _unpack_hi(w):
    return lax.bitcast_convert_type(w & jnp.uint32(0xFFFF0000), F32)


def _inproj_kernel(x_ref, g_ref, w_ref, o_ref, xn_ref):
    @pl.when(pl.program_id(1) == 0)
    def _():
        xn_ref[...] = _rms(x_ref[...], g_ref[...]).astype(BF16)

    o_ref[...] = jnp.dot(xn_ref[...], w_ref[...], preferred_element_type=F32).astype(o_ref.dtype)


def _inproj(x2, g, w):
    t, d = x2.shape
    n = w.shape[1]
    tm, tn = min(INPROJ_TM, t), INPROJ_TN
    return pl.pallas_call(
        _inproj_kernel,
        grid=(t // tm, n // tn),
        in_specs=[
            pl.BlockSpec((tm, d), lambda i, j: (i, 0)),
            pl.BlockSpec((1, d), lambda i, j: (0, 0)),
            pl.BlockSpec((d, tn), lambda i, j: (0, j)),
        ],
        out_specs=pl.BlockSpec((tm, tn), lambda i, j: (i, j)),
        out_shape=jax.ShapeDtypeStruct((t, n), BF16),
        scratch_shapes=[pltpu.VMEM((tm, d), BF16)],
        compiler_params=_cparams(("parallel", "arbitrary")),
        name="inproj",
    )(x2, g, w)


def _swa_kernel(qT_ref, kp_ref, kc_ref, vp_ref, vc_ref, bias_ref, sink_ref, o_ref):
    n = pl.program_id(1)
    blk = SWA_BLOCK
    kband = jnp.concatenate([kp_ref[...], kc_ref[...]], axis=0)
    vband = jnp.concatenate([vp_ref[...], vc_ref[...]], axis=1)
    krow = lax.broadcasted_iota(jnp.int32, (2 * blk, SWA_GROUP * blk), 0)
    has_prev = n > 0
    zeros = jnp.zeros((HEAD_DIM, blk), BF16)
    for h in range(SWA_KV_HEADS):
        lane0 = 2 * HEAD_DIM * (h // 2)
        k128 = kband[:, lane0:lane0 + 2 * HEAD_DIM]
        pieces = []
        for g in range(SWA_GROUP):
            hq = SWA_GROUP * h + g
            qh = qT_ref[HEAD_DIM * hq:HEAD_DIM * (hq + 1), :]
            pieces.append(jnp.concatenate([qh, zeros] if h % 2 == 0 else [zeros, qh], axis=0))
        qz = jnp.concatenate(pieces, axis=1)
        s = jnp.dot(k128, qz, preferred_element_type=F32) * (HEAD_DIM ** -0.5)
        s = s + bias_ref[h]
        s = jnp.where((krow >= blk) | has_prev, s, NEG_INF)
        sink = sink_ref[h]
        m = jnp.maximum(jnp.max(s, axis=0, keepdims=True), sink)
        p = jnp.exp(s - m)
        denom = jnp.sum(p, axis=0, keepdims=True) + jnp.exp(sink - m)
        vh = vband[HEAD_DIM * h:HEAD_DIM * (h + 1), :]
        o = jnp.dot(vh, p.astype(BF16), preferred_element_type=F32) / denom
        for g in range(SWA_GROUP):
            hq = SWA_GROUP * h + g
            o_ref[HEAD_DIM * hq:HEAD_DIM * (hq + 1), :] = o[:, blk * g:blk * (g + 1)].astype(o_ref.dtype)


def _swa_tables(sinks):
    blk = SWA_BLOCK
    slopes = (2.0 ** (-8.0 * jnp.arange(1, SWA_Q_HEADS + 1, dtype=F32) / SWA_Q_HEADS)).reshape(SWA_KV_HEADS, SWA_GROUP)
    q_pos = jnp.arange(blk)[None, :] + blk
    k_pos = jnp.arange(2 * blk)[:, None]
    dist = q_pos - k_pos
    allowed = (dist >= 0) & (dist < SWA_BLOCK)
    bias = -slopes[:, :, None, None] * dist.astype(F32)[None, None]
    bias = jnp.where(allowed[None, None], bias, NEG_INF)
    bias = jnp.transpose(bias, (0, 2, 1, 3)).reshape(SWA_KV_HEADS, 2 * blk, SWA_GROUP * blk)
    sink = sinks.astype(F32).reshape(SWA_KV_HEADS, SWA_GROUP, 1)
    sink = jnp.broadcast_to(sink, (SWA_KV_HEADS, SWA_GROUP, blk)).reshape(SWA_KV_HEADS, 1, SWA_GROUP * blk)
    return bias, sink


def _swa(qT, k, k_col0, vT, sinks):
    b, _, s = qT.shape
    blk = SWA_BLOCK
    bias, sink = _swa_tables(sinks)
    prev = lambda n: jnp.maximum(n - 1, 0)
    kb = k_col0 // SWA_KV_WIDTH
    return pl.pallas_call(
        _swa_kernel,
        grid=(b, s // blk),
        in_specs=[
            pl.BlockSpec((None, SWA_WIDTH, blk), lambda bi, n: (bi, 0, n)),
            pl.BlockSpec((None, blk, SWA_KV_WIDTH), lambda bi, n: (bi, prev(n), kb)),
            pl.BlockSpec((None, blk, SWA_KV_WIDTH), lambda bi, n: (bi, n, kb)),
            pl.BlockSpec((None, SWA_KV_WIDTH, blk), lambda bi, n: (bi, 0, prev(n))),
            pl.BlockSpec((None, SWA_KV_WIDTH, blk), lambda bi, n: (bi, 0, n)),
            pl.BlockSpec(bias.shape, lambda bi, n: (0, 0, 0)),
            pl.BlockSpec(sink.shape, lambda bi, n: (0, 0, 0)),
        ],
        out_specs=pl.BlockSpec((None, SWA_WIDTH, blk), lambda bi, n: (bi, 0, n)),
        out_shape=jax.ShapeDtypeStruct((b, SWA_WIDTH, s), BF16),
        compiler_params=_cparams(("parallel", "arbitrary")),
        name="swa",
    )(qT, k, k, vT, vT, bias, sink)


def _diff_kernel(c_ref, qT_ref, qx_ref, k_ref, kx_ref, v_ref, ones_ref, lamv_ref, g_ref, o_ref,
                 q2_ref, z_ref, zm_ref, p_ref, a_ref, m_ref, acc_ref, *, t):
    h = pl.program_id(1)
    qi = pl.program_id(2)
    w = DIFF_HEAD_WIDTH
    c = c_ref[h]

    q2_ref[w:, :] = qx_ref[...]
    qt = qT_ref[...]
    row = lax.broadcasted_iota(jnp.int32, qt.shape, 0)
    zero = jnp.zeros_like(qt)
    q2_ref[:w, :t] = jnp.where(row < HEAD_DIM, qt, zero)
    q2_ref[:w, t:] = jnp.where(row >= HEAD_DIM, qt, zero)
    m_ref[...] = jnp.full(m_ref.shape, NEG_INF, F32)
    acc_ref[...] = jnp.zeros(acc_ref.shape, F32)
    p_ref[1] = jnp.zeros(p_ref.shape[1:], BF16)
    a_ref[1] = jnp.ones(a_ref.shape[1:], F32)

    def scores(k, slot):
        keys = jnp.concatenate([k_ref[k], kx_ref[...]], axis=1)
        z = jnp.dot(keys, q2_ref[...], preferred_element_type=F32)
        z_ref[slot] = z
        zm_ref[slot] = jnp.max(z, axis=0, keepdims=True)

    def values(k, slot):
        vals = jnp.concatenate([v_ref[jnp.maximum(k, 0)], ones_ref[...]], axis=0)
        pv = jnp.dot(vals, p_ref[slot], preferred_element_type=F32)
        acc_ref[...] = a_ref[slot] * acc_ref[...] + pv

    def softmax(k, slot, masked):
        z = z_ref[slot]
        if masked:
            kr = lax.broadcasted_iota(jnp.int32, z.shape, 0)
            qc = lax.broadcasted_iota(jnp.int32, z.shape, 1)
            qc = jnp.where(qc >= t, qc - t, qc)
            z = jnp.where(kr <= qc, z, NEG_INF)
            zmax = jnp.max(z, axis=0, keepdims=True)
        else:
            zmax = zm_ref[slot]
        off = c * ((k - qi) * t).astype(F32)
        m_old = m_ref[...]
        m_new = jnp.maximum(m_old, zmax + off)
        p_ref[slot] = jnp.exp2(z - (m_new - off)).astype(BF16)
        a_ref[slot] = jnp.exp2(m_old - m_new)
        m_ref[...] = m_new

    scores(0, 0)

    def pair(j, carry):
        k = 2 * j
        scores(k + 1, 1)
        values(k - 1, 1)
        softmax(k, 0, False)
        scores(k + 2, 0)
        values(k, 0)
        softmax(k + 1, 1, False)
        return carry

    lax.fori_loop(0, qi >> 1, pair, 0)

    @pl.when((qi & 1) == 0)
    def _():
        values(qi - 1, 1)
        softmax(qi, 0, True)
        values(qi, 0)

    @pl.when((qi & 1) == 1)
    def _():
        scores(qi, 1)
        values(qi - 2, 1)
        softmax(qi - 1, 0, False)
        values(qi - 1, 0)
        softmax(qi, 1, True)
        values(qi, 1)

    lamv = lamv_ref[...]
    lam = (jnp.exp(jnp.sum(lamv[0:1] * lamv[1:2], axis=-1, keepdims=True))
           - jnp.exp(jnp.sum(lamv[2:3] * lamv[3:4], axis=-1, keepdims=True)) + LAMBDA_INIT)
    acc = acc_ref[...]
    l = acc[w:w + 1, :]
    o = acc[:w, :t] / l[:, :t] - lam * (acc[:w, t:] / l[:, t:])
    ms = jnp.mean(o * o, axis=0, keepdims=True)
    y = o * lax.rsqrt(ms + RMS_EPS) * g_ref[...] * (1.0 - LAMBDA_INIT)
    o_ref[...] = y.astype(o_ref.dtype)


def _diff_tables(t):
    w = DIFF_HEAD_WIDTH
    slopes = 2.0 ** (-8.0 * jnp.arange(1, DIFF_HEADS + 1, dtype=F32) / DIFF_HEADS)
    c = slopes * jnp.float32(LOG2E)
    c1 = c.astype(BF16)
    c2 = (c - c1.astype(F32)).astype(BF16)
    c3 = (c - c1.astype(F32) - c2.astype(F32)).astype(BF16)
    parts = jnp.stack([c1, c2, c3, c1, c2, c3], axis=1)
    qx = jnp.zeros((DIFF_HEADS, w, 2 * t), BF16).at[:, :6, :].set(parts[:, :, None])
    r = jnp.arange(t)
    r_lo = (r % DIFF_POS_SPLIT).astype(BF16)
    r_hi = ((r // DIFF_POS_SPLIT) * DIFF_POS_SPLIT).astype(BF16)
    kx = jnp.zeros((t, w), BF16).at[:, 0:3].set(r_lo[:, None]).at[:, 3:6].set(r_hi[:, None])
    return c, qx, kx


def _diff(qT, k4, k_col0, v5, qx, kx, ones, cs, lamv, subln_g):
    b, _, s = qT.shape
    t = k4.shape[2]
    nk = s // t
    w = DIFF_HEAD_WIDTH
    wv = w + ones.shape[0]
    k_blk0 = k_col0 // w
    return pl.pallas_call(
        functools.partial(_diff_kernel, t=t),
        grid=(b, DIFF_HEADS, nk),
        in_specs=[
            pl.BlockSpec(memory_space=pltpu.SMEM),
            pl.BlockSpec((None, w, t), lambda bi, h, qi: (bi, h, qi)),
            pl.BlockSpec((None, w, 2 * t), lambda bi, h, qi: (h, 0, 0)),
            pl.BlockSpec((None, nk, t, w), lambda bi, h, qi: (bi, 0, 0, k_blk0 + h)),
            pl.BlockSpec((t, w), lambda bi, h, qi: (0, 0)),
            pl.BlockSpec((None, None, nk, w, t), lambda bi, h, qi: (bi, h, 0, 0, 0)),
            pl.BlockSpec(ones.shape, lambda bi, h, qi: (0, 0)),
            pl.BlockSpec((4, HEAD_DIM), lambda bi, h, qi: (0, 0)),
            pl.BlockSpec((w, 1), lambda bi, h, qi: (0, 0)),
        ],
        out_specs=pl.BlockSpec((None, w, t), lambda bi, h, qi: (bi, h, qi)),
        out_shape=jax.ShapeDtypeStruct((b, DIFF_WIDTH, s), BF16),
        scratch_shapes=[
            pltpu.VMEM((2 * w, 2 * t), BF16),
            pltpu.VMEM((2, t, 2 * t), F32),
            pltpu.VMEM((2, 1, 2 * t), F32),
            pltpu.VMEM((2, t, 2 * t), BF16),
            pltpu.VMEM((2, 1, 2 * t), F32),
            pltpu.VMEM((1, 2 * t), F32),
            pltpu.VMEM((wv, 2 * t), F32),
        ],
        compiler_params=_cparams(("arbitrary", "arbitrary", "arbitrary")),
        name="diffattn",
    )(cs, qT, qx, k4, kx, v5, ones, lamv, subln_g)


def _merge_kernel(ya_ref, yd_ref, g0_ref, g1_ref, x_ref, wb0_ref, wb1_ref, wo_ref, o_ref):
    ua = jnp.dot(ya_ref[...], wb0_ref[...], preferred_element_type=F32)
    ud = jnp.dot(yd_ref[...], wb1_ref[...], preferred_element_type=F32)
    merged = jax.nn.sigmoid(g0_ref[...].astype(F32)) * ua + jax.nn.sigmoid(g1_ref[...].astype(F32)) * ud
    o_ref[...] = x_ref[...] + jnp.dot(merged.astype(BF16), wo_ref[...], preferred_element_type=F32)


def _merge(ya, yd, proj, x2, wb, wo):
    t, d = x2.shape
    tm = min(MERGE_TM, t)
    bw = ya.shape[1]
    const = dict(pipeline_mode=pl.Buffered(1))
    return pl.pallas_call(
        _merge_kernel,
        grid=(t // tm,),
        in_specs=[
            pl.BlockSpec((tm, bw), lambda i: (i, 0)),
            pl.BlockSpec((tm, bw), lambda i: (i, 0)),
            pl.BlockSpec((tm, d), lambda i: (i, 0)),
            pl.BlockSpec((tm, d), lambda i: (i, 1)),
            pl.BlockSpec((tm, d), lambda i: (i, 0)),
            pl.BlockSpec((None, bw, d), lambda i: (0, 0, 0), **const),
            pl.BlockSpec((None, bw, d), lambda i: (1, 0, 0), **const),
            pl.BlockSpec((d, d), lambda i: (0, 0), **const),
        ],
        out_specs=pl.BlockSpec((tm, d), lambda i: (i, 0)),
        out_shape=jax.ShapeDtypeStruct((t, d), F32),
        compiler_params=_cparams(("parallel",)),
        name="merge",
    )(ya, yd, proj, proj, x2, wb, wb, wo)


def _memkv_kernel(mem_ref, g_ref, w_ref, k_ref, v_ref):
    mn = _rms(mem_ref[...], g_ref[...]).astype(BF16)
    kv = jnp.dot(mn, w_ref[...], preferred_element_type=F32)
    k_ref[...] = kv[:, :X_WIDTH].astype(k_ref.dtype)
    v_ref[...] = kv[:, X_WIDTH:].astype(v_ref.dtype)


def _memkv(mem, g, w):
    b, m, d = mem.shape
    out = jax.ShapeDtypeStruct((b, m, X_WIDTH), BF16)
    return pl.pallas_call(
        _memkv_kernel,
        grid=(b,),
        in_specs=[
            pl.BlockSpec((None, m, d), lambda i: (i, 0, 0)),
            pl.BlockSpec((1, d), lambda i: (0, 0)),
            pl.BlockSpec((d, 2 * X_WIDTH), lambda i: (0, 0)),
        ],
        out_specs=[pl.BlockSpec((None, m, X_WIDTH), lambda i: (i, 0, 0))] * 2,
        out_shape=[out, out],
        compiler_params=_cparams(("parallel",)),
        name="memkv",
    )(mem, g, w)


def _cross_router_kernel(h_ref, k_ref, v_ref, gc_ref, wq_ref, wo_ref, gf_ref, wr_ref, br_ref,
                         h2_ref, xn_ref, idx_ref, gate_ref, rank_ref, cnt_ref, run_ref):
    i = pl.program_id(0)
    tm, d = h_ref.shape

    @pl.when(i == 0)
    def _():
        run_ref[...] = jnp.zeros(run_ref.shape, F32)

    h = h_ref[...]
    hn = _rms(h, gc_ref[...]).astype(BF16)
    q = jnp.dot(hn, wq_ref[...], preferred_element_type=F32).astype(BF16)
    k = k_ref[...]
    v = v_ref[...]
    outs = []
    for hd in range(X_HEADS):
        sl = slice(X_HEAD_DIM * hd, X_HEAD_DIM * (hd + 1))
        s = lax.dot_general(q[:, sl], k[:, sl], (((1,), (1,)), ((), ())),
                            preferred_element_type=F32) * (X_HEAD_DIM ** -0.5)
        s = s - jnp.max(s, axis=-1, keepdims=True)
        p = jnp.exp(s)
        p = p / jnp.sum(p, axis=-1, keepdims=True)
        outs.append(jnp.dot(p.astype(BF16), v[:, sl], preferred_element_type=F32))
    o = jnp.concatenate(outs, axis=1).astype(BF16)
    h2 = h + jnp.dot(o, wo_ref[...], preferred_element_type=F32)
    h2_ref[...] = h2

    xn = _rms(h2, gf_ref[...])
    packed = _pack_pair(xn[:, :d // 2], xn[:, d // 2:])
    for j in range(V7X_SUBLANES):
        xn_ref[pl.ds(j, tm, stride=V7X_SUBLANES), :] = packed[:, j * V7X_LANES:(j + 1) * V7X_LANES]
    logits = jnp.dot(xn.astype(BF16), wr_ref[...], preferred_element_type=F32) + br_ref[...]

    lane = lax.broadcasted_iota(jnp.int32, logits.shape, 1).astype(F32)
    work = logits
    vals, idxs = [], []
    onehot = jnp.zeros(logits.shape, F32)
    for _ in range(TOP_K):
        mx = jnp.max(work, axis=-1, keepdims=True)
        ix = jnp.min(jnp.where(work == mx, lane, float(N_EXPERTS)), axis=-1, keepdims=True)
        sel = lane == ix
        vals.append(mx)
        idxs.append(ix)
        onehot = jnp.where(sel, 1.0, onehot)
        work = jnp.where(sel, NEG_INF, work)
    es = [jnp.exp(vv - vals[0]) for vv in vals]
    tot = es[0] + es[1] + es[2] + es[3]

    r = lax.broadcasted_iota(jnp.int32, (tm, tm), 0)
    c = lax.broadcasted_iota(jnp.int32, (tm, tm), 1)
    lower = jnp.where(c < r, 1.0, 0.0).astype(BF16)
    prefix = jnp.dot(lower, onehot.astype(BF16), preferred_element_type=F32) + run_ref[...]
    for kk in range(TOP_K):
        idx_ref[:, kk:kk + 1] = idxs[kk].astype(jnp.int32)
        gate_ref[:, kk:kk + 1] = es[kk] / tot
        rk = jnp.sum(jnp.where(lane == idxs[kk], prefix, 0.0), axis=-1, keepdims=True)
        rank_ref[:, kk:kk + 1] = rk.astype(jnp.int32)
    run_ref[...] = run_ref[...] + jnp.sum(onehot, axis=0, keepdims=True)
    cnt_ref[...] = run_ref[...].astype(jnp.int32)


def _cross_router(h1, kmem, vmem, gc, wq, wo, gf, wr, br, seq):
    t, d = h1.shape
    tm = min(CROSS_TM, seq)
    per_b = seq // tm
    m = kmem.shape[1]
    const = dict(pipeline_mode=pl.Buffered(1))
    outs = pl.pallas_call(
        _cross_router_kernel,
        grid=(t // tm,),
        in_specs=[
            pl.BlockSpec((tm, d), lambda i: (i, 0)),
            pl.BlockSpec((None, m, X_WIDTH), lambda i: (i // per_b, 0, 0)),
            pl.BlockSpec((None, m, X_WIDTH), lambda i: (i // per_b, 0, 0)),
            pl.BlockSpec((1, d), lambda i: (0, 0)),
            pl.BlockSpec((d, X_WIDTH), lambda i: (0, 0), **const),
            pl.BlockSpec((X_WIDTH, d), lambda i: (0, 0), **const),
            pl.BlockSpec((1, d), lambda i: (0, 0)),
            pl.BlockSpec((d, N_EXPERTS), lambda i: (0, 0), **const),
            pl.BlockSpec((1, N_EXPERTS), lambda i: (0, 0)),
        ],
        out_specs=[
            pl.BlockSpec((tm, d), lambda i: (i, 0)),
            pl.BlockSpec((tm * V7X_SUBLANES, V7X_LANES), lambda i: (i, 0)),
            pl.BlockSpec((tm, TOP_K), lambda i: (i, 0)),
            pl.BlockSpec((tm, TOP_K), lambda i: (i, 0)),
            pl.BlockSpec((tm, TOP_K), lambda i: (i, 0)),
            pl.BlockSpec((1, N_EXPERTS), lambda i: (0, 0)),
        ],
        out_shape=[
            jax.ShapeDtypeStruct((t, d), F32),
            jax.ShapeDtypeStruct((t * V7X_SUBLANES, V7X_LANES), U32),
            jax.ShapeDtypeStruct((t, TOP_K), jnp.int32),
            jax.ShapeDtypeStruct((t, TOP_K), F32),
            jax.ShapeDtypeStruct((t, TOP_K), jnp.int32),
            jax.ShapeDtypeStruct((1, N_EXPERTS), jnp.int32),
        ],
        scratch_shapes=[pltpu.VMEM((1, N_EXPERTS), F32)],
        compiler_params=_cparams(("arbitrary",)),
        name="cross_router",
    )(h1, kmem, vmem, gc, wq, wo, gf, wr, br)
    return outs


def _tile_rows(row):
    return pl.ds(pl.multiple_of(row * V7X_SUBLANES, V7X_SUBLANES), V7X_SUBLANES)


def _dispatch_kernel(pos_ref, tail_ref, x_ref, xs_ref, zero_ref, sem, zsem):
    i = pl.program_id(0)
    tm = x_ref.shape[0] // V7X_SUBLANES
    chunk_rows = zero_ref.shape[0]

    @pl.when(i == 0)
    def _():
        zero_ref[...] = jnp.zeros(zero_ref.shape, zero_ref.dtype)

        def tail_copy(e):
            row0 = pl.multiple_of(tail_ref[e] * V7X_SUBLANES, V7X_SUBLANES)
            return pltpu.make_async_copy(zero_ref, xs_ref.at[pl.ds(row0, chunk_rows)], zsem)

        for e in range(N_EXPERTS):
            @pl.when(tail_ref[e] >= 0)
            def _():
                tail_copy(e).start()
        for e in range(N_EXPERTS):
            @pl.when(tail_ref[e] >= 0)
            def _():
                tail_copy(e).wait()

    def row_copy(r, p):
        return pltpu.make_async_copy(x_ref.at[_tile_rows(r)], xs_ref.at[_tile_rows(p)], sem)

    def issue(r, carry):
        for kk in range(TOP_K):
            row_copy(r, pos_ref[0, 0, r * TOP_K + kk]).start(priority=kk % 2)
        return carry

    lax.fori_loop(0, tm, issue, 0, unroll=DMA_ISSUE_UNROLL)
    for _ in range(tm * TOP_K):
        row_copy(0, 0).wait()


def _dispatch(xn, pos, tails, n_rows):
    t = xn.shape[0] // V7X_SUBLANES
    tm = min(DISPATCH_TM, t)
    pos3 = pos.reshape(t // tm, 1, tm * TOP_K)
    return pl.pallas_call(
        _dispatch_kernel,
        grid=(t // tm,),
        in_specs=[
            pl.BlockSpec((1, 1, tm * TOP_K), lambda i: (i, 0, 0), memory_space=pltpu.SMEM),
            pl.BlockSpec(memory_space=pltpu.SMEM),
            pl.BlockSpec((tm * V7X_SUBLANES, V7X_LANES), lambda i: (i, 0)),
        ],
        out_specs=pl.BlockSpec(memory_space=pl.ANY),
        out_shape=jax.ShapeDtypeStruct((n_rows * V7X_SUBLANES, V7X_LANES), xn.dtype),
        scratch_shapes=[pltpu.VMEM((EXPERT_CHUNK * V7X_SUBLANES, V7X_LANES), xn.dtype),
                        pltpu.SemaphoreType.DMA(()), pltpu.SemaphoreType.DMA(())],
        compiler_params=_cparams(("arbitrary",)),
        name="dispatch",
    )(pos3, tails, xn)


def _expert_kernel(ve_ref, vr_ref, vb_ref, xs_ref, wg_ref, wl_ref, bg_ref, bl_ref, wd_ref, bd_ref, ys_ref,
                   x_ref, y_ref, act_ref, wa_ref, wb_ref, xsem, ysem):
    del ve_ref
    v = pl.program_id(0)
    s = pl.program_id(1)
    rows = vr_ref[v]
    nj = act_ref.shape[0]
    tf = act_ref.shape[2]
    ch = EXPERT_CHUNK
    n_chunks = x_ref.shape[0] // (ch * V7X_SUBLANES)
    half = tf // 2
    last_step = pl.num_programs(1) - 1
    sub = V7X_SUBLANES

    def x_copy(visit):
        row0 = pl.multiple_of(vb_ref[visit] * x_ref.shape[0], V7X_SUBLANES)
        return pltpu.make_async_copy(xs_ref.at[pl.ds(row0, x_ref.shape[0])], x_ref, xsem)

    def y_copy(visit):
        row0 = pl.multiple_of(vb_ref[visit] * y_ref.shape[0], V7X_SUBLANES)
        return pltpu.make_async_copy(y_ref, ys_ref.at[pl.ds(row0, y_ref.shape[0])], ysem)

    @pl.when((s == nj) & (v > 0))
    def _():
        y_copy(v - 1).wait()

    @pl.when((v == 0) & (s == 0))
    def _():
        x_copy(0).start()

    @pl.when(s == 0)
    def _():
        x_copy(v).wait()

    @pl.when((s == nj) & (v + 1 < pl.num_programs(0)))
    def _():
        x_copy(v + 1).start()

    @pl.when((rows > 0) & (s < nj))
    def _():
        wa_ref[...] = wg_ref[...].astype(BF16)
        wb_ref[...] = wl_ref[...].astype(BF16)
        for c in range(n_chunks):
            @pl.when(c * ch < rows)
            def _():
                xw = [x_ref[pl.ds(c * ch * V7X_SUBLANES + j, ch, stride=V7X_SUBLANES), :]
                      for j in range(V7X_SUBLANES)]
                xb = jnp.concatenate([_unpack_lo(u).astype(BF16) for u in xw]
                                     + [_unpack_hi(u).astype(BF16) for u in xw], axis=1)
                glu = jnp.dot(xb, wa_ref[...], preferred_element_type=F32) + bg_ref[...]
                lin = jnp.dot(xb, wb_ref[...], preferred_element_type=F32) + bl_ref[...]
                glu = jnp.minimum(glu, SWIGLU_LIMIT)
                lin = jnp.clip(lin, -SWIGLU_LIMIT, SWIGLU_LIMIT)
                act = glu * jax.nn.sigmoid(SWIGLU_ALPHA * glu) * (lin + 1.0)
                act_ref[s, pl.ds(c * ch, ch), :] = act.astype(BF16)

    @pl.when((rows > 0) & (s >= nj))
    def _():
        wa_ref[...] = wd_ref[...].astype(BF16)
        for c in range(n_chunks):
            sl = pl.ds(c * ch, ch)

            @pl.when(c * ch < rows)
            def _():
                acc = jnp.dot(act_ref[0, sl, :], wa_ref[0:tf, :], preferred_element_type=F32)
                for j in range(1, nj):
                    acc = acc + jnp.dot(act_ref[j, sl, :], wa_ref[j * tf:(j + 1) * tf, :],
                                        preferred_element_type=F32)
                acc = acc + bd_ref[...]
                packed = _pack_pair(acc[:, :half], acc[:, half:])
                for jj in range(half // V7X_LANES):
                    tile_sub = (s - nj) * (half // V7X_LANES) + jj
                    y_ref[pl.ds(c * ch * sub + tile_sub, ch, stride=sub), :] = (
                        packed[:, jj * V7X_LANES:(jj + 1) * V7X_LANES])

            @pl.when((c * ch >= rows) & (s == nj))
            def _():
                y_ref[pl.ds(c * ch * sub, ch * sub), :] = jnp.zeros((ch * sub, V7X_LANES), y_ref.dtype)

    @pl.when(s == last_step)
    def _():
        y_copy(v).start()

    @pl.when((s == last_step) & (v == pl.num_programs(0) - 1))
    def _():
        y_copy(v).wait()


def _experts(xs, visit_expert, visit_rows, visit_blk, n_used, w_gu, b_gu, w_down, b_down):
    n_rows = xs.shape[0] // V7X_SUBLANES
    e, d, ff2 = w_gu.shape
    ff = ff2 // 2
    r, tf = EXPERT_ROWS, EXPERT_TF
    nj = ff // tf
    nn = d // tf
    b_gu3 = b_gu.reshape(e, 1, ff2)
    b_down3 = b_down.reshape(e, 1, d)

    def up(v, s, vr):
        return jnp.where(vr[v] > 0, jnp.minimum(s, nj - 1), nj - 1)

    def down(v, s, vr):
        return jnp.where(vr[v] > 0, jnp.maximum(s - nj, 0), nn - 1)

    grid_spec = pltpu.PrefetchScalarGridSpec(
        num_scalar_prefetch=3,
        grid=(n_used, nj + nn),
        in_specs=[
            pl.BlockSpec(memory_space=pl.ANY),
            pl.BlockSpec((None, d, tf), lambda v, s, ve, vr, vb: (ve[v], 0, up(v, s, vr))),
            pl.BlockSpec((None, d, tf), lambda v, s, ve, vr, vb: (ve[v], 0, nj + up(v, s, vr))),
            pl.BlockSpec((None, 1, tf), lambda v, s, ve, vr, vb: (ve[v], 0, up(v, s, vr))),
            pl.BlockSpec((None, 1, tf), lambda v, s, ve, vr, vb: (ve[v], 0, nj + up(v, s, vr))),
            pl.BlockSpec((None, ff, tf), lambda v, s, ve, vr, vb: (ve[v], 0, down(v, s, vr))),
            pl.BlockSpec((None, 1, tf), lambda v, s, ve, vr, vb: (ve[v], 0, down(v, s, vr))),
        ],
        out_specs=pl.BlockSpec(memory_space=pl.ANY),
        scratch_shapes=[
            pltpu.VMEM((r * V7X_SUBLANES, V7X_LANES), xs.dtype),
            pltpu.VMEM((r * V7X_SUBLANES, V7X_LANES), U32),
            pltpu.VMEM((nj, r, tf), BF16),
            pltpu.VMEM((d, tf), BF16),
            pltpu.VMEM((d, tf), BF16),
            pltpu.SemaphoreType.DMA(()),
            pltpu.SemaphoreType.DMA(()),
        ],
    )
    return pl.pallas_call(
        _expert_kernel,
        grid_spec=grid_spec,
        out_shape=jax.ShapeDtypeStruct((n_rows * V7X_SUBLANES, V7X_LANES), U32),
        compiler_params=pltpu.CompilerParams(dimension_semantics=("arbitrary", "arbitrary"),
                                             vmem_limit_bytes=EXPERT_VMEM_LIMIT),
        name="experts",
    )(visit_expert, visit_rows, visit_blk, xs, w_gu, w_gu, b_gu3, b_gu3, w_down, b_down3)


def _combine_kernel(pos_ref, h_ref, gate_ref, g_ref, ys_ref, o_ref, buf_ref, sem):
    tm = h_ref.shape[0]
    subs_per_half = EXPERT_TF // 2 // V7X_LANES

    def row_copy(r, kk, p):
        return pltpu.make_async_copy(ys_ref.at[_tile_rows(p)], buf_ref.at[kk, _tile_rows(r)], sem)

    def issue(r, carry):
        for kk in range(TOP_K):
            row_copy(r, kk, pos_ref[0, 0, r * TOP_K + kk]).start(priority=kk % 2)
        return carry

    lax.fori_loop(0, tm, issue, 0, unroll=DMA_ISSUE_UNROLL)
    for _ in range(tm * TOP_K):
        row_copy(0, 0, 0).wait()

    gates = gate_ref[...]
    y = h_ref[...]
    for kk in range(TOP_K):
        words = [buf_ref[kk, pl.ds(j, tm, stride=V7X_SUBLANES), :] for j in range(V7X_SUBLANES)]
        pieces = []
        for n in range(V7X_SUBLANES // subs_per_half):
            chunk = words[n * subs_per_half:(n + 1) * subs_per_half]
            pieces += [_unpack_lo(u) for u in chunk] + [_unpack_hi(u) for u in chunk]
        y = y + gates[:, kk:kk + 1] * jnp.concatenate(pieces, axis=1)
    o_ref[...] = _rms(y, g_ref[...])


def _combine(h2, gates, pos, ys, g_final):
    t, d = h2.shape
    tm = min(COMBINE_TM, t)
    pos3 = pos.reshape(t // tm, 1, tm * TOP_K)
    return pl.pallas_call(
        _combine_kernel,
        grid=(t // tm,),
        in_specs=[
            pl.BlockSpec((1, 1, tm * TOP_K), lambda i: (i, 0, 0), memory_space=pltpu.SMEM),
            pl.BlockSpec((tm, d), lambda i: (i, 0)),
            pl.BlockSpec((tm, TOP_K), lambda i: (i, 0)),
            pl.BlockSpec((1, d), lambda i: (0, 0)),
            pl.BlockSpec(memory_space=pl.ANY),
        ],
        out_specs=pl.BlockSpec((tm, d), lambda i: (i, 0)),
        out_shape=jax.ShapeDtypeStruct((t, d), F32),
        scratch_shapes=[pltpu.VMEM((TOP_K, tm * V7X_SUBLANES, V7X_LANES), ys.dtype), pltpu.SemaphoreType.DMA(())],
        compiler_params=_cparams(("arbitrary",)),
        name="combine",
    )(pos3, h2, gates, g_final, ys)


def _routing_tables(counts, top_idx, rank, n_visits):
    r, ch = EXPERT_ROWS, EXPERT_CHUNK
    counts = counts.reshape(N_EXPERTS)
    padded = (counts + r - 1) // r * r
    ends = jnp.cumsum(padded)
    starts = ends - padded
    experts = jnp.arange(N_EXPERTS, dtype=jnp.int32)
    start_of = jnp.sum(jnp.where(top_idx[..., None] == experts, starts, 0), axis=-1)
    pos = start_of + rank
    visit_row0 = jnp.arange(n_visits, dtype=jnp.int32) * r
    owner = jnp.sum((ends[None, :] <= visit_row0[:, None]).astype(jnp.int32), axis=1)
    visit_expert = jnp.minimum(owner, N_EXPERTS - 1).astype(jnp.int32)
    visit_rows = jnp.clip(counts[visit_expert] - (visit_row0 - starts[visit_expert]), 0, r).astype(jnp.int32)
    n_used = (ends[-1] // r).astype(jnp.int32)
    last = jnp.maximum(n_used - 1, 0)
    ids = jnp.arange(n_visits, dtype=jnp.int32)
    active = ids < n_used
    visit_rows = jnp.where(active, visit_rows, 0)
    visit_blk = jnp.where(active, ids, last)
    visit_expert = jnp.where(active, visit_expert, visit_expert[last])
    tail = starts + jnp.minimum(counts // ch * ch, padded - ch)
    tail = jnp.where(counts > 0, tail, -1)
    return pos.astype(jnp.int32), visit_expert, visit_rows, visit_blk, tail.astype(jnp.int32), n_used


def kernel(x, mem, norm_mix_g, w_in, sinks, lambda_q1, lambda_k1, lambda_q2, lambda_k2, diff_subln_g, w_branch, w_o, norm_cross_g, norm_mem_g, w_cq, w_ckv, w_co, norm_ffn_g, w_router, b_router, w_gate_up, b_gate_up, w_down, b_down, norm_final_g):
    b, s, d = x.shape
    t = b * s
    x2 = x.reshape(t, d)
    l = 0

    w = w_in[l]
    o_qa, o_ka, o_va = 0, SWA_WIDTH, SWA_WIDTH + SWA_KV_WIDTH
    o_qd = o_va + SWA_KV_WIDTH
    o_kd, o_vd, o_g = o_qd + DIFF_WIDTH, o_qd + 2 * DIFF_WIDTH, o_qd + 3 * DIFF_WIDTH
    w_perm = jnp.concatenate([
        w[:, o_g:], w[:, o_qd:o_kd] * jnp.float32(HEAD_DIM ** -0.5 * LOG2E), w[:, o_kd:o_vd], w[:, o_vd:o_g],
        w[:, o_qa:o_ka], w[:, o_ka:o_va], w[:, o_va:o_qd]], axis=1).astype(BF16)
    proj = _inproj(x2, norm_mix_g[l].reshape(1, d), w_perm)
    c_qd = 2 * d
    c_kd, c_vd, c_qa = c_qd + DIFF_WIDTH, c_qd + 2 * DIFF_WIDTH, c_qd + 3 * DIFF_WIDTH
    c_ka, c_va = c_qa + SWA_WIDTH, c_qa + SWA_WIDTH + SWA_KV_WIDTH

    def seg(c0, width):
        return proj[:, c0:c0 + width].reshape(b, s, width)

    qaT = jnp.swapaxes(seg(c_qa, SWA_WIDTH), 1, 2)
    vaT = jnp.swapaxes(seg(c_va, SWA_KV_WIDTH), 1, 2)
    yaT = _swa(qaT, proj.reshape(b, s, proj.shape[1]), c_ka, vaT, sinks[l])
    ya = jnp.swapaxes(yaT, 1, 2).reshape(t, SWA_WIDTH)

    td = min(DIFF_T, s)
    nk = s // td
    hw = DIFF_HEAD_WIDTH
    cs, qx, kx = _diff_tables(td)
    qdT = jnp.swapaxes(seg(c_qd, DIFF_WIDTH), 1, 2)
    k4 = proj.reshape(b, nk, td, proj.shape[1])
    v5 = seg(c_vd, DIFF_WIDTH).reshape(b, nk, td, DIFF_HEADS, hw).transpose(0, 3, 1, 4, 2)
    ones = jnp.zeros((DIFF_DENOM_ROWS, td), BF16).at[0].set(1.0)
    lamv = jnp.stack([lambda_q1[l], lambda_k1[l], lambda_q2[l], lambda_k2[l]]).astype(F32)
    ydT = _diff(qdT, k4, c_kd, v5, qx, kx, ones, cs, lamv, diff_subln_g[l].astype(F32).reshape(hw, 1))
    yd = jnp.swapaxes(ydT, 1, 2).reshape(t, DIFF_WIDTH)

    h1 = _merge(ya, yd, proj, x2, w_branch[l].astype(BF16), w_o[l].astype(BF16))

    kmem, vmem = _memkv(mem, norm_mem_g[l].reshape(1, d), w_ckv[l].astype(BF16))
    h2, xn, top_idx, gates, rank, counts = _cross_router(
        h1, kmem, vmem, norm_cross_g[l].reshape(1, d), w_cq[l].astype(BF16), w_co[l].astype(BF16),
        norm_ffn_g[l].reshape(1, d), w_router[l].astype(BF16), b_router[l].reshape(1, N_EXPERTS), s)

    assert d // 2 == V7X_SUBLANES * V7X_LANES
    n_visits = (t * TOP_K) // EXPERT_ROWS + N_EXPERTS
    n_rows = n_visits * EXPERT_ROWS
    pos, visit_expert, visit_rows, visit_blk, tails, n_used = _routing_tables(counts, top_idx, rank, n_visits)
    xs = _dispatch(xn, pos, tails, n_rows)
    ys = _experts(xs, visit_expert, visit_rows, visit_blk, n_used, w_gate_up[l], b_gate_up[l], w_down[l], b_down[l])
    out = _combine(h2, gates, pos, ys, norm_final_g.reshape(1, d))
    return out.reshape(b, s, d)
```

```python
import functools
import math

import jax
import jax.numpy as jnp
from jax import lax
from jax.experimental import pallas as pl
from jax.experimental.pallas import tpu as pltpu

BF16 = jnp.bfloat16
F32 = jnp.float32
U32 = jnp.uint32
NEG_INF = float("-inf")
LOG2E = math.log2(math.e)

HEAD_DIM = 64
SWA_Q_HEADS = 16
SWA_KV_HEADS = 4
SWA_GROUP = SWA_Q_HEADS // SWA_KV_HEADS
SWA_BLOCK = 128
SWA_WIDTH = SWA_Q_HEADS * HEAD_DIM
SWA_KV_WIDTH = SWA_KV_HEADS * HEAD_DIM
DIFF_HEADS = 8
DIFF_HEAD_WIDTH = 2 * HEAD_DIM
DIFF_WIDTH = DIFF_HEADS * DIFF_HEAD_WIDTH
X_HEADS = 4
X_HEAD_DIM = 128
X_WIDTH = X_HEADS * X_HEAD_DIM
N_EXPERTS = 32
TOP_K = 4
SWIGLU_LIMIT = 7.0
SWIGLU_ALPHA = 1.702
RMS_EPS = 1e-5
LAMBDA_INIT = 0.8 - 0.6 * math.exp(-0.3 * 0)

V7X_VMEM_BYTES = 64 * 1024 * 1024
VMEM_LIMIT = V7X_VMEM_BYTES - 8 * 1024 * 1024
EXPERT_VMEM_LIMIT = V7X_VMEM_BYTES - 3 * 1024 * 1024
V7X_LANES = 128
V7X_SUBLANES = 8

INPROJ_TM = 1024
INPROJ_TN = 2176
DIFF_T = 512
DIFF_POS_SPLIT = 256
DIFF_DENOM_ROWS = 16
MERGE_TM = 256
CROSS_TM = 512
DISPATCH_TM = 512
EXPERT_CHUNK = 768
EXPERT_ROWS = 3 * EXPERT_CHUNK
EXPERT_TF = 512
COMBINE_TM = 512
DMA_ISSUE_UNROLL = 8


def _cparams(semantics):
    return pltpu.CompilerParams(dimension_semantics=semantics, vmem_limit_bytes=VMEM_LIMIT)


def _rms(x, g):
    ms = jnp.mean(x * x, axis=-1, keepdims=True)
    return x * lax.rsqrt(ms + RMS_EPS) * g


def _pack_pair(lo, hi):
    lo_bits = lax.bitcast_convert_type(lo.astype(BF16).astype(F32), U32)
    hi_bits = lax.bitcast_convert_type(hi.astype(BF16).astype(F32), U32)
    return hi_bits | (lo_bits >> 16)


def _unpack_lo(w):
    return lax.bitcast_convert_type(w << 16, F32)


def _unpack_hi(w):
    return lax.bitcast_convert_type(w & jnp.uint32(0xFFFF0000), F32)


def _inproj_kernel(x_ref, g_ref, w_ref, o_ref, xn_ref):
    @pl.when(pl.program_id(1) == 0)
    def _():
        xn_ref[...] = _rms(x_ref[...], g_ref[...]).astype(BF16)

    o_ref[...] = jnp.dot(xn_ref[...], w_ref[...], preferred_element_type=F32).astype(o_ref.dtype)


def _inproj(x2, g, w):
    t, d = x2.shape
    n = w.shape[1]
    tm, tn = min(INPROJ_TM, t), INPROJ_TN
    return pl.pallas_call(
        _inproj_kernel,
        grid=(t // tm, n // tn),
        in_specs=[
            pl.BlockSpec((tm, d), lambda i, j: (i, 0)),
            pl.BlockSpec((1, d), lambda i, j: (0, 0)),
            pl.BlockSpec((d, tn), lambda i, j: (0, j)),
        ],
        out_specs=pl.BlockSpec((tm, tn), lambda i, j: (i, j)),
        out_shape=jax.ShapeDtypeStruct((t, n), BF16),
        scratch_shapes=[pltpu.VMEM((tm, d), BF16)],
        compiler_params=_cparams(("parallel", "arbitrary")),
        name="inproj",
    )(x2, g, w)


def _swa_kernel(qT_ref, kp_ref, kc_ref, vp_ref, vc_ref, bias_ref, sink_ref, o_ref):
    n = pl.program_id(1)
    blk = SWA_BLOCK
    kband = jnp.concatenate([kp_ref[...], kc_ref[...]], axis=0)
    vband = jnp.concatenate([vp_ref[...], vc_ref[...]], axis=1)
    krow = lax.broadcasted_iota(jnp.int32, (2 * blk, SWA_GROUP * blk), 0)
    has_prev = n > 0
    zeros = jnp.zeros((HEAD_DIM, blk), BF16)
    for h in range(SWA_KV_HEADS):
        lane0 = 2 * HEAD_DIM * (h // 2)
        k128 = kband[:, lane0:lane0 + 2 * HEAD_DIM]
        pieces = []
        for g in range(SWA_GROUP):
            hq = SWA_GROUP * h + g
            qh = qT_ref[HEAD_DIM * hq:HEAD_DIM * (hq + 1), :]
            pieces.append(jnp.concatenate([qh, zeros] if h % 2 == 0 else [zeros, qh], axis=0))
        qz = jnp.concatenate(pieces, axis=1)
        s = jnp.dot(k128, qz, preferred_element_type=F32) * (HEAD_DIM ** -0.5)
        s = s + bias_ref[h]
        s = jnp.where((krow >= blk) | has_prev, s, NEG_INF)
        sink = sink_ref[h]
        m = jnp.maximum(jnp.max(s, axis=0, keepdims=True), sink)
        p = jnp.exp(s - m)
        denom = jnp.sum(p, axis=0, keepdims=True) + jnp.exp(sink - m)
        vh = vband[HEAD_DIM * h:HEAD_DIM * (h + 1), :]
        o = jnp.dot(vh, p.astype(BF16), preferred_element_type=F32) / denom
        for g in range(SWA_GROUP):
            hq = SWA_GROUP * h + g
            o_ref[HEAD_DIM * hq:HEAD_DIM * (hq + 1), :] = o[:, blk * g:blk * (g + 1)].astype(o_ref.dtype)


def _swa_tables(sinks):
    blk = SWA_BLOCK
    slopes = (2.0 ** (-8.0 * jnp.arange(1, SWA_Q_HEADS + 1, dtype=F32) / SWA_Q_HEADS)).reshape(SWA_KV_HEADS, SWA_GROUP)
    q_pos = jnp.arange(blk)[None, :] + blk
    k_pos = jnp.arange(2 * blk)[:, None]
    dist = q_pos - k_pos
    allowed = (dist >= 0) & (dist < SWA_BLOCK)
    bias = -slopes[:, :, None, None] * dist.astype(F32)[None, None]
    bias = jnp.where(allowed[None, None], bias, NEG_INF)
    bias = jnp.transpose(bias, (0, 2, 1, 3)).reshape(SWA_KV_HEADS, 2 * blk, SWA_GROUP * blk)
    sink = sinks.astype(F32).reshape(SWA_KV_HEADS, SWA_GROUP, 1)
    sink = jnp.broadcast_to(sink, (SWA_KV_HEADS, SWA_GROUP, blk)).reshape(SWA_KV_HEADS, 1, SWA_GROUP * blk)
    return bias, sink


def _swa(qT, k, k_col0, vT, sinks):
    b, _, s = qT.shape
    blk = SWA_BLOCK
    bias, sink = _swa_tables(sinks)
    prev = lambda n: jnp.maximum(n - 1, 0)
    kb = k_col0 // SWA_KV_WIDTH
    return pl.pallas_call(
        _swa_kernel,
        grid=(b, s // blk),
        in_specs=[
            pl.BlockSpec((None, SWA_WIDTH, blk), lambda bi, n: (bi, 0, n)),
            pl.BlockSpec((None, blk, SWA_KV_WIDTH), lambda bi, n: (bi, prev(n), kb)),
            pl.BlockSpec((None, blk, SWA_KV_WIDTH), lambda bi, n: (bi, n, kb)),
            pl.BlockSpec((None, SWA_KV_WIDTH, blk), lambda bi, n: (bi, 0, prev(n))),
            pl.BlockSpec((None, SWA_KV_WIDTH, blk), lambda bi, n: (bi, 0, n)),
            pl.BlockSpec(bias.shape, lambda bi, n: (0, 0, 0)),
            pl.BlockSpec(sink.shape, lambda bi, n: (0, 0, 0)),
        ],
        out_specs=pl.BlockSpec((None, SWA_WIDTH, blk), lambda bi, n: (bi, 0, n)),
        out_shape=jax.ShapeDtypeStruct((b, SWA_WIDTH, s), BF16),
        compiler_params=_cparams(("parallel", "arbitrary")),
        name="swa",
    )(qT, k, k, vT, vT, bias, sink)


def _diff_kernel(c_ref, qT_ref, qx_ref, k_ref, kx_ref, v_ref, ones_ref, lamv_ref, g_ref, o_ref,
                 q2_ref, z_ref, zm_ref, p_ref, a_ref, m_ref, acc_ref, *, t):
    h = pl.program_id(1)
    qi = pl.program_id(2)
    w = DIFF_HEAD_WIDTH
    c = c_ref[h]

    q2_ref[w:, :] = qx_ref[...]
    qt = qT_ref[...]
    row = lax.broadcasted_iota(jnp.int32, qt.shape, 0)
    zero = jnp.zeros_like(qt)
    q2_ref[:w, :t] = jnp.where(row < HEAD_DIM, qt, zero)
    q2_ref[:w, t:] = jnp.where(row >= HEAD_DIM, qt, zero)
    m_ref[...] = jnp.full(m_ref.shape, NEG_INF, F32)
    acc_ref[...] = jnp.zeros(acc_ref.shape, F32)
    p_ref[1] = jnp.zeros(p_ref.shape[1:], BF16)
    a_ref[1] = jnp.ones(a_ref.shape[1:], F32)

    def scores(k, slot):
        keys = jnp.concatenate([k_ref[k], kx_ref[...]], axis=1)
        z = jnp.dot(keys, q2_ref[...], preferred_element_type=F32)
        z_ref[slot] = z
        zm_ref[slot] = jnp.max(z, axis=0, keepdims=True)

    def values(k, slot):
        vals = jnp.concatenate([v_ref[jnp.maximum(k, 0)], ones_ref[...]], axis=0)
        pv = jnp.dot(vals, p_ref[slot], preferred_element_type=F32)
        acc_ref[...] = a_ref[slot] * acc_ref[...] + pv

    def softmax(k, slot, masked):
        z = z_ref[slot]
        if masked:
            kr = lax.broadcasted_iota(jnp.int32, z.shape, 0)
            qc = lax.broadcasted_iota(jnp.int32, z.shape, 1)
            qc = jnp.where(qc >= t, qc - t, qc)
            z = jnp.where(kr <= qc, z, NEG_INF)
            zmax = jnp.max(z, axis=0, keepdims=True)
        else:
            zmax = zm_ref[slot]
        off = c * ((k - qi) * t).astype(F32)
        m_old = m_ref[...]
        m_new = jnp.maximum(m_old, zmax + off)
        p_ref[slot] = jnp.exp2(z - (m_new - off)).astype(BF16)
        a_ref[slot] = jnp.exp2(m_old - m_new)
        m_ref[...] = m_new

    scores(0, 0)

    def pair(j, carry):
        k = 2 * j
        scores(k + 1, 1)
        values(k - 1, 1)
        softmax(k, 0, False)
        scores(k + 2, 0)
        values(k, 0)
        softmax(k + 1, 1, False)
        return carry

    lax.fori_loop(0, qi >> 1, pair, 0)

    @pl.when((qi & 1) == 0)
    def _():
        values(qi - 1, 1)
        softmax(qi, 0, True)
        values(qi, 0)

    @pl.when((qi & 1) == 1)
    def _():
        scores(qi, 1)
        values(qi - 2, 1)
        softmax(qi - 1, 0, False)
        values(qi - 1, 0)
        softmax(qi, 1, True)
        values(qi, 1)

    lamv = lamv_ref[...]
    lam = (jnp.exp(jnp.sum(lamv[0:1] * lamv[1:2], axis=-1, keepdims=True))
           - jnp.exp(jnp.sum(lamv[2:3] * lamv[3:4], axis=-1, keepdims=True)) + LAMBDA_INIT)
    acc = acc_ref[...]
    l = acc[w:w + 1, :]
    o = acc[:w, :t] / l[:, :t] - lam * (acc[:w, t:] / l[:, t:])
    ms = jnp.mean(o * o, axis=0, keepdims=True)
    y = o * lax.rsqrt(ms + RMS_EPS) * g_ref[...] * (1.0 - LAMBDA_INIT)
    o_ref[...] = y.astype(o_ref.dtype)


def _diff_tables(t):
    w = DIFF_HEAD_WIDTH
    slopes = 2.0 ** (-8.0 * jnp.arange(1, DIFF_HEADS + 1, dtype=F32) / DIFF_HEADS)
    c = slopes * jnp.float32(LOG2E)
    c1 = c.astype(BF16)
    c2 = (c - c1.astype(F32)).astype(BF16)
    c3 = (c - c1.astype(F32) - c2.astype(F32)).astype(BF16)
    parts = jnp.stack([c1, c2, c3, c1, c2, c3], axis=1)
    qx = jnp.zeros((DIFF_HEADS, w, 2 * t), BF16).at[:, :6, :].set(parts[:, :, None])
    r = jnp.arange(t)
    r_lo = (r % DIFF_POS_SPLIT).astype(BF16)
    r_hi = ((r // DIFF_POS_SPLIT) * DIFF_POS_SPLIT).astype(BF16)
    kx = jnp.zeros((t, w), BF16).at[:, 0:3].set(r_lo[:, None]).at[:, 3:6].set(r_hi[:, None])
    return c, qx, kx


def _diff(qT, k4, k_col0, v5, qx, kx, ones, cs, lamv, subln_g):
    b, _, s = qT.shape
    t = k4.shape[2]
    nk = s // t
    w = DIFF_HEAD_WIDTH
    wv = w + ones.shape[0]
    k_blk0 = k_col0 // w
    return pl.pallas_call(
        functools.partial(_diff_kernel, t=t),
        grid=(b, DIFF_HEADS, nk),
        in_specs=[
            pl.BlockSpec(memory_space=pltpu.SMEM),
            pl.BlockSpec((None, w, t), lambda bi, h, qi: (bi, h, qi)),
            pl.BlockSpec((None, w, 2 * t), lambda bi, h, qi: (h, 0, 0)),
            pl.BlockSpec((None, nk, t, w), lambda bi, h, qi: (bi, 0, 0, k_blk0 + h)),
            pl.BlockSpec((t, w), lambda bi, h, qi: (0, 0)),
            pl.BlockSpec((None, None, nk, w, t), lambda bi, h, qi: (bi, h, 0, 0, 0)),
            pl.BlockSpec(ones.shape, lambda bi, h, qi: (0, 0)),
            pl.BlockSpec((4, HEAD_DIM), lambda bi, h, qi: (0, 0)),
            pl.BlockSpec((w, 1), lambda bi, h, qi: (0, 0)),
        ],
        out_specs=pl.BlockSpec((None, w, t), lambda bi, h, qi: (bi, h, qi)),
        out_shape=jax.ShapeDtypeStruct((b, DIFF_WIDTH, s), BF16),
        scratch_shapes=[
            pltpu.VMEM((2 * w, 2 * t), BF16),
            pltpu.VMEM((2, t, 2 * t), F32),
            pltpu.VMEM((2, 1, 2 * t), F32),
            pltpu.VMEM((2, t, 2 * t), BF16),
            pltpu.VMEM((2, 1, 2 * t), F32),
            pltpu.VMEM((1, 2 * t), F32),
            pltpu.VMEM((wv, 2 * t), F32),
        ],
        compiler_params=_cparams(("arbitrary", "arbitrary", "arbitrary")),
        name="diffattn",
    )(cs, qT, qx, k4, kx, v5, ones, lamv, subln_g)


def _merge_kernel(ya_ref, yd_ref, g0_ref, g1_ref, x_ref, wb0_ref, wb1_ref, wo_ref, o_ref):
    ua = jnp.dot(ya_ref[...], wb0_ref[...], preferred_element_type=F32)
    ud = jnp.dot(yd_ref[...], wb1_ref[...], preferred_element_type=F32)
    merged = jax.nn.sigmoid(g0_ref[...].astype(F32)) * ua + jax.nn.sigmoid(g1_ref[...].astype(F32)) * ud
    o_ref[...] = x_ref[...] + jnp.dot(merged.astype(BF16), wo_ref[...], preferred_element_type=F32)


def _merge(ya, yd, proj, x2, wb, wo):
    t, d = x2.shape
    tm = min(MERGE_TM, t)
    bw = ya.shape[1]
    const = dict(pipeline_mode=pl.Buffered(1))
    return pl.pallas_call(
        _merge_kernel,
        grid=(t // tm,),
        in_specs=[
            pl.BlockSpec((tm, bw), lambda i: (i, 0)),
            pl.BlockSpec((tm, bw), lambda i: (i, 0)),
            pl.BlockSpec((tm, d), lambda i: (i, 0)),
            pl.BlockSpec((tm, d), lambda i: (i, 1)),
            pl.BlockSpec((tm, d), lambda i: (i, 0)),
            pl.BlockSpec((None, bw, d), lambda i: (0, 0, 0), **const),
            pl.BlockSpec((None, bw, d), lambda i: (1, 0, 0), **const),
            pl.BlockSpec((d, d), lambda i: (0, 0), **const),
        ],
        out_specs=pl.BlockSpec((tm, d), lambda i: (i, 0)),
        out_shape=jax.ShapeDtypeStruct((t, d), F32),
        compiler_params=_cparams(("parallel",)),
        name="merge",
    )(ya, yd, proj, proj, x2, wb, wb, wo)


def _memkv_kernel(mem_ref, g_ref, w_ref, k_ref, v_ref):
    mn = _rms(mem_ref[...], g_ref[...]).astype(BF16)
    kv = jnp.dot(mn, w_ref[...], preferred_element_type=F32)
    k_ref[...] = kv[:, :X_WIDTH].astype(k_ref.dtype)
    v_ref[...] = kv[:, X_WIDTH:].astype(v_ref.dtype)


def _memkv(mem, g, w):
    b, m, d = mem.shape
    out = jax.ShapeDtypeStruct((b, m, X_WIDTH), BF16)
    return pl.pallas_call(
        _memkv_kernel,
        grid=(b,),
        in_specs=[
            pl.BlockSpec((None, m, d), lambda i: (i, 0, 0)),
            pl.BlockSpec((1, d), lambda i: (0, 0)),
            pl.BlockSpec((d, 2 * X_WIDTH), lambda i: (0, 0)),
        ],
        out_specs=[pl.BlockSpec((None, m, X_WIDTH), lambda i: (i, 0, 0))] * 2,
        out_shape=[out, out],
        compiler_params=_cparams(("parallel",)),
        name="memkv",
    )(mem, g, w)


def _cross_router_kernel(h_ref, k_ref, v_ref, gc_ref, wq_ref, wo_ref, gf_ref, wr_ref, br_ref,
                         h2_ref, xn_ref, idx_ref, gate_ref, rank_ref, cnt_ref, run_ref):
    i = pl.program_id(0)
    tm, d = h_ref.shape

    @pl.when(i == 0)
    def _():
        run_ref[...] = jnp.zeros(run_ref.shape, F32)

    h = h_ref[...]
    hn = _rms(h, gc_ref[...]).astype(BF16)
    q = jnp.dot(hn, wq_ref[...], preferred_element_type=F32).astype(BF16)
    k = k_ref[...]
    v = v_ref[...]
    outs = []
    for hd in range(X_HEADS):
        sl = slice(X_HEAD_DIM * hd, X_HEAD_DIM * (hd + 1))
        s = lax.dot_general(q[:, sl], k[:, sl], (((1,), (1,)), ((), ())),
                            preferred_element_type=F32) * (X_HEAD_DIM ** -0.5)
        s = s - jnp.max(s, axis=-1, keepdims=True)
        p = jnp.exp(s)
        p = p / jnp.sum(p, axis=-1, keepdims=True)
        outs.append(jnp.dot(p.astype(BF16), v[:, sl], preferred_element_type=F32))
    o = jnp.concatenate(outs, axis=1).astype(BF16)
    h2 = h + jnp.dot(o, wo_ref[...], preferred_element_type=F32)
    h2_ref[...] = h2

    xn = _rms(h2, gf_ref[...])
    packed = _pack_pair(xn[:, :d // 2], xn[:, d // 2:])
    for j in range(V7X_SUBLANES):
        xn_ref[pl.ds(j, tm, stride=V7X_SUBLANES), :] = packed[:, j * V7X_LANES:(j + 1) * V7X_LANES]
    logits = jnp.dot(xn.astype(BF16), wr_ref[...], preferred_element_type=F32) + br_ref[...]

    lane = lax.broadcasted_iota(jnp.int32, logits.shape, 1).astype(F32)
    work = logits
    vals, idxs = [], []
    onehot = jnp.zeros(logits.shape, F32)
    for _ in range(TOP_K):
        mx = jnp.max(work, axis=-1, keepdims=True)
        ix = jnp.min(jnp.where(work == mx, lane, float(N_EXPERTS)), axis=-1, keepdims=True)
        sel = lane == ix
        vals.append(mx)
        idxs.append(ix)
        onehot = jnp.where(sel, 1.0, onehot)
        work = jnp.where(sel, NEG_INF, work)
    es = [jnp.exp(vv - vals[0]) for vv in vals]
    tot = es[0] + es[1] + es[2] + es[3]

    r = lax.broadcasted_iota(jnp.int32, (tm, tm), 0)
    c = lax.broadcasted_iota(jnp.int32, (tm, tm), 1)
    lower = jnp.where(c < r, 1.0, 0.0).astype(BF16)
    prefix = jnp.dot(lower, onehot.astype(BF16), preferred_element_type=F32) + run_ref[...]
    for kk in range(TOP_K):
        idx_ref[:, kk:kk + 1] = idxs[kk].astype(jnp.int32)
        gate_ref[:, kk:kk + 1] = es[kk] / tot
        rk = jnp.sum(jnp.where(lane == idxs[kk], prefix, 0.0), axis=-1, keepdims=True)
        rank_ref[:, kk:kk + 1] = rk.astype(jnp.int32)
    run_ref[...] = run_ref[...] + jnp.sum(onehot, axis=0, keepdims=True)
    cnt_ref[...] = run_ref[...].astype(jnp.int32)


def _cross_router(h1, kmem, vmem, gc, wq, wo, gf, wr, br, seq):
    t, d = h1.shape
    tm = min(CROSS_TM, seq)
    per_b = seq // tm
    m = kmem.shape[1]
    const = dict(pipeline_mode=pl.Buffered(1))
    outs = pl.pallas_call(
        _cross_router_kernel,
        grid=(t // tm,),
        in_specs=[
            pl.BlockSpec((tm, d), lambda i: (i, 0)),
            pl.BlockSpec((None, m, X_WIDTH), lambda i: (i // per_b, 0, 0)),
            pl.BlockSpec((None, m, X_WIDTH), lambda i: (i // per_b, 0, 0)),
            pl.BlockSpec((1, d), lambda i: (0, 0)),
            pl.BlockSpec((d, X_WIDTH), lambda i: (0, 0), **const),
            pl.BlockSpec((X_WIDTH, d), lambda i: (0, 0), **const),
            pl.BlockSpec((1, d), lambda i: (0, 0)),
            pl.BlockSpec((d, N_EXPERTS), lambda i: (0, 0), **const),
            pl.BlockSpec((1, N_EXPERTS), lambda i: (0, 0)),
        ],
        out_specs=[
            pl.BlockSpec((tm, d), lambda i: (i, 0)),
            pl.BlockSpec((tm * V7X_SUBLANES, V7X_LANES), lambda i: (i, 0)),
            pl.BlockSpec((tm, TOP_K), lambda i: (i, 0)),
            pl.BlockSpec((tm, TOP_K), lambda i: (i, 0)),
            pl.BlockSpec((tm, TOP_K), lambda i: (i, 0)),
            pl.BlockSpec((1, N_EXPERTS), lambda i: (0, 0)),
        ],
        out_shape=[
            jax.ShapeDtypeStruct((t, d), F32),
            jax.ShapeDtypeStruct((t * V7X_SUBLANES, V7X_LANES), U32),
            jax.ShapeDtypeStruct((t, TOP_K), jnp.int32),
            jax.ShapeDtypeStruct((t, TOP_K), F32),
            jax.ShapeDtypeStruct((t, TOP_K), jnp.int32),
            jax.ShapeDtypeStruct((1, N_EXPERTS), jnp.int32),
        ],
        scratch_shapes=[pltpu.VMEM((1, N_EXPERTS), F32)],
        compiler_params=_cparams(("arbitrary",)),
        name="cross_router",
    )(h1, kmem, vmem, gc, wq, wo, gf, wr, br)
    return outs


def _tile_rows(row):
    return pl.ds(pl.multiple_of(row * V7X_SUBLANES, V7X_SUBLANES), V7X_SUBLANES)


def _dispatch_kernel(pos_ref, tail_ref, x_ref, xs_ref, zero_ref, sem, zsem):
    i = pl.program_id(0)
    tm = x_ref.shape[0] // V7X_SUBLANES
    chunk_rows = zero_ref.shape[0]

    @pl.when(i == 0)
    def _():
        zero_ref[...] = jnp.zeros(zero_ref.shape, zero_ref.dtype)

        def tail_copy(e):
            row0 = pl.multiple_of(tail_ref[e] * V7X_SUBLANES, V7X_SUBLANES)
            return pltpu.make_async_copy(zero_ref, xs_ref.at[pl.ds(row0, chunk_rows)], zsem)

        for e in range(N_EXPERTS):
            @pl.when(tail_ref[e] >= 0)
            def _():
                tail_copy(e).start()
        for e in range(N_EXPERTS):
            @pl.when(tail_ref[e] >= 0)
            def _():
                tail_copy(e).wait()

    def row_copy(r, p):
        return pltpu.make_async_copy(x_ref.at[_tile_rows(r)], xs_ref.at[_tile_rows(p)], sem)

    def issue(r, carry):
        for kk in range(TOP_K):
            row_copy(r, pos_ref[0, 0, r * TOP_K + kk]).start(priority=kk % 2)
        return carry

    lax.fori_loop(0, tm, issue, 0, unroll=DMA_ISSUE_UNROLL)
    for _ in range(tm * TOP_K):
        row_copy(0, 0).wait()


def _dispatch(xn, pos, tails, n_rows):
    t = xn.shape[0] // V7X_SUBLANES
    tm = min(DISPATCH_TM, t)
    pos3 = pos.reshape(t // tm, 1, tm * TOP_K)
    return pl.pallas_call(
        _dispatch_kernel,
        grid=(t // tm,),
        in_specs=[
            pl.BlockSpec((1, 1, tm * TOP_K), lambda i: (i, 0, 0), memory_space=pltpu.SMEM),
            pl.BlockSpec(memory_space=pltpu.SMEM),
            pl.BlockSpec((tm * V7X_SUBLANES, V7X_LANES), lambda i: (i, 0)),
        ],
        out_specs=pl.BlockSpec(memory_space=pl.ANY),
        out_shape=jax.ShapeDtypeStruct((n_rows * V7X_SUBLANES, V7X_LANES), xn.dtype),
        scratch_shapes=[pltpu.VMEM((EXPERT_CHUNK * V7X_SUBLANES, V7X_LANES), xn.dtype),
                        pltpu.SemaphoreType.DMA(()), pltpu.SemaphoreType.DMA(())],
        compiler_params=_cparams(("arbitrary",)),
        name="dispatch",
    )(pos3, tails, xn)


def _expert_kernel(ve_ref, vr_ref, vb_ref, xs_ref, wg_ref, wl_ref, bg_ref, bl_ref, wd_ref, bd_ref, ys_ref,
                   x_ref, y_ref, act_ref, wa_ref, wb_ref, xsem, ysem):
    del ve_ref
    v = pl.program_id(0)
    s = pl.program_id(1)
    rows = vr_ref[v]
    nj = act_ref.shape[0]
    tf = act_ref.shape[2]
    ch = EXPERT_CHUNK
    n_chunks = x_ref.shape[0] // (ch * V7X_SUBLANES)
    half = tf // 2
    last_step = pl.num_programs(1) - 1
    sub = V7X_SUBLANES

    def x_copy(visit):
        row0 = pl.multiple_of(vb_ref[visit] * x_ref.shape[0], V7X_SUBLANES)
        return pltpu.make_async_copy(xs_ref.at[pl.ds(row0, x_ref.shape[0])], x_ref, xsem)

    def y_copy(visit):
        row0 = pl.multiple_of(vb_ref[visit] * y_ref.shape[0], V7X_SUBLANES)
        return pltpu.make_async_copy(y_ref, ys_ref.at[pl.ds(row0, y_ref.shape[0])], ysem)

    @pl.when((s == nj) & (v > 0))
    def _():
        y_copy(v - 1).wait()

    @pl.when((v == 0) & (s == 0))
    def _():
        x_copy(0).start()

    @pl.when(s == 0)
    def _():
        x_copy(v).wait()

    @pl.when((s == nj) & (v + 1 < pl.num_programs(0)))
    def _():
        x_copy(v + 1).start()

    @pl.when((rows > 0) & (s < nj))
    def _():
        wa_ref[...] = wg_ref[...].astype(BF16)
        wb_ref[...] = wl_ref[...].astype(BF16)
        for c in range(n_chunks):
            @pl.when(c * ch < rows)
            def _():
                xw = [x_ref[pl.ds(c * ch * V7X_SUBLANES + j, ch, stride=V7X_SUBLANES), :]
                      for j in range(V7X_SUBLANES)]
                xb = jnp.concatenate([_unpack_lo(u).astype(BF16) for u in xw]
                                     + [_unpack_hi(u).astype(BF16) for u in xw], axis=1)
                glu = jnp.dot(xb, wa_ref[...], preferred_element_type=F32) + bg_ref[...]
                lin = jnp.dot(xb, wb_ref[...], preferred_element_type=F32) + bl_ref[...]
                glu = jnp.minimum(glu, SWIGLU_LIMIT)
                lin = jnp.clip(lin, -SWIGLU_LIMIT, SWIGLU_LIMIT)
                act = glu * jax.nn.sigmoid(SWIGLU_ALPHA * glu) * (lin + 1.0)
                act_ref[s, pl.ds(c * ch, ch), :] = act.astype(BF16)

    @pl.when((rows > 0) & (s >= nj))
    def _():
        wa_ref[...] = wd_ref[...].astype(BF16)
        for c in range(n_chunks):
            sl = pl.ds(c * ch, ch)

            @pl.when(c * ch < rows)
            def _():
                acc = jnp.dot(act_ref[0, sl, :], wa_ref[0:tf, :], preferred_element_type=F32)
                for j in range(1, nj):
                    acc = acc + jnp.dot(act_ref[j, sl, :], wa_ref[j * tf:(j + 1) * tf, :],
                                        preferred_element_type=F32)
                acc = acc + bd_ref[...]
                packed = _pack_pair(acc[:, :half], acc[:, half:])
                for jj in range(half // V7X_LANES):
                    tile_sub = (s - nj) * (half // V7X_LANES) + jj
                    y_ref[pl.ds(c * ch * sub + tile_sub, ch, stride=sub), :] = (
                        packed[:, jj * V7X_LANES:(jj + 1) * V7X_LANES])

            @pl.when((c * ch >= rows) & (s == nj))
            def _():
                y_ref[pl.ds(c * ch * sub, ch * sub), :] = jnp.zeros((ch * sub, V7X_LANES), y_ref.dtype)

    @pl.when(s == last_step)
    def _():
        y_copy(v).start()

    @pl.when((s == last_step) & (v == pl.num_programs(0) - 1))
    def _():
        y_copy(v).wait()


def _experts(xs, visit_expert, visit_rows, visit_blk, n_used, w_gu, b_gu, w_down, b_down):
    n_rows = xs.shape[0] // V7X_SUBLANES
    e, d, ff2 = w_gu.shape
    ff = ff2 // 2
    r, tf = EXPERT_ROWS, EXPERT_TF
    nj = ff // tf
    nn = d // tf
    b_gu3 = b_gu.reshape(e, 1, ff2)
    b_down3 = b_down.reshape(e, 1, d)

    def up(v, s, vr):
        return jnp.where(vr[v] > 0, jnp.minimum(s, nj - 1), nj - 1)

    def down(v, s, vr):
        return jnp.where(vr[v] > 0, jnp.maximum(s - nj, 0), nn - 1)

    grid_spec = pltpu.PrefetchScalarGridSpec(
        num_scalar_prefetch=3,
        grid=(n_used, nj + nn),
        in_specs=[
            pl.BlockSpec(memory_space=pl.ANY),
            pl.BlockSpec((None, d, tf), lambda v, s, ve, vr, vb: (ve[v], 0, up(v, s, vr))),
            pl.BlockSpec((None, d, tf), lambda v, s, ve, vr, vb: (ve[v], 0, nj + up(v, s, vr))),
            pl.BlockSpec((None, 1, tf), lambda v, s, ve, vr, vb: (ve[v], 0, up(v, s, vr))),
            pl.BlockSpec((None, 1, tf), lambda v, s, ve, vr, vb: (ve[v], 0, nj + up(v, s, vr))),
            pl.BlockSpec((None, ff, tf), lambda v, s, ve, vr, vb: (ve[v], 0, down(v, s, vr))),
            pl.BlockSpec((None, 1, tf), lambda v, s, ve, vr, vb: (ve[v], 0, down(v, s, vr))),
        ],
        out_specs=pl.BlockSpec(memory_space=pl.ANY),
        scratch_shapes=[
            pltpu.VMEM((r * V7X_SUBLANES, V7X_LANES), xs.dtype),
            pltpu.VMEM((r * V7X_SUBLANES, V7X_LANES), U32),
            pltpu.VMEM((nj, r, tf), BF16),
            pltpu.VMEM((d, tf), BF16),
            pltpu.VMEM((d, tf), BF16),
            pltpu.SemaphoreType.DMA(()),
            pltpu.SemaphoreType.DMA(()),
        ],
    )
    return pl.pallas_call(
        _expert_kernel,
        grid_spec=grid_spec,
        out_shape=jax.ShapeDtypeStruct((n_rows * V7X_SUBLANES, V7X_LANES), U32),
        compiler_params=pltpu.CompilerParams(dimension_semantics=("arbitrary", "arbitrary"),
                                             vmem_limit_bytes=EXPERT_VMEM_LIMIT),
        name="experts",
    )(visit_expert, visit_rows, visit_blk, xs, w_gu, w_gu, b_gu3, b_gu3, w_down, b_down3)


def _combine_kernel(pos_ref, h_ref, gate_ref, g_ref, ys_ref, o_ref, buf_ref, sem):
    tm = h_ref.shape[0]
    subs_per_half = EXPERT_TF // 2 // V7X_LANES

    def row_copy(r, kk, p):
        return pltpu.make_async_copy(ys_ref.at[_tile_rows(p)], buf_ref.at[kk, _tile_rows(r)], sem)

    def issue(r, carry):
        for kk in range(TOP_K):
            row_copy(r, kk, pos_ref[0, 0, r * TOP_K + kk]).start(priority=kk % 2)
        return carry

    lax.fori_loop(0, tm, issue, 0, unroll=DMA_ISSUE_UNROLL)
    for _ in range(tm * TOP_K):
        row_copy(0, 0, 0).wait()

    gates = gate_ref[...]
    y = h_ref[...]
    for kk in range(TOP_K):
        words = [buf_ref[kk, pl.ds(j, tm, stride=V7X_SUBLANES), :] for j in range(V7X_SUBLANES)]
        pieces = []
        for n in range(V7X_SUBLANES // subs_per_half):
            chunk = words[n * subs_per_half:(n + 1) * subs_per_half]
            pieces += [_unpack_lo(u) for u in chunk] + [_unpack_hi(u) for u in chunk]
        y = y + gates[:, kk:kk + 1] * jnp.concatenate(pieces, axis=1)
    o_ref[...] = _rms(y, g_ref[...])


def _combine(h2, gates, pos, ys, g_final):
    t, d = h2.shape
    tm = min(COMBINE_TM, t)
    pos3 = pos.reshape(t // tm, 1, tm * TOP_K)
    return pl.pallas_call(
        _combine_kernel,
        grid=(t // tm,),
        in_specs=[
            pl.BlockSpec((1, 1, tm * TOP_K), lambda i: (i, 0, 0), memory_space=pltpu.SMEM),
            pl.BlockSpec((tm, d), lambda i: (i, 0)),
            pl.BlockSpec((tm, TOP_K), lambda i: (i, 0)),
            pl.BlockSpec((1, d), lambda i: (0, 0)),
            pl.BlockSpec(memory_space=pl.ANY),
        ],
        out_specs=pl.BlockSpec((tm, d), lambda i: (i, 0)),
        out_shape=jax.ShapeDtypeStruct((t, d), F32),
        scratch_shapes=[pltpu.VMEM((TOP_K, tm * V7X_SUBLANES, V7X_LANES), ys.dtype), pltpu.SemaphoreType.DMA(())],
        compiler_params=_cparams(("arbitrary",)),
        name="combine",
    )(pos3, h2, gates, g_final, ys)


def _routing_tables(counts, top_idx, rank, n_visits):
    r, ch = EXPERT_ROWS, EXPERT_CHUNK
    counts = counts.reshape(N_EXPERTS)
    padded = (counts + r - 1) // r * r
    ends = jnp.cumsum(padded)
    starts = ends - padded
    experts = jnp.arange(N_EXPERTS, dtype=jnp.int32)
    start_of = jnp.sum(jnp.where(top_idx[..., None] == experts, starts, 0), axis=-1)
    pos = start_of + rank
    visit_row0 = jnp.arange(n_visits, dtype=jnp.int32) * r
    owner = jnp.sum((ends[None, :] <= visit_row0[:, None]).astype(jnp.int32), axis=1)
    visit_expert = jnp.minimum(owner, N_EXPERTS - 1).astype(jnp.int32)
    visit_rows = jnp.clip(counts[visit_expert] - (visit_row0 - starts[visit_expert]), 0, r).astype(jnp.int32)
    n_used = (ends[-1] // r).astype(jnp.int32)
    last = jnp.maximum(n_used - 1, 0)
    ids = jnp.arange(n_visits, dtype=jnp.int32)
    active = ids < n_used
    visit_rows = jnp.where(active, visit_rows, 0)
    visit_blk = jnp.where(active, ids, last)
    visit_expert = jnp.where(active, visit_expert, visit_expert[last])
    tail = starts + jnp.minimum(counts // ch * ch, padded - ch)
    tail = jnp.where(counts > 0, tail, -1)
    return pos.astype(jnp.int32), visit_expert, visit_rows, visit_blk, tail.astype(jnp.int32), n_used


def kernel(x, mem, norm_mix_g, w_in, sinks, lambda_q1, lambda_k1, lambda_q2, lambda_k2, diff_subln_g, w_branch, w_o, norm_cross_g, norm_mem_g, w_cq, w_ckv, w_co, norm_ffn_g, w_router, b_router, w_gate_up, b_gate_up, w_down, b_down, norm_final_g):
    b, s, d = x.shape
    t = b * s
    x2 = x.reshape(t, d)
    l = 0

    w = w_in[l]
    o_qa, o_ka, o_va = 0, SWA_WIDTH, SWA_WIDTH + SWA_KV_WIDTH
    o_qd = o_va + SWA_KV_WIDTH
    o_kd, o_vd, o_g = o_qd + DIFF_WIDTH, o_qd + 2 * DIFF_WIDTH, o_qd + 3 * DIFF_WIDTH
    w_perm = jnp.concatenate([
        w[:, o_g:], w[:, o_qd:o_kd] * jnp.float32(HEAD_DIM ** -0.5 * LOG2E), w[:, o_kd:o_vd], w[:, o_vd:o_g],
        w[:, o_qa:o_ka], w[:, o_ka:o_va], w[:, o_va:o_qd]], axis=1).astype(BF16)
    proj = _inproj(x2, norm_mix_g[l].reshape(1, d), w_perm)
    c_qd = 2 * d
    c_kd, c_vd, c_qa = c_qd + DIFF_WIDTH, c_qd + 2 * DIFF_WIDTH, c_qd + 3 * DIFF_WIDTH
    c_ka, c_va = c_qa + SWA_WIDTH, c_qa + SWA_WIDTH + SWA_KV_WIDTH

    def seg(c0, width):
        return proj[:, c0:c0 + width].reshape(b, s, width)

    qaT = jnp.swapaxes(seg(c_qa, SWA_WIDTH), 1, 2)
    vaT = jnp.swapaxes(seg(c_va, SWA_KV_WIDTH), 1, 2)
    yaT = _swa(qaT, proj.reshape(b, s, proj.shape[1]), c_ka, vaT, sinks[l])
    ya = jnp.swapaxes(yaT, 1, 2).reshape(t, SWA_WIDTH)

    td = min(DIFF_T, s)
    nk = s // td
    hw = DIFF_HEAD_WIDTH
    cs, qx, kx = _diff_tables(td)
    qdT = jnp.swapaxes(seg(c_qd, DIFF_WIDTH), 1, 2)
    k4 = proj.reshape(b, nk, td, proj.shape[1])
    v5 = seg(c_vd, DIFF_WIDTH).reshape(b, nk, td, DIFF_HEADS, hw).transpose(0, 3, 1, 4, 2)
    ones = jnp.zeros((DIFF_DENOM_ROWS, td), BF16).at[0].set(1.0)
    lamv = jnp.stack([lambda_q1[l], lambda_k1[l], lambda_q2[l], lambda_k2[l]]).astype(F32)
    ydT = _diff(qdT, k4, c_kd, v5, qx, kx, ones, cs, lamv, diff_subln_g[l].astype(F32).reshape(hw, 1))
    yd = jnp.swapaxes(ydT, 1, 2).reshape(t, DIFF_WIDTH)

    h1 = _merge(ya, yd, proj, x2, w_branch[l].astype(BF16), w_o[l].astype(BF16))

    kmem, vmem = _memkv(mem, norm_mem_g[l].reshape(1, d), w_ckv[l].astype(BF16))
    h2, xn, top_idx, gates, rank, counts = _cross_router(
        h1, kmem, vmem, norm_cross_g[l].reshape(1, d), w_cq[l].astype(BF16), w_co[l].astype(BF16),
        norm_ffn_g[l].reshape(1, d), w_router[l].astype(BF16), b_router[l].reshape(1, N_EXPERTS), s)

    assert d // 2 == V7X_SUBLANES * V7X_LANES
    n_visits = (t * TOP_K) // EXPERT_ROWS + N_EXPERTS
    n_rows = n_visits * EXPERT_ROWS
    pos, visit_expert, visit_rows, visit_blk, tails, n_used = _routing_tables(counts, top_idx, rank, n_visits)
    xs = _dispatch(xn, pos, tails, n_rows)
    ys = _experts(xs, visit_expert, visit_rows, visit_blk, n_used, w_gate_up[l], b_gate_up[l], w_down[l], b_down[l])
    out = _combine(h2, gates, pos, ys, norm_final_g.reshape(1, d))
    return out.reshape(b, s, d)
```

```python
import functools
import math

import jax
import jax.numpy as jnp
from jax import lax
from jax.experimental import pallas as pl
from jax.experimental.pallas import tpu as pltpu

BF16 = jnp.bfloat16
F32 = jnp.float32
U32 = jnp.uint32
NEG_INF = float("-inf")
LOG2E = math.log2(math.e)

HEAD_DIM = 64
SWA_Q_HEADS = 16
SWA_KV_HEADS = 4
SWA_GROUP = SWA_Q_HEADS // SWA_KV_HEADS
SWA_BLOCK = 128
SWA_WIDTH = SWA_Q_HEADS * HEAD_DIM
SWA_KV_WIDTH = SWA_KV_HEADS * HEAD_DIM
DIFF_HEADS = 8
DIFF_HEAD_WIDTH = 2 * HEAD_DIM
DIFF_WIDTH = DIFF_HEADS * DIFF_HEAD_WIDTH
X_HEADS = 4
X_HEAD_DIM = 128
X_WIDTH = X_HEADS * X_HEAD_DIM
N_EXPERTS = 32
TOP_K = 4
SWIGLU_LIMIT = 7.0
SWIGLU_ALPHA = 1.702
RMS_EPS = 1e-5
LAMBDA_INIT = 0.8 - 0.6 * math.exp(-0.3 * 0)

V7X_VMEM_BYTES = 64 * 1024 * 1024
VMEM_LIMIT = V7X_VMEM_BYTES - 8 * 1024 * 1024
EXPERT_VMEM_LIMIT = V7X_VMEM_BYTES - 3 * 1024 * 1024
V7X_LANES = 128
V7X_SUBLANES = 8

INPROJ_TM = 1024
INPROJ_TN = 2176
DIFF_T = 512
DIFF_POS_SPLIT = 256
DIFF_DENOM_ROWS = 16
MERGE_TM = 256
CROSS_TM = 512
DISPATCH_TM = 1024
EXPERT_CHUNK = 768
EXPERT_ROWS = 3 * EXPERT_CHUNK
EXPERT_TF = 512
COMBINE_TM = 1024
DMA_ISSUE_UNROLL = 8


def _cparams(semantics):
    return pltpu.CompilerParams(dimension_semantics=semantics, vmem_limit_bytes=VMEM_LIMIT)


def _rms(x, g):
    ms = jnp.mean(x * x, axis=-1, keepdims=True)
    return x * lax.rsqrt(ms + RMS_EPS) * g


def _pack_pair(lo, hi):
    lo_bits = lax.bitcast_convert_type(lo.astype(BF16).astype(F32), U32)
    hi_bits = lax.bitcast_convert_type(hi.astype(BF16).astype(F32), U32)
    return hi_bits | (lo_bits >> 16)


def _unpack_lo(w):
    return lax.bitcast_convert_type(w << 16, F32)


def _unpack_hi(w):
    return lax.bitcast_convert_type(w & jnp.uint32(0xFFFF0000), F32)


def _inproj_kernel(x_ref, g_ref, w_ref, o_ref, xn_ref):
    @pl.when(pl.program_id(1) == 0)
    def _():
        xn_ref[...] = _rms(x_ref[...], g_ref[...]).astype(BF16)

    o_ref[...] = jnp.dot(xn_ref[...], w_ref[...], preferred_element_type=F32).astype(o_ref.dtype)


def _inproj(x2, g, w):
    t, d = x2.shape
    n = w.shape[1]
    tm, tn = min(INPROJ_TM, t), INPROJ_TN
    return pl.pallas_call(
        _inproj_kernel,
        grid=(t // tm, n // tn),
        in_specs=[
            pl.BlockSpec((tm, d), lambda i, j: (i, 0)),
            pl.BlockSpec((1, d), lambda i, j: (0, 0)),
            pl.BlockSpec((d, tn), lambda i, j: (0, j)),
        ],
        out_specs=pl.BlockSpec((tm, tn), lambda i, j: (i, j)),
        out_shape=jax.ShapeDtypeStruct((t, n), BF16),
        scratch_shapes=[pltpu.VMEM((tm, d), BF16)],
        compiler_params=_cparams(("parallel", "arbitrary")),
        name="inproj",
    )(x2, g, w)


def _swa_kernel(qT_ref, kp_ref, kc_ref, vp_ref, vc_ref, bias_ref, sink_ref, o_ref):
    n = pl.program_id(1)
    blk = SWA_BLOCK
    kband = jnp.concatenate([kp_ref[...], kc_ref[...]], axis=0)
    vband = jnp.concatenate([vp_ref[...], vc_ref[...]], axis=1)
    krow = lax.broadcasted_iota(jnp.int32, (2 * blk, SWA_GROUP * blk), 0)
    has_prev = n > 0
    zeros = jnp.zeros((HEAD_DIM, blk), BF16)
    for h in range(SWA_KV_HEADS):
        lane0 = 2 * HEAD_DIM * (h // 2)
        k128 = kband[:, lane0:lane0 + 2 * HEAD_DIM]
        pieces = []
        for g in range(SWA_GROUP):
            hq = SWA_GROUP * h + g
            qh = qT_ref[HEAD_DIM * hq:HEAD_DIM * (hq + 1), :]
            pieces.append(jnp.concatenate([qh, zeros] if h % 2 == 0 else [zeros, qh], axis=0))
        qz = jnp.concatenate(pieces, axis=1)
        s = jnp.dot(k128, qz, preferred_element_type=F32) * (HEAD_DIM ** -0.5)
        s = s + bias_ref[h]
        s = jnp.where((krow >= blk) | has_prev, s, NEG_INF)
        sink = sink_ref[h]
        m = jnp.maximum(jnp.max(s, axis=0, keepdims=True), sink)
        p = jnp.exp(s - m)
        denom = jnp.sum(p, axis=0, keepdims=True) + jnp.exp(sink - m)
        vh = vband[HEAD_DIM * h:HEAD_DIM * (h + 1), :]
        o = jnp.dot(vh, p.astype(BF16), preferred_element_type=F32) / denom
        for g in range(SWA_GROUP):
            hq = SWA_GROUP * h + g
            o_ref[HEAD_DIM * hq:HEAD_DIM * (hq + 1), :] = o[:, blk * g:blk * (g + 1)].astype(o_ref.dtype)


def _swa_tables(sinks):
    blk = SWA_BLOCK
    slopes = (2.0 ** (-8.0 * jnp.arange(1, SWA_Q_HEADS + 1, dtype=F32) / SWA_Q_HEADS)).reshape(SWA_KV_HEADS, SWA_GROUP)
    q_pos = jnp.arange(blk)[None, :] + blk
    k_pos = jnp.arange(2 * blk)[:, None]
    dist = q_pos - k_pos
    allowed = (dist >= 0) & (dist < SWA_BLOCK)
    bias = -slopes[:, :, None, None] * dist.astype(F32)[None, None]
    bias = jnp.where(allowed[None, None], bias, NEG_INF)
    bias = jnp.transpose(bias, (0, 2, 1, 3)).reshape(SWA_KV_HEADS, 2 * blk, SWA_GROUP * blk)
    sink = sinks.astype(F32).reshape(SWA_KV_HEADS, SWA_GROUP, 1)
    sink = jnp.broadcast_to(sink, (SWA_KV_HEADS, SWA_GROUP, blk)).reshape(SWA_KV_HEADS, 1, SWA_GROUP * blk)
    return bias, sink


def _swa(qT, k, k_col0, vT, sinks):
    b, _, s = qT.shape
    blk = SWA_BLOCK
    bias, sink = _swa_tables(sinks)
    prev = lambda n: jnp.maximum(n - 1, 0)
    kb = k_col0 // SWA_KV_WIDTH
    return pl.pallas_call(
        _swa_kernel,
        grid=(b, s // blk),
        in_specs=[
            pl.BlockSpec((None, SWA_WIDTH, blk), lambda bi, n: (bi, 0, n)),
            pl.BlockSpec((None, blk, SWA_KV_WIDTH), lambda bi, n: (bi, prev(n), kb)),
            pl.BlockSpec((None, blk, SWA_KV_WIDTH), lambda bi, n: (bi, n, kb)),
            pl.BlockSpec((None, SWA_KV_WIDTH, blk), lambda bi, n: (bi, 0, prev(n))),
            pl.BlockSpec((None, SWA_KV_WIDTH, blk), lambda bi, n: (bi, 0, n)),
            pl.BlockSpec(bias.shape, lambda bi, n: (0, 0, 0)),
            pl.BlockSpec(sink.shape, lambda bi, n: (0, 0, 0)),
        ],
        out_specs=pl.BlockSpec((None, SWA_WIDTH, blk), lambda bi, n: (bi, 0, n)),
        out_shape=jax.ShapeDtypeStruct((b, SWA_WIDTH, s), BF16),
        compiler_params=_cparams(("parallel", "arbitrary")),
        name="swa",
    )(qT, k, k, vT, vT, bias, sink)


def _diff_kernel(c_ref, qT_ref, qx_ref, k_ref, kx_ref, v_ref, ones_ref, lamv_ref, g_ref, o_ref,
                 q2_ref, z_ref, zm_ref, p_ref, a_ref, m_ref, acc_ref, *, t):
    h = pl.program_id(1)
    qi = pl.program_id(2)
    w = DIFF_HEAD_WIDTH
    c = c_ref[h]

    q2_ref[w:, :] = qx_ref[...]
    qt = qT_ref[...]
    row = lax.broadcasted_iota(jnp.int32, qt.shape, 0)
    zero = jnp.zeros_like(qt)
    q2_ref[:w, :t] = jnp.where(row < HEAD_DIM, qt, zero)
    q2_ref[:w, t:] = jnp.where(row >= HEAD_DIM, qt, zero)
    m_ref[...] = jnp.full(m_ref.shape, NEG_INF, F32)
    acc_ref[...] = jnp.zeros(acc_ref.shape, F32)
    p_ref[1] = jnp.zeros(p_ref.shape[1:], BF16)
    a_ref[1] = jnp.ones(a_ref.shape[1:], F32)

    def scores(k, slot):
        keys = jnp.concatenate([k_ref[k], kx_ref[...]], axis=1)
        z = jnp.dot(keys, q2_ref[...], preferred_element_type=F32)
        z_ref[slot] = z
        zm_ref[slot] = jnp.max(z, axis=0, keepdims=True)

    def values(k, slot):
        vals = jnp.concatenate([v_ref[jnp.maximum(k, 0)], ones_ref[...]], axis=0)
        pv = jnp.dot(vals, p_ref[slot], preferred_element_type=F32)
        acc_ref[...] = a_ref[slot] * acc_ref[...] + pv

    def softmax(k, slot, masked):
        z = z_ref[slot]
        if masked:
            kr = lax.broadcasted_iota(jnp.int32, z.shape, 0)
            qc = lax.broadcasted_iota(jnp.int32, z.shape, 1)
            qc = jnp.where(qc >= t, qc - t, qc)
            z = jnp.where(kr <= qc, z, NEG_INF)
            zmax = jnp.max(z, axis=0, keepdims=True)
        else:
            zmax = zm_ref[slot]
        off = c * ((k - qi) * t).astype(F32)
        m_old = m_ref[...]
        m_new = jnp.maximum(m_old, zmax + off)
        p_ref[slot] = jnp.exp2(z - (m_new - off)).astype(BF16)
        a_ref[slot] = jnp.exp2(m_old - m_new)
        m_ref[...] = m_new

    scores(0, 0)

    def pair(j, carry):
        k = 2 * j
        scores(k + 1, 1)
        values(k - 1, 1)
        softmax(k, 0, False)
        scores(k + 2, 0)
        values(k, 0)
        softmax(k + 1, 1, False)
        return carry

    lax.fori_loop(0, qi >> 1, pair, 0)

    @pl.when((qi & 1) == 0)
    def _():
        values(qi - 1, 1)
        softmax(qi, 0, True)
        values(qi, 0)

    @pl.when((qi & 1) == 1)
    def _():
        scores(qi, 1)
        values(qi - 2, 1)
        softmax(qi - 1, 0, False)
        values(qi - 1, 0)
        softmax(qi, 1, True)
        values(qi, 1)

    lamv = lamv_ref[...]
    lam = (jnp.exp(jnp.sum(lamv[0:1] * lamv[1:2], axis=-1, keepdims=True))
           - jnp.exp(jnp.sum(lamv[2:3] * lamv[3:4], axis=-1, keepdims=True)) + LAMBDA_INIT)
    acc = acc_ref[...]
    l = acc[w:w + 1, :]
    o = acc[:w, :t] / l[:, :t] - lam * (acc[:w, t:] / l[:, t:])
    ms = jnp.mean(o * o, axis=0, keepdims=True)
    y = o * lax.rsqrt(ms + RMS_EPS) * g_ref[...] * (1.0 - LAMBDA_INIT)
    o_ref[...] = y.astype(o_ref.dtype)


def _diff_tables(t):
    w = DIFF_HEAD_WIDTH
    slopes = 2.0 ** (-8.0 * jnp.arange(1, DIFF_HEADS + 1, dtype=F32) / DIFF_HEADS)
    c = slopes * jnp.float32(LOG2E)
    c1 = c.astype(BF16)
    c2 = (c - c1.astype(F32)).astype(BF16)
    c3 = (c - c1.astype(F32) - c2.astype(F32)).astype(BF16)
    parts = jnp.stack([c1, c2, c3, c1, c2, c3], axis=1)
    qx = jnp.zeros((DIFF_HEADS, w, 2 * t), BF16).at[:, :6, :].set(parts[:, :, None])
    r = jnp.arange(t)
    r_lo = (r % DIFF_POS_SPLIT).astype(BF16)
    r_hi = ((r // DIFF_POS_SPLIT) * DIFF_POS_SPLIT).astype(BF16)
    kx = jnp.zeros((t, w), BF16).at[:, 0:3].set(r_lo[:, None]).at[:, 3:6].set(r_hi[:, None])
    return c, qx, kx


def _diff(qT, k4, k_col0, v5, qx, kx, ones, cs, lamv, subln_g):
    b, _, s = qT.shape
    t = k4.shape[2]
    nk = s // t
    w = DIFF_HEAD_WIDTH
    wv = w + ones.shape[0]
    k_blk0 = k_col0 // w
    return pl.pallas_call(
        functools.partial(_diff_kernel, t=t),
        grid=(b, DIFF_HEADS, nk),
        in_specs=[
            pl.BlockSpec(memory_space=pltpu.SMEM),
            pl.BlockSpec((None, w, t), lambda bi, h, qi: (bi, h, qi)),
            pl.BlockSpec((None, w, 2 * t), lambda bi, h, qi: (h, 0, 0)),
            pl.BlockSpec((None, nk, t, w), lambda bi, h, qi: (bi, 0, 0, k_blk0 + h)),
            pl.BlockSpec((t, w), lambda bi, h, qi: (0, 0)),
            pl.BlockSpec((None, None, nk, w, t), lambda bi, h, qi: (bi, h, 0, 0, 0)),
            pl.BlockSpec(ones.shape, lambda bi, h, qi: (0, 0)),
            pl.BlockSpec((4, HEAD_DIM), lambda bi, h, qi: (0, 0)),
            pl.BlockSpec((w, 1), lambda bi, h, qi: (0, 0)),
        ],
        out_specs=pl.BlockSpec((None, w, t), lambda bi, h, qi: (bi, h, qi)),
        out_shape=jax.ShapeDtypeStruct((b, DIFF_WIDTH, s), BF16),
        scratch_shapes=[
            pltpu.VMEM((2 * w, 2 * t), BF16),
            pltpu.VMEM((2, t, 2 * t), F32),
            pltpu.VMEM((2, 1, 2 * t), F32),
            pltpu.VMEM((2, t, 2 * t), BF16),
            pltpu.VMEM((2, 1, 2 * t), F32),
            pltpu.VMEM((1, 2 * t), F32),
            pltpu.VMEM((wv, 2 * t), F32),
        ],
        compiler_params=_cparams(("arbitrary", "arbitrary", "arbitrary")),
        name="diffattn",
    )(cs, qT, qx, k4, kx, v5, ones, lamv, subln_g)


def _merge_kernel(ya_ref, yd_ref, g0_ref, g1_ref, x_ref, wb0_ref, wb1_ref, wo_ref, o_ref):
    ua = jnp.dot(ya_ref[...], wb0_ref[...], preferred_element_type=F32)
    ud = jnp.dot(yd_ref[...], wb1_ref[...], preferred_element_type=F32)
    merged = jax.nn.sigmoid(g0_ref[...].astype(F32)) * ua + jax.nn.sigmoid(g1_ref[...].astype(F32)) * ud
    o_ref[...] = x_ref[...] + jnp.dot(merged.astype(BF16), wo_ref[...], preferred_element_type=F32)


def _merge(ya, yd, proj, x2, wb, wo):
    t, d = x2.shape
    tm = min(MERGE_TM, t)
    bw = ya.shape[1]
    const = dict(pipeline_mode=pl.Buffered(1))
    return pl.pallas_call(
        _merge_kernel,
        grid=(t // tm,),
        in_specs=[
            pl.BlockSpec((tm, bw), lambda i: (i, 0)),
            pl.BlockSpec((tm, bw), lambda i: (i, 0)),
            pl.BlockSpec((tm, d), lambda i: (i, 0)),
            pl.BlockSpec((tm, d), lambda i: (i, 1)),
            pl.BlockSpec((tm, d), lambda i: (i, 0)),
            pl.BlockSpec((None, bw, d), lambda i: (0, 0, 0), **const),
            pl.BlockSpec((None, bw, d), lambda i: (1, 0, 0), **const),
            pl.BlockSpec((d, d), lambda i: (0, 0), **const),
        ],
        out_specs=pl.BlockSpec((tm, d), lambda i: (i, 0)),
        out_shape=jax.ShapeDtypeStruct((t, d), F32),
        compiler_params=_cparams(("parallel",)),
        name="merge",
    )(ya, yd, proj, proj, x2, wb, wb, wo)


def _memkv_kernel(mem_ref, g_ref, w_ref, k_ref, v_ref):
    mn = _rms(mem_ref[...], g_ref[...]).astype(BF16)
    kv = jnp.dot(mn, w_ref[...], preferred_element_type=F32)
    k_ref[...] = kv[:, :X_WIDTH].astype(k_ref.dtype)
    v_ref[...] = kv[:, X_WIDTH:].astype(v_ref.dtype)


def _memkv(mem, g, w):
    b, m, d = mem.shape
    out = jax.ShapeDtypeStruct((b, m, X_WIDTH), BF16)
    return pl.pallas_call(
        _memkv_kernel,
        grid=(b,),
        in_specs=[
            pl.BlockSpec((None, m, d), lambda i: (i, 0, 0)),
            pl.BlockSpec((1, d), lambda i: (0, 0)),
            pl.BlockSpec((d, 2 * X_WIDTH), lambda i: (0, 0)),
        ],
        out_specs=[pl.BlockSpec((None, m, X_WIDTH), lambda i: (i, 0, 0))] * 2,
        out_shape=[out, out],
        compiler_params=_cparams(("parallel",)),
        name="memkv",
    )(mem, g, w)


def _cross_router_kernel(h_ref, k_ref, v_ref, gc_ref, wq_ref, wo_ref, gf_ref, wr_ref, br_ref,
                         h2_ref, xn_ref, idx_ref, gate_ref, rank_ref, cnt_ref, run_ref):
    i = pl.program_id(0)
    tm, d = h_ref.shape

    @pl.when(i == 0)
    def _():
        run_ref[...] = jnp.zeros(run_ref.shape, F32)

    h = h_ref[...]
    hn = _rms(h, gc_ref[...]).astype(BF16)
    q = jnp.dot(hn, wq_ref[...], preferred_element_type=F32).astype(BF16)
    k = k_ref[...]
    v = v_ref[...]
    outs = []
    for hd in range(X_HEADS):
        sl = slice(X_HEAD_DIM * hd, X_HEAD_DIM * (hd + 1))
        s = lax.dot_general(q[:, sl], k[:, sl], (((1,), (1,)), ((), ())),
                            preferred_element_type=F32) * (X_HEAD_DIM ** -0.5)
        s = s - jnp.max(s, axis=-1, keepdims=True)
        p = jnp.exp(s)
        p = p / jnp.sum(p, axis=-1, keepdims=True)
        outs.append(jnp.dot(p.astype(BF16), v[:, sl], preferred_element_type=F32))
    o = jnp.concatenate(outs, axis=1).astype(BF16)
    h2 = h + jnp.dot(o, wo_ref[...], preferred_element_type=F32)
    h2_ref[...] = h2

    xn = _rms(h2, gf_ref[...])
    packed = _pack_pair(xn[:, :d // 2], xn[:, d // 2:])
    for j in range(V7X_SUBLANES):
        xn_ref[pl.ds(j, tm, stride=V7X_SUBLANES), :] = packed[:, j * V7X_LANES:(j + 1) * V7X_LANES]
    logits = jnp.dot(xn.astype(BF16), wr_ref[...], preferred_element_type=F32) + br_ref[...]

    lane = lax.broadcasted_iota(jnp.int32, logits.shape, 1).astype(F32)
    work = logits
    vals, idxs = [], []
    onehot = jnp.zeros(logits.shape, F32)
    for _ in range(TOP_K):
        mx = jnp.max(work, axis=-1, keepdims=True)
        ix = jnp.min(jnp.where(work == mx, lane, float(N_EXPERTS)), axis=-1, keepdims=True)
        sel = lane == ix
        vals.append(mx)
        idxs.append(ix)
        onehot = jnp.where(sel, 1.0, onehot)
        work = jnp.where(sel, NEG_INF, work)
    es = [jnp.exp(vv - vals[0]) for vv in vals]
    tot = es[0] + es[1] + es[2] + es[3]

    r = lax.broadcasted_iota(jnp.int32, (tm, tm), 0)
    c = lax.broadcasted_iota(jnp.int32, (tm, tm), 1)
    lower = jnp.where(c < r, 1.0, 0.0).astype(BF16)
    prefix = jnp.dot(lower, onehot.astype(BF16), preferred_element_type=F32) + run_ref[...]
    for kk in range(TOP_K):
        idx_ref[:, kk:kk + 1] = idxs[kk].astype(jnp.int32)
        gate_ref[:, kk:kk + 1] = es[kk] / tot
        rk = jnp.sum(jnp.where(lane == idxs[kk], prefix, 0.0), axis=-1, keepdims=True)
        rank_ref[:, kk:kk + 1] = rk.astype(jnp.int32)
    run_ref[...] = run_ref[...] + jnp.sum(onehot, axis=0, keepdims=True)
    cnt_ref[...] = run_ref[...].astype(jnp.int32)


def _cross_router(h1, kmem, vmem, gc, wq, wo, gf, wr, br, seq):
    t, d = h1.shape
    tm = min(CROSS_TM, seq)
    per_b = seq // tm
    m = kmem.shape[1]
    const = dict(pipeline_mode=pl.Buffered(1))
    outs = pl.pallas_call(
        _cross_router_kernel,
        grid=(t // tm,),
        in_specs=[
            pl.BlockSpec((tm, d), lambda i: (i, 0)),
            pl.BlockSpec((None, m, X_WIDTH), lambda i: (i // per_b, 0, 0)),
            pl.BlockSpec((None, m, X_WIDTH), lambda i: (i // per_b, 0, 0)),
            pl.BlockSpec((1, d), lambda i: (0, 0)),
            pl.BlockSpec((d, X_WIDTH), lambda i: (0, 0), **const),
            pl.BlockSpec((X_WIDTH, d), lambda i: (0, 0), **const),
            pl.BlockSpec((1, d), lambda i: (0, 0)),
            pl.BlockSpec((d, N_EXPERTS), lambda i: (0, 0), **const),
            pl.BlockSpec((1, N_EXPERTS), lambda i: (0, 0)),
        ],
        out_specs=[
            pl.BlockSpec((tm, d), lambda i: (i, 0)),
            pl.BlockSpec((tm * V7X_SUBLANES, V7X_LANES), lambda i: (i, 0)),
            pl.BlockSpec((tm, TOP_K), lambda i: (i, 0)),
            pl.BlockSpec((tm, TOP_K), lambda i: (i, 0)),
            pl.BlockSpec((tm, TOP_K), lambda i: (i, 0)),
            pl.BlockSpec((1, N_EXPERTS), lambda i: (0, 0)),
        ],
        out_shape=[
            jax.ShapeDtypeStruct((t, d), F32),
            jax.ShapeDtypeStruct((t * V7X_SUBLANES, V7X_LANES), U32),
            jax.ShapeDtypeStruct((t, TOP_K), jnp.int32),
            jax.ShapeDtypeStruct((t, TOP_K), F32),
            jax.ShapeDtypeStruct((t, TOP_K), jnp.int32),
            jax.ShapeDtypeStruct((1, N_EXPERTS), jnp.int32),
        ],
        scratch_shapes=[pltpu.VMEM((1, N_EXPERTS), F32)],
        compiler_params=_cparams(("arbitrary",)),
        name="cross_router",
    )(h1, kmem, vmem, gc, wq, wo, gf, wr, br)
    return outs


def _tile_rows(row):
    return pl.ds(pl.multiple_of(row * V7X_SUBLANES, V7X_SUBLANES), V7X_SUBLANES)


def _dispatch_kernel(pos_ref, tail_ref, x_ref, xs_ref, zero_ref, sem, zsem):
    i = pl.program_id(0)
    tm = x_ref.shape[0] // V7X_SUBLANES
    chunk_rows = zero_ref.shape[0]

    @pl.when(i == 0)
    def _():
        zero_ref[...] = jnp.zeros(zero_ref.shape, zero_ref.dtype)

        def tail_copy(e):
            row0 = pl.multiple_of(tail_ref[e] * V7X_SUBLANES, V7X_SUBLANES)
            return pltpu.make_async_copy(zero_ref, xs_ref.at[pl.ds(row0, chunk_rows)], zsem)

        for e in range(N_EXPERTS):
            @pl.when(tail_ref[e] >= 0)
            def _():
                tail_copy(e).start()
        for e in range(N_EXPERTS):
            @pl.when(tail_ref[e] >= 0)
            def _():
                tail_copy(e).wait()

    def row_copy(r, p):
        return pltpu.make_async_copy(x_ref.at[_tile_rows(r)], xs_ref.at[_tile_rows(p)], sem)

    def issue(r, carry):
        for kk in range(TOP_K):
            row_copy(r, pos_ref[0, 0, r * TOP_K + kk]).start(priority=kk % 2)
        return carry

    lax.fori_loop(0, tm, issue, 0, unroll=DMA_ISSUE_UNROLL)
    for _ in range(tm * TOP_K):
        row_copy(0, 0).wait()


def _dispatch(xn, pos, tails, n_rows):
    t = xn.shape[0] // V7X_SUBLANES
    tm = min(DISPATCH_TM, t)
    pos3 = pos.reshape(t // tm, 1, tm * TOP_K)
    return pl.pallas_call(
        _dispatch_kernel,
        grid=(t // tm,),
        in_specs=[
            pl.BlockSpec((1, 1, tm * TOP_K), lambda i: (i, 0, 0), memory_space=pltpu.SMEM),
            pl.BlockSpec(memory_space=pltpu.SMEM),
            pl.BlockSpec((tm * V7X_SUBLANES, V7X_LANES), lambda i: (i, 0)),
        ],
        out_specs=pl.BlockSpec(memory_space=pl.ANY),
        out_shape=jax.ShapeDtypeStruct((n_rows * V7X_SUBLANES, V7X_LANES), xn.dtype),
        scratch_shapes=[pltpu.VMEM((EXPERT_CHUNK * V7X_SUBLANES, V7X_LANES), xn.dtype),
                        pltpu.SemaphoreType.DMA(()), pltpu.SemaphoreType.DMA(())],
        compiler_params=_cparams(("arbitrary",)),
        name="dispatch",
    )(pos3, tails, xn)


def _expert_kernel(ve_ref, vr_ref, vb_ref, xs_ref, wg_ref, wl_ref, bg_ref, bl_ref, wd_ref, bd_ref, ys_ref,
                   x_ref, y_ref, act_ref, wa_ref, wb_ref, xsem, ysem):
    del ve_ref
    v = pl.program_id(0)
    s = pl.program_id(1)
    rows = vr_ref[v]
    nj = act_ref.shape[0]
    tf = act_ref.shape[2]
    ch = EXPERT_CHUNK
    n_chunks = x_ref.shape[0] // (ch * V7X_SUBLANES)
    half = tf // 2
    last_step = pl.num_programs(1) - 1
    sub = V7X_SUBLANES

    def x_copy(visit):
        row0 = pl.multiple_of(vb_ref[visit] * x_ref.shape[0], V7X_SUBLANES)
        return pltpu.make_async_copy(xs_ref.at[pl.ds(row0, x_ref.shape[0])], x_ref, xsem)

    def y_copy(visit):
        row0 = pl.multiple_of(vb_ref[visit] * y_ref.shape[0], V7X_SUBLANES)
        return pltpu.make_async_copy(y_ref, ys_ref.at[pl.ds(row0, y_ref.shape[0])], ysem)

    @pl.when((s == nj) & (v > 0))
    def _():
        y_copy(v - 1).wait()

    @pl.when((v == 0) & (s == 0))
    def _():
        x_copy(0).start()

    @pl.when(s == 0)
    def _():
        x_copy(v).wait()

    @pl.when((s == nj) & (v + 1 < pl.num_programs(0)))
    def _():
        x_copy(v + 1).start()

    @pl.when((rows > 0) & (s < nj))
    def _():
        wa_ref[...] = wg_ref[...].astype(BF16)
        wb_ref[...] = wl_ref[...].astype(BF16)
        for c in range(n_chunks):
            @pl.when(c * ch < rows)
            def _():
                xw = [x_ref[pl.ds(c * ch * V7X_SUBLANES + j, ch, stride=V7X_SUBLANES), :]
                      for j in range(V7X_SUBLANES)]
                xb = jnp.concatenate([_unpack_lo(u).astype(BF16) for u in xw]
                                     + [_unpack_hi(u).astype(BF16) for u in xw], axis=1)
                glu = jnp.dot(xb, wa_ref[...], preferred_element_type=F32) + bg_ref[...]
                lin = jnp.dot(xb, wb_ref[...], preferred_element_type=F32) + bl_ref[...]
                glu = jnp.minimum(glu, SWIGLU_LIMIT)
                lin = jnp.clip(lin, -SWIGLU_LIMIT, SWIGLU_LIMIT)
                act = glu * jax.nn.sigmoid(SWIGLU_ALPHA * glu) * (lin + 1.0)
                act_ref[s, pl.ds(c * ch, ch), :] = act.astype(BF16)

    @pl.when((rows > 0) & (s >= nj))
    def _():
        wa_ref[...] = wd_ref[...].astype(BF16)
        for c in range(n_chunks):
            sl = pl.ds(c * ch, ch)

            @pl.when(c * ch < rows)
            def _():
                acc = jnp.dot(act_ref[0, sl, :], wa_ref[0:tf, :], preferred_element_type=F32)
                for j in range(1, nj):
                    acc = acc + jnp.dot(act_ref[j, sl, :], wa_ref[j * tf:(j + 1) * tf, :],
                                        preferred_element_type=F32)
                acc = acc + bd_ref[...]
                packed = _pack_pair(acc[:, :half], acc[:, half:])
                for jj in range(half // V7X_LANES):
                    tile_sub = (s - nj) * (half // V7X_LANES) + jj
                    y_ref[pl.ds(c * ch * sub + tile_sub, ch, stride=sub), :] = (
                        packed[:, jj * V7X_LANES:(jj + 1) * V7X_LANES])

            @pl.when((c * ch >= rows) & (s == nj))
            def _():
                y_ref[pl.ds(c * ch * sub, ch * sub), :] = jnp.zeros((ch * sub, V7X_LANES), y_ref.dtype)

    @pl.when(s == last_step)
    def _():
        y_copy(v).start()

    @pl.when((s == last_step) & (v == pl.num_programs(0) - 1))
    def _():
        y_copy(v).wait()


def _experts(xs, visit_expert, visit_rows, visit_blk, n_used, w_gu, b_gu, w_down, b_down):
    n_rows = xs.shape[0] // V7X_SUBLANES
    e, d, ff2 = w_gu.shape
    ff = ff2 // 2
    r, tf = EXPERT_ROWS, EXPERT_TF
    nj = ff // tf
    nn = d // tf
    b_gu3 = b_gu.reshape(e, 1, ff2)
    b_down3 = b_down.reshape(e, 1, d)

    def up(v, s, vr):
        return jnp.where(vr[v] > 0, jnp.minimum(s, nj - 1), nj - 1)

    def down(v, s, vr):
        return jnp.where(vr[v] > 0, jnp.maximum(s - nj, 0), nn - 1)

    grid_spec = pltpu.PrefetchScalarGridSpec(
        num_scalar_prefetch=3,
        grid=(n_used, nj + nn),
        in_specs=[
            pl.BlockSpec(memory_space=pl.ANY),
            pl.BlockSpec((None, d, tf), lambda v, s, ve, vr, vb: (ve[v], 0, up(v, s, vr))),
            pl.BlockSpec((None, d, tf), lambda v, s, ve, vr, vb: (ve[v], 0, nj + up(v, s, vr))),
            pl.BlockSpec((None, 1, tf), lambda v, s, ve, vr, vb: (ve[v], 0, up(v, s, vr))),
            pl.BlockSpec((None, 1, tf), lambda v, s, ve, vr, vb: (ve[v], 0, nj + up(v, s, vr))),
            pl.BlockSpec((None, ff, tf), lambda v, s, ve, vr, vb: (ve[v], 0, down(v, s, vr))),
            pl.BlockSpec((None, 1, tf), lambda v, s, ve, vr, vb: (ve[v], 0, down(v, s, vr))),
        ],
        out_specs=pl.BlockSpec(memory_space=pl.ANY),
        scratch_shapes=[
            pltpu.VMEM((r * V7X_SUBLANES, V7X_LANES), xs.dtype),
            pltpu.VMEM((r * V7X_SUBLANES, V7X_LANES), U32),
            pltpu.VMEM((nj, r, tf), BF16),
            pltpu.VMEM((d, tf), BF16),
            pltpu.VMEM((d, tf), BF16),
            pltpu.SemaphoreType.DMA(()),
            pltpu.SemaphoreType.DMA(()),
        ],
    )
    return pl.pallas_call(
        _expert_kernel,
        grid_spec=grid_spec,
        out_shape=jax.ShapeDtypeStruct((n_rows * V7X_SUBLANES, V7X_LANES), U32),
        compiler_params=pltpu.CompilerParams(dimension_semantics=("arbitrary", "arbitrary"),
                                             vmem_limit_bytes=EXPERT_VMEM_LIMIT),
        name="experts",
    )(visit_expert, visit_rows, visit_blk, xs, w_gu, w_gu, b_gu3, b_gu3, w_down, b_down3)


def _combine_kernel(pos_ref, h_ref, gate_ref, g_ref, ys_ref, o_ref, buf_ref, sem):
    tm = h_ref.shape[0]
    subs_per_half = EXPERT_TF // 2 // V7X_LANES

    def row_copy(r, kk, p):
        return pltpu.make_async_copy(ys_ref.at[_tile_rows(p)], buf_ref.at[kk, _tile_rows(r)], sem)

    def issue(r, carry):
        for kk in range(TOP_K):
            row_copy(r, kk, pos_ref[0, 0, r * TOP_K + kk]).start(priority=kk % 2)
        return carry

    lax.fori_loop(0, tm, issue, 0, unroll=DMA_ISSUE_UNROLL)
    for _ in range(tm * TOP_K):
        row_copy(0, 0, 0).wait()

    gates = gate_ref[...]
    y = h_ref[...]
    for kk in range(TOP_K):
        words = [buf_ref[kk, pl.ds(j, tm, stride=V7X_SUBLANES), :] for j in range(V7X_SUBLANES)]
        pieces = []
        for n in range(V7X_SUBLANES // subs_per_half):
            chunk = words[n * subs_per_half:(n + 1) * subs_per_half]
            pieces += [_unpack_lo(u) for u in chunk] + [_unpack_hi(u) for u in chunk]
        y = y + gates[:, kk:kk + 1] * jnp.concatenate(pieces, axis=1)
    o_ref[...] = _rms(y, g_ref[...])


def _combine(h2, gates, pos, ys, g_final):
    t, d = h2.shape
    tm = min(COMBINE_TM, t)
    pos3 = pos.reshape(t // tm, 1, tm * TOP_K)
    return pl.pallas_call(
        _combine_kernel,
        grid=(t // tm,),
        in_specs=[
            pl.BlockSpec((1, 1, tm * TOP_K), lambda i: (i, 0, 0), memory_space=pltpu.SMEM),
            pl.BlockSpec((tm, d), lambda i: (i, 0)),
            pl.BlockSpec((tm, TOP_K), lambda i: (i, 0)),
            pl.BlockSpec((1, d), lambda i: (0, 0)),
            pl.BlockSpec(memory_space=pl.ANY),
        ],
        out_specs=pl.BlockSpec((tm, d), lambda i: (i, 0)),
        out_shape=jax.ShapeDtypeStruct((t, d), F32),
        scratch_shapes=[pltpu.VMEM((TOP_K, tm * V7X_SUBLANES, V7X_LANES), ys.dtype), pltpu.SemaphoreType.DMA(())],
        compiler_params=_cparams(("arbitrary",)),
        name="combine",
    )(pos3, h2, gates, g_final, ys)


def _routing_tables(counts, top_idx, rank, n_visits):
    r, ch = EXPERT_ROWS, EXPERT_CHUNK
    counts = counts.reshape(N_EXPERTS)
    padded = (counts + r - 1) // r * r
    ends = jnp.cumsum(padded)
    starts = ends - padded
    experts = jnp.arange(N_EXPERTS, dtype=jnp.int32)
    start_of = jnp.sum(jnp.where(top_idx[..., None] == experts, starts, 0), axis=-1)
    pos = start_of + rank
    visit_row0 = jnp.arange(n_visits, dtype=jnp.int32) * r
    owner = jnp.sum((ends[None, :] <= visit_row0[:, None]).astype(jnp.int32), axis=1)
    visit_expert = jnp.minimum(owner, N_EXPERTS - 1).astype(jnp.int32)
    visit_rows = jnp.clip(counts[visit_expert] - (visit_row0 - starts[visit_expert]), 0, r).astype(jnp.int32)
    n_used = (ends[-1] // r).astype(jnp.int32)
    last = jnp.maximum(n_used - 1, 0)
    ids = jnp.arange(n_visits, dtype=jnp.int32)
    active = ids < n_used
    visit_rows = jnp.where(active, visit_rows, 0)
    visit_blk = jnp.where(active, ids, last)
    visit_expert = jnp.where(active, visit_expert, visit_expert[last])
    tail = starts + jnp.minimum(counts // ch * ch, padded - ch)
    tail = jnp.where(counts > 0, tail, -1)
    return pos.astype(jnp.int32), visit_expert, visit_rows, visit_blk, tail.astype(jnp.int32), n_used


def kernel(x, mem, norm_mix_g, w_in, sinks, lambda_q1, lambda_k1, lambda_q2, lambda_k2, diff_subln_g, w_branch, w_o, norm_cross_g, norm_mem_g, w_cq, w_ckv, w_co, norm_ffn_g, w_router, b_router, w_gate_up, b_gate_up, w_down, b_down, norm_final_g):
    b, s, d = x.shape
    t = b * s
    x2 = x.reshape(t, d)
    l = 0

    w = w_in[l]
    o_qa, o_ka, o_va = 0, SWA_WIDTH, SWA_WIDTH + SWA_KV_WIDTH
    o_qd = o_va + SWA_KV_WIDTH
    o_kd, o_vd, o_g = o_qd + DIFF_WIDTH, o_qd + 2 * DIFF_WIDTH, o_qd + 3 * DIFF_WIDTH
    w_perm = jnp.concatenate([
        w[:, o_g:], w[:, o_qd:o_kd] * jnp.float32(HEAD_DIM ** -0.5 * LOG2E), w[:, o_kd:o_vd], w[:, o_vd:o_g],
        w[:, o_qa:o_ka], w[:, o_ka:o_va], w[:, o_va:o_qd]], axis=1).astype(BF16)
    proj = _inproj(x2, norm_mix_g[l].reshape(1, d), w_perm)
    c_qd = 2 * d
    c_kd, c_vd, c_qa = c_qd + DIFF_WIDTH, c_qd + 2 * DIFF_WIDTH, c_qd + 3 * DIFF_WIDTH
    c_ka, c_va = c_qa + SWA_WIDTH, c_qa + SWA_WIDTH + SWA_KV_WIDTH

    def seg(c0, width):
        return proj[:, c0:c0 + width].reshape(b, s, width)

    qaT = jnp.swapaxes(seg(c_qa, SWA_WIDTH), 1, 2)
    vaT = jnp.swapaxes(seg(c_va, SWA_KV_WIDTH), 1, 2)
    yaT = _swa(qaT, proj.reshape(b, s, proj.shape[1]), c_ka, vaT, sinks[l])
    ya = jnp.swapaxes(yaT, 1, 2).reshape(t, SWA_WIDTH)

    td = min(DIFF_T, s)
    nk = s // td
    hw = DIFF_HEAD_WIDTH
    cs, qx, kx = _diff_tables(td)
    qdT = jnp.swapaxes(seg(c_qd, DIFF_WIDTH), 1, 2)
    k4 = proj.reshape(b, nk, td, proj.shape[1])
    v5 = seg(c_vd, DIFF_WIDTH).reshape(b, nk, td, DIFF_HEADS, hw).transpose(0, 3, 1, 4, 2)
    ones = jnp.zeros((DIFF_DENOM_ROWS, td), BF16).at[0].set(1.0)
    lamv = jnp.stack([lambda_q1[l], lambda_k1[l], lambda_q2[l], lambda_k2[l]]).astype(F32)
    ydT = _diff(qdT, k4, c_kd, v5, qx, kx, ones, cs, lamv, diff_subln_g[l].astype(F32).reshape(hw, 1))
    yd = jnp.swapaxes(ydT, 1, 2).reshape(t, DIFF_WIDTH)

    h1 = _merge(ya, yd, proj, x2, w_branch[l].astype(BF16), w_o[l].astype(BF16))

    kmem, vmem = _memkv(mem, norm_mem_g[l].reshape(1, d), w_ckv[l].astype(BF16))
    h2, xn, top_idx, gates, rank, counts = _cross_router(
        h1, kmem, vmem, norm_cross_g[l].reshape(1, d), w_cq[l].astype(BF16), w_co[l].astype(BF16),
        norm_ffn_g[l].reshape(1, d), w_router[l].astype(BF16), b_router[l].reshape(1, N_EXPERTS), s)

    assert d // 2 == V7X_SUBLANES * V7X_LANES
    n_visits = (t * TOP_K) // EXPERT_ROWS + N_EXPERTS
    n_rows = n_visits * EXPERT_ROWS
    pos, visit_expert, visit_rows, visit_blk, tails, n_used = _routing_tables(counts, top_idx, rank, n_visits)
    xs = _dispatch(xn, pos, tails, n_rows)
    ys = _experts(xs, visit_expert, visit_rows, visit_blk, n_used, w_gate_up[l], b_gate_up[l], w_down[l], b_down[l])
    out = _combine(h2, gates, pos, ys, norm_final_g.reshape(1, d))
    return out.reshape(b, s, d)
```
